```python
import jax, jax.numpy as jnp
from jax import lax
import numpy as np

D_MODEL = 2048
BATCH = 16
SEQ = 256
DEPTH = 1
DEC_BATCH = 8
DEC_SEQ = 2048
PAST_LEN = 512

GRID_W = 64
N_HEADS = 4
D_MLSTM = D_MODEL // 2
HEAD_DIM = D_MLSTM // N_HEADS
N_FGROUPS = 4
D_FOURIER = D_MODEL // 2
FGROUP_DIM = D_FOURIER // N_FGROUPS
D_FF = 5632
CONV_W = 3
CHUNK = 64
N_DIR = 2
N_GATE = 2 * N_DIR * N_HEADS
D_IN = 4 * D_MLSTM + N_GATE + D_FOURIER + 2 * D_MODEL
ALPHA = (2.0 * DEPTH) ** 0.25
BETA = (8.0 * DEPTH) ** -0.25
LN_EPS = 1e-5

kernel_name = "hybrid_mlstm_fnet_convffn_diffusion_step"


def _layernorm(x, g=None, b=None):
    xf = x.astype(jnp.float32)
    mu = jnp.mean(xf, -1, keepdims=True)
    var = jnp.mean(jnp.square(xf - mu), -1, keepdims=True)
    y = (xf - mu) * lax.rsqrt(var + LN_EPS)
    if g is not None:
        y = y * g.astype(jnp.float32) + b.astype(jnp.float32)
    return y.astype(x.dtype)


def _grid_posemb(n_tokens):
    rows = n_tokens // GRID_W
    t = jnp.arange(rows * GRID_W)
    r = (t // GRID_W).astype(jnp.float32)[:, None]
    col = (t % GRID_W).astype(jnp.float32)[:, None]
    quarter = D_MODEL // 4
    freq = 1.0 / (10000.0 ** (jnp.arange(quarter, dtype=jnp.float32) / quarter))
    er, ec = r * freq, col * freq
    return jnp.concatenate([jnp.sin(er), jnp.cos(er), jnp.sin(ec), jnp.cos(ec)], -1)


def _mlstm_chunkwise(q, k, v, ig, lf, C0, n0, m0):
    f32 = jnp.float32
    B, H, T, dh = q.shape
    nc = T // CHUNK

    def to_chunks(a):
        a = a.astype(f32).reshape(a.shape[:2] + (nc, CHUNK) + a.shape[3:])
        return jnp.moveaxis(a, 2, 0)

    causal = jnp.tril(jnp.ones((CHUNK, CHUNK), dtype=bool))

    def step(carry, xs):
        C, n, m = carry
        qc, kc, vc, ic, fc = xs
        b = jnp.cumsum(fc, axis=-1)
        dmat = jnp.where(causal, b[..., :, None] - b[..., None, :] + ic[..., None, :], -jnp.inf)
        inter = b + m[..., None]
        m_t = jnp.maximum(inter, jnp.max(dmat, -1))
        w = jnp.exp(dmat - m_t[..., None])
        wi = jnp.exp(inter - m_t)
        s = jnp.einsum("bhtd,bhsd->bhts", qc, kc) * w
        num = jnp.einsum("bhts,bhsv->bhtv", s, vc) + wi[..., None] * jnp.einsum("bhtd,bhdv->bhtv", qc, C)
        den = jnp.sum(s, -1) + wi * jnp.einsum("bhtd,bhd->bht", qc, n)
        h = num / jnp.maximum(jnp.abs(den), jnp.exp(-m_t))[..., None]
        b_last = b[..., -1]
        g = b_last[..., None] - b + ic
        m_new = jnp.maximum(b_last + m, jnp.max(g, -1))
        wk = jnp.exp(g - m_new[..., None])
        decay = jnp.exp(b_last + m - m_new)
        C_new = decay[..., None, None] * C + jnp.einsum("bhs,bhsd,bhsv->bhdv", wk, kc, vc)
        n_new = decay[..., None] * n + jnp.einsum("bhs,bhsd->bhd", wk, kc)
        return (C_new, n_new, m_new), h

    xs = (to_chunks(q), to_chunks(k), to_chunks(v), to_chunks(ig), to_chunks(lf))
    (C, n, m), h = lax.scan(step, (C0.astype(f32), n0.astype(f32), m0.astype(f32)), xs)
    h = jnp.moveaxis(h, 0, 2).reshape(B, H, T, dh)
    return h, C, n, m


def _mlstm_bidir(q, k, v, gates, C0, n0, m0):
    B, T = gates.shape[:2]
    g = jnp.transpose(gates.reshape(B, T, N_DIR, 2, N_HEADS), (2, 3, 0, 4, 1))
    ig = g[:, 0]
    lf = jax.nn.log_sigmoid(g[:, 1])
    hf, Cf, nf, mf = _mlstm_chunkwise(q, k, v, ig[0], lf[0], C0[:, 0], n0[:, 0], m0[:, 0])
    hb, Cb, nb, mb = _mlstm_chunkwise(jnp.flip(q, 2), jnp.flip(k, 2), jnp.flip(v, 2),
                                      jnp.flip(ig[1], 2), jnp.flip(lf[1], 2),
                                      C0[:, 1], n0[:, 1], m0[:, 1])
    h = hf + jnp.flip(hb, 2)
    return h, jnp.stack([Cf, Cb], 1), jnp.stack([nf, nb], 1), jnp.stack([mf, mb], 1)


def _layer(x, cond, C0, n0, m0, w_ada, b_ada, w_in, b_gate, w_hnorm, w_br_m, w_br_f, w_out,
           ln1_g, ln1_b, w_up, w_conv, b_conv, w_down, ln2_g, ln2_b):
    B, T, _ = x.shape
    mod = jax.nn.silu(cond) @ w_ada + b_ada
    sh1, sc1, g1, sh2, sc2, g2 = [a[:, None, :] for a in jnp.split(mod, 6, axis=-1)]

    h = _layernorm(x) * (1.0 + sc1) + sh1
    p = h @ w_in
    splits = (D_MLSTM, 2 * D_MLSTM, 3 * D_MLSTM, 4 * D_MLSTM, 4 * D_MLSTM + N_GATE,
              4 * D_MLSTM + N_GATE + D_FOURIER, 4 * D_MLSTM + N_GATE + D_FOURIER + D_MODEL)
    q, k, v, o, gates, fr, ga, gb = jnp.split(p, splits, axis=-1)

    def heads(a):
        return a.reshape(B, T, N_HEADS, HEAD_DIM).transpose(0, 2, 1, 3)

    hm, C, n, m = _mlstm_bidir(heads(q) * HEAD_DIM ** -0.5, heads(k), heads(v),
                               (gates.astype(jnp.float32) + b_gate.astype(jnp.float32)), C0, n0, m0)
    hm = _layernorm(hm).transpose(0, 2, 1, 3) * w_hnorm.reshape(N_HEADS, HEAD_DIM)
    hm = jax.nn.sigmoid(o) * hm.reshape(B, T, D_MLSTM).astype(x.dtype)

    fr = fr.reshape(B, T, N_FGROUPS, FGROUP_DIM).astype(jnp.float32)
    fr = jnp.fft.fft2(fr, axes=(1, 3), norm="ortho").real.astype(x.dtype).reshape(B, T, D_FOURIER)

    mixed = jax.nn.sigmoid(ga) * (hm @ w_br_m) + jax.nn.sigmoid(gb) * (fr @ w_br_f)
    x = _layernorm(ALPHA * x + g1 * (mixed @ w_out), ln1_g, ln1_b)

    h = _layernorm(x) * (1.0 + sc2) + sh2
    u = h @ w_up
    up = jnp.pad(u, ((0, 0), (1, 1), (0, 0)))
    u = up[:, :-2] * w_conv[0] + up[:, 1:-1] * w_conv[1] + up[:, 2:] * w_conv[2] + b_conv
    val, gate = jnp.split(u, 2, axis=-1)
    y = (jax.nn.silu(gate) * val) @ w_down
    x = _layernorm(ALPHA * x + g2 * y, ln2_g, ln2_b)
    return x, C, n, m


def setup_inputs(seed: int = 0) -> dict:
    key = jax.random.key(seed)
    ks = jax.random.split(key, 24)
    f32 = jnp.float32
    nrm = lambda k, shape, s: jax.random.normal(k, shape, f32) * s
    fbias = jnp.linspace(3.0, 6.0, N_HEADS).astype(f32)
    gate_offset = jnp.stack([jnp.zeros((N_HEADS,), f32), fbias])[None, None]
    b_gate = (nrm(ks[10], (DEPTH, N_DIR, 2, N_HEADS), 0.1) + gate_offset).reshape(DEPTH, N_GATE)
    return {
        "x_prompt": nrm(ks[0], (BATCH, SEQ, D_MODEL), 1.0),
        "x_sample": nrm(ks[1], (DEC_BATCH, DEC_SEQ, D_MODEL), 1.0),
        "c": nrm(ks[2], (DEC_BATCH, D_MODEL), 1.0),
        "state_C": nrm(ks[3], (DEC_BATCH, DEPTH, N_DIR, N_HEADS, HEAD_DIM, HEAD_DIM), HEAD_DIM ** -0.5),
        "state_n": nrm(ks[4], (DEC_BATCH, DEPTH, N_DIR, N_HEADS, HEAD_DIM), HEAD_DIM ** -0.5),
        "state_m": nrm(ks[5], (DEC_BATCH, DEPTH, N_DIR, N_HEADS), 1.0),
        "c_ctx": nrm(ks[6], (D_MODEL,), 1.0),
        "w_ada": nrm(ks[7], (DEPTH, D_MODEL, 6 * D_MODEL), 0.5 * D_MODEL ** -0.5),
        "b_ada": nrm(ks[8], (DEPTH, 6 * D_MODEL), 0.01),
        "w_in": nrm(ks[9], (DEPTH, D_MODEL, D_IN), D_MODEL ** -0.5),
        "b_gate": b_gate,
        "w_hnorm": 1.0 + nrm(ks[11], (DEPTH, D_MLSTM), 0.02),
        "w_br_m": nrm(ks[12], (DEPTH, D_MLSTM, D_MODEL), D_MLSTM ** -0.5),
        "w_br_f": nrm(ks[13], (DEPTH, D_FOURIER, D_MODEL), D_FOURIER ** -0.5),
        "w_out": nrm(ks[14], (DEPTH, D_MODEL, D_MODEL), BETA * D_MODEL ** -0.5),
        "ln1_g": 1.0 + nrm(ks[15], (DEPTH, D_MODEL), 0.02),
        "ln1_b": nrm(ks[16], (DEPTH, D_MODEL), 0.02),
        "w_up": nrm(ks[17], (DEPTH, D_MODEL, 2 * D_FF), D_MODEL ** -0.5),
        "w_conv": nrm(ks[18], (DEPTH, CONV_W, 2 * D_FF), CONV_W ** -0.5),
        "b_conv": nrm(ks[19], (DEPTH, 2 * D_FF), 0.01),
        "w_down": nrm(ks[20], (DEPTH, D_FF, D_MODEL), BETA * D_FF ** -0.5),
        "ln2_g": 1.0 + nrm(ks[21], (DEPTH, D_MODEL), 0.02),
        "ln2_b": nrm(ks[22], (DEPTH, D_MODEL), 0.02),
    }


def reference(x_prompt, x_sample, c, state_C, state_n, state_m, c_ctx, w_ada, b_ada, w_in, b_gate,
              w_hnorm, w_br_m, w_br_f, w_out, ln1_g, ln1_b, w_up, w_conv, b_conv, w_down, ln2_g, ln2_b):
    f32 = jnp.float32
    B = x_prompt.shape[0]
    xp = x_prompt
    xs = x_sample + _grid_posemb(x_sample.shape[1]).astype(x_sample.dtype)[None]
    C0 = jnp.zeros((B, N_DIR, N_HEADS, HEAD_DIM, HEAD_DIM), f32)
    n0 = jnp.zeros((B, N_DIR, N_HEADS, HEAD_DIM), f32)
    m0 = jnp.zeros((B, N_DIR, N_HEADS), f32)
    Cs, ns, ms = [], [], []
    for l in range(DEPTH):
        params = (w_ada[l], b_ada[l], w_in[l], b_gate[l], w_hnorm[l], w_br_m[l], w_br_f[l], w_out[l],
                  ln1_g[l], ln1_b[l], w_up[l], w_conv[l], b_conv[l], w_down[l], ln2_g[l], ln2_b[l])
        xp, Cl, nl, ml = _layer(xp, c_ctx[None, :], C0, n0, m0, *params)
        Cs.append(Cl)
        ns.append(nl)
        ms.append(ml)
        xs, _, _, _ = _layer(xs, c, state_C[:, l], state_n[:, l], state_m[:, l], *params)
    new_C = jnp.stack(Cs, 1)
    new_n = jnp.stack(ns, 1)
    new_m = jnp.stack(ms, 1)
    return (xp, xs, new_C, new_n, new_m)
```

```python
import functools

import jax
import jax.numpy as jnp
from jax import lax
from jax.experimental import pallas as pl
from jax.experimental.pallas import tpu as pltpu

D_MODEL = 2048
N_HEADS = 4
HEAD_DIM = 256
D_MLSTM = N_HEADS * HEAD_DIM
N_FGROUPS = 4
FGROUP_DIM = 256
D_FOURIER = N_FGROUPS * FGROUP_DIM
D_FF = 5632
GRID_W = 64
N_DIR = 2
DEPTH = 1
ALPHA = (2.0 * DEPTH) ** 0.25
LN_EPS = 1e-5

F32 = jnp.float32
BF16 = jnp.bfloat16

CHUNK = 256
COL_GA = 0
COL_GB = D_MODEL
COL_Q = 2 * D_MODEL
COL_K = COL_Q + D_MLSTM
COL_V = COL_K + D_MLSTM
COL_O = COL_V + D_MLSTM
COL_FR = COL_O + D_MLSTM
D_MAIN = COL_FR + D_FOURIER
GATE_LANES = 128
HALO = 8

VMEM_LIMIT = 56 * 1024 * 1024


def _params(sem):
    return pltpu.CompilerParams(dimension_semantics=sem, vmem_limit_bytes=VMEM_LIMIT)


def _resident(shape, index_map):
    return pl.BlockSpec(shape, index_map, pipeline_mode=pl.Buffered(1))


def _norm(x):
    mu = jnp.mean(x, axis=-1, keepdims=True)
    xc = x - mu
    var = jnp.mean(xc * xc, axis=-1, keepdims=True)
    return xc * lax.rsqrt(var + LN_EPS)


def _dot(a, b):
    return jnp.dot(a, b, preferred_element_type=F32)


def _dot_nt(a, b):
    return lax.dot_general(a, b, (((1,), (1,)), ((), ())), preferred_element_type=F32)


def _dot_tn(a, b):
    return lax.dot_general(a, b, (((0,), (0,)), ((), ())), preferred_element_type=F32)


def _mod_kernel(c_ref, w_ref, b_ref, o_ref):
    c = c_ref[...]
    s = c * jax.nn.sigmoid(c)
    o_ref[...] = _dot(s.astype(BF16), w_ref[...].astype(BF16)) + b_ref[...]


def _modulation(cond, w_ada, b_ada):
    rows, tn = cond.shape[0], 1024
    n_out = w_ada.shape[1]
    return pl.pallas_call(
        _mod_kernel,
        grid=(n_out // tn,),
        in_specs=[
            _resident((rows, D_MODEL), lambda j: (0, 0)),
            pl.BlockSpec((D_MODEL, tn), lambda j: (0, j)),
            pl.BlockSpec((1, tn), lambda j: (0, j)),
        ],
        out_specs=pl.BlockSpec((rows, tn), lambda j: (0, j)),
        out_shape=jax.ShapeDtypeStruct((rows, n_out), F32),
        compiler_params=_params(("arbitrary",)),
        name="modulation",
    )(cond, w_ada, b_ada.reshape(1, n_out))


def _inproj_kernel(*refs, has_pos):
    if has_pos:
        x_ref, pos_ref, mod_ref, w_ref, wg_ref, bg_ref, p_ref, g_ref, h_scr = refs
    else:
        x_ref, mod_ref, w_ref, wg_ref, bg_ref, p_ref, g_ref, h_scr = refs

    @pl.when(pl.program_id(1) == 0)
    def _():
        x = x_ref[...]
        if has_pos:
            x = x + pos_ref[...]
        sh = mod_ref[0, 0:1, :]
        sc = mod_ref[0, 1:2, :]
        h = (_norm(x) * (1.0 + sc) + sh).astype(BF16)
        h_scr[...] = h
        g_ref[...] = _dot(h, wg_ref[...]) + bg_ref[...]

    p_ref[...] = _dot(h_scr[...], w_ref[...])


def _inproj(x2d, pos, mod6, w_main, w_gate, b_gate, seq_len, per_batch_mod):
    n_tok = x2d.shape[0]
    tm, tn = 512, 512
    tiles_per_seq = max(seq_len // tm, 1)
    has_pos = pos is not None
    mod_idx = (lambda i, j: (i // tiles_per_seq, 0, 0)) if per_batch_mod else (lambda i, j: (0, 0, 0))
    in_specs = [pl.BlockSpec((tm, D_MODEL), lambda i, j: (i, 0))]
    args = [x2d]
    if has_pos:
        in_specs.append(pl.BlockSpec((tm, D_MODEL), lambda i, j: (i % tiles_per_seq, 0)))
        args.append(pos)
    in_specs += [
        pl.BlockSpec((1, 6, D_MODEL), mod_idx),
        pl.BlockSpec((D_MODEL, tn), lambda i, j: (0, j)),
        _resident((D_MODEL, N_HEADS * GATE_LANES), lambda i, j: (0, 0)),
        _resident((1, N_HEADS * GATE_LANES), lambda i, j: (0, 0)),
    ]
    args += [mod6, w_main, w_gate, b_gate]
    return pl.pallas_call(
        functools.partial(_inproj_kernel, has_pos=has_pos),
        grid=(n_tok // tm, D_MAIN // tn),
        in_specs=in_specs,
        out_specs=[
            pl.BlockSpec((tm, tn), lambda i, j: (i, j)),
            pl.BlockSpec((tm, N_HEADS * GATE_LANES), lambda i, j: (i, 0)),
        ],
        out_shape=[
            jax.ShapeDtypeStruct((n_tok, D_MAIN), F32),
            jax.ShapeDtypeStruct((n_tok, N_HEADS * GATE_LANES), F32),
        ],
        scratch_shapes=[pltpu.VMEM((tm, D_MODEL), BF16)],
        compiler_params=_params(("parallel", "arbitrary")),
        name="inproj",
    )(*args)


def _log_sigmoid(x):
    return jnp.minimum(x, 0.0) - jnp.log1p(jnp.exp(-jnp.abs(x)))


def _scan_chunk(q32, k32, v32, ig_row, fg_row, ig_col, fg_col, C, n, m, fwd):
    L = CHUNK
    t_idx = lax.broadcasted_iota(jnp.int32, (L, L), 0)
    s_idx = lax.broadcasted_iota(jnp.int32, (L, L), 1)
    if fwd:
        mask = s_idx <= t_idx
        mask_t = t_idx <= s_idx
    else:
        mask = s_idx >= t_idx
        mask_t = t_idx >= s_idx
    lf_row = _log_sigmoid(fg_row)
    lf_col = _log_sigmoid(fg_col)
    b_col = jnp.sum(jnp.where(mask, lf_row, 0.0), axis=1, keepdims=True)
    b_row = jnp.sum(jnp.where(mask_t, lf_col, 0.0), axis=0, keepdims=True)
    dmat = jnp.where(mask, b_col - b_row + ig_row, -jnp.inf)
    m_col = jnp.maximum(b_col + m, jnp.max(dmat, axis=1, keepdims=True))
    w = jnp.exp(dmat - m_col)
    wi = jnp.exp(b_col + m - m_col)
    qb, kb, vb = q32.astype(BF16), k32.astype(BF16), v32.astype(BF16)
    s = _dot_nt(qb, kb) * w
    num = _dot(s.astype(BF16), vb) + wi * _dot(qb, C.astype(BF16))
    den = jnp.sum(s, axis=1, keepdims=True) + wi * jnp.sum(q32 * n, axis=1, keepdims=True)
    h = num * (1.0 / jnp.maximum(jnp.abs(den), jnp.exp(-m_col)))
    b_last = jnp.sum(lf_row, axis=1, keepdims=True)
    g_col = b_last - b_col + ig_col
    m_new = jnp.maximum(b_last + m, jnp.max(g_col, axis=0, keepdims=True))
    wk = jnp.exp(g_col - m_new)
    decay = jnp.exp(b_last + m - m_new)
    C_new = decay * C + _dot_tn(kb, (wk * v32).astype(BF16))
    n_new = decay * n + jnp.sum(wk * k32, axis=0, keepdims=True)
    return h, C_new, n_new, m_new


def _head_out(h, o, wn):
    return (jax.nn.sigmoid(o) * (_norm(h) * wn)).astype(BF16)


def _mlstm_kernel(*refs, seq_len, has_state, emit_state):
    refs = list(refs)
    q_ref, k_ref, v_ref, o_ref, g_ref, wn_ref = refs[:6]
    refs = refs[6:]
    if has_state:
        m0_ref, c0_ref, n0_ref = refs[:3]
        refs = refs[3:]
    hm_ref = refs[0]
    refs = refs[1:]
    if emit_state:
        cout_ref, nout_ref, mout_ref = refs[:3]
        refs = refs[3:]
    nc = seq_len // CHUNK
    L = CHUNK
    scale = HEAD_DIM ** -0.5

    def load_chunk(start):
        rows = pl.ds(start, L)
        q32 = q_ref[rows, :] * scale
        k32 = k_ref[rows, :]
        v32 = v_ref[rows, :]
        g_col = g_ref[rows, :]
        g_row = jnp.transpose(g_col)
        return q32, k32, v32, g_col, g_row

    def run_dir(chunk, d, C, n, m):
        q32, k32, v32, g_col, g_row = chunk
        return _scan_chunk(q32, k32, v32,
                           g_row[2 * d:2 * d + 1, :], g_row[2 * d + 1:2 * d + 2, :],
                           g_col[:, 2 * d:2 * d + 1], g_col[:, 2 * d + 1:2 * d + 2],
                           C, n, m, fwd=(d == 0))

    wn = wn_ref[0]

    if nc == 1:
        chunk = load_chunk(0)
        outs = []
        for d in range(N_DIR):
            if has_state:
                bh = pl.program_id(0) * (N_DIR * N_HEADS) + d * N_HEADS + pl.program_id(1)
                C0, n0, m0 = c0_ref[0, d, 0], n0_ref[0, d, 0], jnp.full((1, 1), m0_ref[bh], F32)
            else:
                C0 = jnp.zeros((HEAD_DIM, HEAD_DIM), F32)
                n0 = jnp.zeros((1, HEAD_DIM), F32)
                m0 = jnp.zeros((1, 1), F32)
            outs.append(run_dir(chunk, d, C0, n0, m0))
        hm_ref[...] = _head_out(outs[0][0] + outs[1][0], o_ref[...], wn)
        if emit_state:
            for d in range(N_DIR):
                cout_ref[0, 0, d, 0] = outs[d][1]
                nout_ref[0, 0, d, 0] = outs[d][2]
                mout_ref[0, d, 0] = jnp.broadcast_to(outs[d][3], (1, GATE_LANES))
        return

    c_scr, hf_scr, hb_scr = refs
    ns, ms = [], []
    for d in range(N_DIR):
        if has_state:
            bh = pl.program_id(0) * (N_DIR * N_HEADS) + d * N_HEADS + pl.program_id(1)
            c_scr[d] = c0_ref[0, d, 0]
            ns.append(n0_ref[0, d, 0])
            ms.append(jnp.full((1, 1), m0_ref[bh], F32))
        else:
            c_scr[d] = jnp.zeros((HEAD_DIM, HEAD_DIM), F32)
            ns.append(jnp.zeros((1, HEAD_DIM), F32))
            ms.append(jnp.zeros((1, 1), F32))

    def body(j, carry):
        nf, mf, nb, mb = carry
        start_f = pl.multiple_of(j * L, L)
        start_b = pl.multiple_of((nc - 1 - j) * L, L)
        hf, Cf, nf, mf = run_dir(load_chunk(start_f), 0, c_scr[0], nf, mf)
        c_scr[0] = Cf
        hf_scr[pl.ds(start_f, L), :] = hf
        hb, Cb, nb, mb = run_dir(load_chunk(start_b), 1, c_scr[1], nb, mb)
        c_scr[1] = Cb
        hb_scr[pl.ds(start_b, L), :] = hb
        return nf, mf, nb, mb

    nf, mf, nb, mb = lax.fori_loop(0, nc, body, (ns[0], ms[0], ns[1], ms[1]))

    def finish(j, carry):
        rows = pl.ds(pl.multiple_of(j * L, L), L)
        hm_ref[rows, :] = _head_out(hf_scr[rows, :] + hb_scr[rows, :], o_ref[rows, :], wn)
        return carry

    lax.fori_loop(0, nc, finish, 0)
    if emit_state:
        for d, (nd, md) in enumerate(((nf, mf), (nb, mb))):
            cout_ref[0, 0, d, 0] = c_scr[d]
            nout_ref[0, 0, d, 0] = nd
            mout_ref[0, d, 0] = jnp.broadcast_to(md, (1, GATE_LANES))


def _mlstm(p_main, gates, w_hnorm, batch, seq_len, state=None, emit_state=False):
    n_tok = p_main.shape[0]
    T = seq_len
    blk = lambda col: pl.BlockSpec((T, HEAD_DIM), lambda b, h: (b, col // HEAD_DIM + h))
    in_specs = [blk(COL_Q), blk(COL_K), blk(COL_V), blk(COL_O),
                pl.BlockSpec((T, GATE_LANES), lambda b, h: (b, h)),
                pl.BlockSpec((1, 1, HEAD_DIM), lambda b, h: (h, 0, 0))]
    args = [p_main, p_main, p_main, p_main, gates, w_hnorm.reshape(N_HEADS, 1, HEAD_DIM)]
    has_state = state is not None
    if has_state:
        C0, n0, m0 = state
        in_specs += [
            pl.BlockSpec(memory_space=pltpu.SMEM),
            pl.BlockSpec((1, N_DIR, 1, HEAD_DIM, HEAD_DIM), lambda b, h: (b, 0, h, 0, 0)),
            pl.BlockSpec((1, N_DIR, 1, 1, HEAD_DIM), lambda b, h: (b, 0, h, 0, 0)),
        ]
        args += [m0.reshape(-1), C0, n0.reshape(batch, N_DIR, N_HEADS, 1, HEAD_DIM)]
    out_specs = [pl.BlockSpec((T, HEAD_DIM), lambda b, h: (b, h))]
    out_shape = [jax.ShapeDtypeStruct((n_tok, D_MLSTM), BF16)]
    if emit_state:
        out_specs += [
            pl.BlockSpec((1, 1, N_DIR, 1, HEAD_DIM, HEAD_DIM), lambda b, h: (b, 0, 0, h, 0, 0)),
            pl.BlockSpec((1, 1, N_DIR, 1, 1, HEAD_DIM), lambda b, h: (b, 0, 0, h, 0, 0)),
            pl.BlockSpec((1, N_DIR, 1, 1, GATE_LANES), lambda b, h: (b, 0, h, 0, 0)),
        ]
        out_shape += [
            jax.ShapeDtypeStruct((batch, DEPTH, N_DIR, N_HEADS, HEAD_DIM, HEAD_DIM), F32),
            jax.ShapeDtypeStruct((batch, DEPTH, N_DIR, N_HEADS, 1, HEAD_DIM), F32),
            jax.ShapeDtypeStruct((batch, N_DIR, N_HEADS, 1, GATE_LANES), F32),
        ]
    scratch = []
    if T // CHUNK > 1:
        scratch = [pltpu.VMEM((N_DIR, HEAD_DIM, HEAD_DIM), F32),
                   pltpu.VMEM((T, HEAD_DIM), F32), pltpu.VMEM((T, HEAD_DIM), F32)]
    return pl.pallas_call(
        functools.partial(_mlstm_kernel, seq_len=T, has_state=has_state, emit_state=emit_state),
        grid=(batch, N_HEADS),
        in_specs=in_specs,
        out_specs=out_specs,
        out_shape=out_shape,
        scratch_shapes=scratch,
        compiler_params=_params(("parallel", "parallel")),
        name="mlstm",
    )(*args)


def _fourier_kernel(x_ref, cs_ref, ct_ref, st_ref, o_ref, *, scale):
    z = _dot(x_ref[...].astype(BF16), cs_ref[...])
    zc = z[:, :FGROUP_DIM].astype(BF16)
    zs = z[:, FGROUP_DIM:].astype(BF16)
    y = _dot(ct_ref[...], zc) - _dot(st_ref[...], zs)
    o_ref[...] = (y * scale).astype(BF16)


def _dft_tables(n):
    k = jnp.arange(n, dtype=jnp.int32)
    ang = ((k[:, None] * k[None, :]) % n).astype(F32) * (2.0 * jnp.pi / n)
    return jnp.cos(ang), jnp.sin(ang)


def _fourier(p_main, batch, seq_len):
    n_tok = p_main.shape[0]
    T = seq_len
    cc, sc = _dft_tables(FGROUP_DIM)
    cs = jnp.concatenate([cc, sc], axis=1).astype(BF16)
    ct, st = _dft_tables(T)
    return pl.pallas_call(
        functools.partial(_fourier_kernel, scale=float((T * FGROUP_DIM) ** -0.5)),
        grid=(batch, N_FGROUPS),
        in_specs=[
            pl.BlockSpec((T, FGROUP_DIM), lambda b, g: (b, COL_FR // FGROUP_DIM + g)),
            _resident((FGROUP_DIM, 2 * FGROUP_DIM), lambda b, g: (0, 0)),
            _resident((T, T), lambda b, g: (0, 0)),
            _resident((T, T), lambda b, g: (0, 0)),
        ],
        out_specs=pl.BlockSpec((T, FGROUP_DIM), lambda b, g: (b, g)),
        out_shape=jax.ShapeDtypeStruct((n_tok, D_FOURIER), BF16),
        compiler_params=_params(("parallel", "parallel")),
        name="fourier",
    )(p_main, cs, ct.astype(BF16), st.astype(BF16))


def _mix_kernel(*refs, has_pos):
    if has_pos:
        (hm_ref, fr_ref, ga_ref, gb_ref, x_ref, pos_ref, mod_ref, wm_ref, wf_ref, wo_ref,
         lg_ref, lb_ref, o_ref) = refs
    else:
        (hm_ref, fr_ref, ga_ref, gb_ref, x_ref, mod_ref, wm_ref, wf_ref, wo_ref,
         lg_ref, lb_ref, o_ref) = refs
    a = _dot(hm_ref[...], wm_ref[...])
    b = _dot(fr_ref[...], wf_ref[...])
    mixed = jax.nn.sigmoid(ga_ref[...]) * a + jax.nn.sigmoid(gb_ref[...]) * b
    z = _dot(mixed.astype(BF16), wo_ref[...])
    x = x_ref[...]
    if has_pos:
        x = x + pos_ref[...]
    g1 = mod_ref[0, 2:3, :]
    o_ref[...] = _norm(ALPHA * x + g1 * z) * lg_ref[...] + lb_ref[...]


def _mix(hm, fr, p_main, x2d, pos, mod6, w_br_m, w_br_f, w_out, ln_g, ln_b, seq_len, per_batch_mod):
    n_tok = x2d.shape[0]
    tm = 256
    tiles_per_seq = seq_len // tm
    has_pos = pos is not None
    mod_idx = (lambda i: (i // tiles_per_seq, 0, 0)) if per_batch_mod else (lambda i: (0, 0, 0))
    in_specs = [
        pl.BlockSpec((tm, D_MLSTM), lambda i: (i, 0)),
        pl.BlockSpec((tm, D_FOURIER), lambda i: (i, 0)),
        pl.BlockSpec((tm, D_MODEL), lambda i: (i, COL_GA // D_MODEL)),
        pl.BlockSpec((tm, D_MODEL), lambda i: (i, COL_GB // D_MODEL)),
        pl.BlockSpec((tm, D_MODEL), lambda i: (i, 0)),
    ]
    args = [hm, fr, p_main, p_main, x2d]
    if has_pos:
        in_specs.append(pl.BlockSpec((tm, D_MODEL), lambda i: (i % tiles_per_seq, 0)))
        args.append(pos)
    in_specs += [
        pl.BlockSpec((1, 6, D_MODEL), mod_idx),
        _resident((D_MLSTM, D_MODEL), lambda i: (0, 0)),
        _resident((D_FOURIER, D_MODEL), lambda i: (0, 0)),
        _resident((D_MODEL, D_MODEL), lambda i: (0, 0)),
        _resident((1, D_MODEL), lambda i: (0, 0)),
        _resident((1, D_MODEL), lambda i: (0, 0)),
    ]
    args += [mod6, w_br_m, w_br_f, w_out, ln_g.reshape(1, D_MODEL), ln_b.reshape(1, D_MODEL)]
    return pl.pallas_call(
        functools.partial(_mix_kernel, has_pos=has_pos),
        grid=(n_tok // tm,),
        in_specs=in_specs,
        out_specs=pl.BlockSpec((tm, D_MODEL), lambda i: (i, 0)),
        out_shape=jax.ShapeDtypeStruct((n_tok, D_MODEL), F32),
        compiler_params=_params(("parallel",)),
        name="mix",
    )(*args)


def _ffn_kernel(*refs, tm, has_halo, tiles_per_seq):
    if has_halo:
        (x_ref, xp_ref, xn_ref, mod_ref, wv_ref, wg_ref, cwv_ref, cwg_ref, cbv_ref, cbg_ref,
         wd_ref, lg_ref, lb_ref, o_ref, h_scr, acc_scr) = refs
    else:
        (x_ref, mod_ref, wv_ref, wg_ref, cwv_ref, cwg_ref, cbv_ref, cbg_ref,
         wd_ref, lg_ref, lb_ref, o_ref, h_scr, acc_scr) = refs
    f = pl.program_id(1)
    rows = tm + 2 * HALO

    @pl.when(f == 0)
    def _():
        sh = mod_ref[0, 3:4, :]
        sc = mod_ref[0, 4:5, :]
        modulate = lambda x: _norm(x) * (1.0 + sc) + sh
        h_scr[HALO:HALO + tm, :] = modulate(x_ref[...]).astype(BF16)
        if has_halo:
            t = pl.program_id(0) % tiles_per_seq
            hp = jnp.where(t == 0, 0.0, modulate(xp_ref[...]))
            hn = jnp.where(t == tiles_per_seq - 1, 0.0, modulate(xn_ref[...]))
            h_scr[0:HALO, :] = hp.astype(BF16)
            h_scr[HALO + tm:rows, :] = hn.astype(BF16)
        else:
            h_scr[0:HALO, :] = jnp.zeros((HALO, D_MODEL), BF16)
            h_scr[HALO + tm:rows, :] = jnp.zeros((HALO, D_MODEL), BF16)
        acc_scr[...] = jnp.zeros_like(acc_scr)

    h = h_scr[...]

    def conv(u, cw_ref, cb_ref):
        prev = pltpu.roll(u, 1, 0)
        nxt = pltpu.roll(u, rows - 1, 0)
        y = prev * cw_ref[0:1, :] + u * cw_ref[1:2, :] + nxt * cw_ref[2:3, :] + cb_ref[...]
        return y[HALO:HALO + tm, :]

    val = conv(_dot(h, wv_ref[...]), cwv_ref, cbv_ref)
    gate = conv(_dot(h, wg_ref[...]), cwg_ref, cbg_ref)
    act = (gate * jax.nn.sigmoid(gate) * val).astype(BF16)
    acc_scr[...] += _dot(act, wd_ref[...])

    @pl.when(f == pl.num_programs(1) - 1)
    def _():
        g2 = mod_ref[0, 5:6, :]
        o_ref[...] = _norm(ALPHA * x_ref[...] + g2 * acc_scr[...]) * lg_ref[...] + lb_ref[...]


def _ffn(x1, mod6, w_up, w_conv, b_conv, w_down, ln_g, ln_b, seq_len, per_batch_mod):
    n_tok = x1.shape[0]
    tf = 512
    nf = D_FF // tf
    tm = min(seq_len, 512)
    tiles_per_seq = seq_len // tm
    has_halo = tiles_per_seq > 1
    mod_idx = (lambda i, f: (i // tiles_per_seq, 0, 0)) if per_batch_mod else (lambda i, f: (0, 0, 0))
    hb = tm // HALO
    n_hblk = n_tok // HALO
    in_specs = [pl.BlockSpec((tm, D_MODEL), lambda i, f: (i, 0))]
    args = [x1]
    if has_halo:
        in_specs += [
            pl.BlockSpec((HALO, D_MODEL), lambda i, f: (jnp.maximum(i * hb - 1, 0), 0)),
            pl.BlockSpec((HALO, D_MODEL), lambda i, f: (jnp.minimum((i + 1) * hb, n_hblk - 1), 0)),
        ]
        args += [x1, x1]
    in_specs += [
        pl.BlockSpec((1, 6, D_MODEL), mod_idx),
        pl.BlockSpec((D_MODEL, tf), lambda i, f: (0, f)),
        pl.BlockSpec((D_MODEL, tf), lambda i, f: (0, nf + f)),
        pl.BlockSpec((3, tf), lambda i, f: (0, f)),
        pl.BlockSpec((3, tf), lambda i, f: (0, nf + f)),
        pl.BlockSpec((1, tf), lambda i, f: (0, f)),
        pl.BlockSpec((1, tf), lambda i, f: (0, nf + f)),
        pl.BlockSpec((tf, D_MODEL), lambda i, f: (f, 0)),
        _resident((1, D_MODEL), lambda i, f: (0, 0)),
        _resident((1, D_MODEL), lambda i, f: (0, 0)),
    ]
    b_conv2 = b_conv.reshape(1, 2 * D_FF)
    args += [mod6, w_up, w_up, w_conv, w_conv, b_conv2, b_conv2, w_down,
             ln_g.reshape(1, D_MODEL), ln_b.reshape(1, D_MODEL)]
    return pl.pallas_call(
        functools.partial(_ffn_kernel, tm=tm, has_halo=has_halo, tiles_per_seq=tiles_per_seq),
        grid=(n_tok // tm, nf),
        in_specs=in_specs,
        out_specs=pl.BlockSpec((tm, D_MODEL), lambda i, f: (i, 0)),
        out_shape=jax.ShapeDtypeStruct((n_tok, D_MODEL), F32),
        scratch_shapes=[pltpu.VMEM((tm + 2 * HALO, D_MODEL), BF16), pltpu.VMEM((tm, D_MODEL), F32)],
        compiler_params=_params(("parallel", "arbitrary")),
        name="ffn",
    )(*args)


def _grid_posemb(n_tokens):
    rows = n_tokens // GRID_W
    t = jnp.arange(rows * GRID_W)
    r = (t // GRID_W).astype(F32)[:, None]
    col = (t % GRID_W).astype(F32)[:, None]
    quarter = D_MODEL // 4
    freq = 1.0 / (10000.0 ** (jnp.arange(quarter, dtype=F32) / quarter))
    er, ec = r * freq, col * freq
    return jnp.concatenate([jnp.sin(er), jnp.cos(er), jnp.sin(ec), jnp.cos(ec)], -1)


def _relayout_w_in(w_in, b_gate):
    qkvo = w_in[:, :4 * D_MLSTM]
    gate_w = w_in[:, 4 * D_MLSTM:4 * D_MLSTM + 2 * N_DIR * N_HEADS]
    rest = w_in[:, 4 * D_MLSTM + 2 * N_DIR * N_HEADS:]
    fr = rest[:, :D_FOURIER]
    gab = rest[:, D_FOURIER:]
    w_main = jnp.concatenate([gab, qkvo, fr], axis=1).astype(BF16)
    gw = gate_w.reshape(D_MODEL, N_DIR, 2, N_HEADS).transpose(0, 3, 1, 2).reshape(D_MODEL, N_HEADS, 2 * N_DIR)
    gw = jnp.pad(gw, ((0, 0), (0, 0), (0, GATE_LANES - 2 * N_DIR))).reshape(D_MODEL, N_HEADS * GATE_LANES)
    gb = b_gate.astype(F32).reshape(N_DIR, 2, N_HEADS).transpose(2, 0, 1).reshape(N_HEADS, 2 * N_DIR)
    gb = jnp.pad(gb, ((0, 0), (0, GATE_LANES - 2 * N_DIR))).reshape(1, N_HEADS * GATE_LANES)
    return w_main, gw.astype(BF16), gb


def _layer(x2d, pos, mod6, weights, batch, seq_len, per_batch_mod, state, emit_state):
    (w_main, w_gate, b_gate, w_hnorm, w_br_m, w_br_f, w_out, ln1_g, ln1_b,
     w_up, w_conv, b_conv, w_down, ln2_g, ln2_b) = weights
    p_main, gates = _inproj(x2d, pos, mod6, w_main, w_gate, b_gate, seq_len, per_batch_mod)
    ml = _mlstm(p_main, gates, w_hnorm, batch, seq_len, state=state, emit_state=emit_state)
    fr = _fourier(p_main, batch, seq_len)
    x1 = _mix(ml[0], fr, p_main, x2d, pos, mod6, w_br_m, w_br_f, w_out, ln1_g, ln1_b,
              seq_len, per_batch_mod)
    x2 = _ffn(x1, mod6, w_up, w_conv, b_conv, w_down, ln2_g, ln2_b, seq_len, per_batch_mod)
    return x2, ml[1:]


def kernel(x_prompt, x_sample, c, state_C, state_n, state_m, c_ctx, w_ada, b_ada, w_in, b_gate,
           w_hnorm, w_br_m, w_br_f, w_out, ln1_g, ln1_b, w_up, w_conv, b_conv, w_down, ln2_g, ln2_b):
    assert w_ada.shape[0] == DEPTH
    B, S, _ = x_prompt.shape
    DB, DS, _ = x_sample.shape
    l = 0
    n_cond = 16
    cond = jnp.zeros((n_cond, D_MODEL), F32).at[0].set(c_ctx).at[1:1 + DB].set(c)
    mod6 = _modulation(cond, w_ada[l], b_ada[l]).reshape(n_cond, 6, D_MODEL)
    w_main, w_gate, b_gate_l = _relayout_w_in(w_in[l], b_gate[l])
    weights = (w_main, w_gate, b_gate_l, w_hnorm[l], w_br_m[l].astype(BF16), w_br_f[l].astype(BF16),
               w_out[l].astype(BF16), ln1_g[l], ln1_b[l], w_up[l].astype(BF16), w_conv[l], b_conv[l],
               w_down[l].astype(BF16), ln2_g[l], ln2_b[l])
    pos = _grid_posemb(DS)

    yp, states = _layer(x_prompt.reshape(B * S, D_MODEL), None, mod6[0:1], weights, B, S,
                        per_batch_mod=False, state=None, emit_state=True)
    ys, _ = _layer(x_sample.reshape(DB * DS, D_MODEL), pos, mod6[1:1 + DB], weights, DB, DS,
                   per_batch_mod=True, state=(state_C[:, l], state_n[:, l], state_m[:, l]),
                   emit_state=False)
    new_C, new_n, new_m = states
    new_n = new_n.reshape(B, DEPTH, N_DIR, N_HEADS, HEAD_DIM)
    new_m = new_m[:, :, :, 0, 0].reshape(B, DEPTH, N_DIR, N_HEADS)
    return (yp.reshape(B, S, D_MODEL), ys.reshape(DB, DS, D_MODEL), new_C, new_n, new_m)
```

```python
import functools

import jax
import jax.numpy as jnp
from jax import lax
from jax.experimental import pallas as pl
from jax.experimental.pallas import tpu as pltpu

D_MODEL = 2048
N_HEADS = 4
HEAD_DIM = 256
D_MLSTM = N_HEADS * HEAD_DIM
N_FGROUPS = 4
FGROUP_DIM = 256
D_FOURIER = N_FGROUPS * FGROUP_DIM
D_FF = 5632
GRID_W = 64
N_DIR = 2
DEPTH = 1
ALPHA = (2.0 * DEPTH) ** 0.25
LN_EPS = 1e-5

F32 = jnp.float32
BF16 = jnp.bfloat16

CHUNK = 256
COL_GA = 0
COL_GB = D_MODEL
COL_Q = 2 * D_MODEL
COL_K = COL_Q + D_MLSTM
COL_V = COL_K + D_MLSTM
COL_O = COL_V + D_MLSTM
COL_FR = COL_O + D_MLSTM
D_MAIN = COL_FR + D_FOURIER
GATE_LANES = 128
HALO = 8

VMEM_LIMIT = 56 * 1024 * 1024


def _params(sem):
    return pltpu.CompilerParams(dimension_semantics=sem, vmem_limit_bytes=VMEM_LIMIT)


def _resident(shape, index_map):
    return pl.BlockSpec(shape, index_map, pipeline_mode=pl.Buffered(1))


def _norm(x):
    mu = jnp.mean(x, axis=-1, keepdims=True)
    xc = x - mu
    var = jnp.mean(xc * xc, axis=-1, keepdims=True)
    return xc * lax.rsqrt(var + LN_EPS)


def _dot(a, b):
    return jnp.dot(a, b, preferred_element_type=F32)


def _dot_nt(a, b):
    return lax.dot_general(a, b, (((1,), (1,)), ((), ())), preferred_element_type=F32)


def _dot_tn(a, b):
    return lax.dot_general(a, b, (((0,), (0,)), ((), ())), preferred_element_type=F32)


def _mod_kernel(c_ref, w_ref, b_ref, o_ref):
    c = c_ref[...]
    s = c * jax.nn.sigmoid(c)
    o_ref[...] = _dot(s.astype(BF16), w_ref[...].astype(BF16)) + b_ref[...]


def _modulation(cond, w_ada, b_ada):
    rows, tn = cond.shape[0], 1024
    n_out = w_ada.shape[1]
    return pl.pallas_call(
        _mod_kernel,
        grid=(n_out // tn,),
        in_specs=[
            _resident((rows, D_MODEL), lambda j: (0, 0)),
            pl.BlockSpec((D_MODEL, tn), lambda j: (0, j)),
            pl.BlockSpec((1, tn), lambda j: (0, j)),
        ],
        out_specs=pl.BlockSpec((rows, tn), lambda j: (0, j)),
        out_shape=jax.ShapeDtypeStruct((rows, n_out), F32),
        compiler_params=_params(("arbitrary",)),
        name="modulation",
    )(cond, w_ada, b_ada.reshape(1, n_out))


INPROJ_TN = 512
N_TILES_A = 4 * D_MLSTM // INPROJ_TN
N_TILES_B = (D_FOURIER + 2 * D_MODEL) // INPROJ_TN
LN_ROWS = 256


def _inproj_kernel(*refs, tm, has_pos):
    if has_pos:
        x_ref, pos_ref, mod_ref, wa_ref, wb_ref, wg_ref, bg_ref, p_ref, g_ref, h_scr = refs
    else:
        x_ref, mod_ref, wa_ref, wb_ref, wg_ref, bg_ref, p_ref, g_ref, h_scr = refs
    j = pl.program_id(1)

    @pl.when(j == 0)
    def _():
        sh = mod_ref[0, 0:1, :]
        sc = mod_ref[0, 1:2, :]

        def ln_rows(r, carry):
            rows = pl.ds(pl.multiple_of(r * LN_ROWS, LN_ROWS), LN_ROWS)
            x = x_ref[rows, :]
            if has_pos:
                x = x + pos_ref[rows, :]
            h = (_norm(x) * (1.0 + sc) + sh).astype(BF16)
            h_scr[rows, :] = h
            g_ref[rows, :] = _dot(h, wg_ref[...]) + bg_ref[...]
            return carry

        lax.fori_loop(0, tm // LN_ROWS, ln_rows, 0)

    @pl.when(j < N_TILES_A)
    def _():
        p_ref[...] = _dot(h_scr[...], wa_ref[...]).astype(BF16)

    @pl.when(j >= N_TILES_A)
    def _():
        p_ref[...] = _dot(h_scr[...], wb_ref[...]).astype(BF16)


def _inproj_out_tile(j):
    n_gab = 2 * D_MODEL // INPROJ_TN
    n_fr = D_FOURIER // INPROJ_TN
    return jnp.where(j < N_TILES_A + n_fr, j + n_gab, j - (N_TILES_A + n_fr))


def _inproj(x2d, pos, mod6, w_a, w_b, w_gate, b_gate, seq_len, per_batch_mod):
    n_tok = x2d.shape[0]
    tm, tn = 1024, INPROJ_TN
    tiles_per_seq = max(seq_len // tm, 1)
    has_pos = pos is not None
    mod_idx = (lambda i, j: (i // tiles_per_seq, 0, 0)) if per_batch_mod else (lambda i, j: (0, 0, 0))
    in_specs = [pl.BlockSpec((tm, D_MODEL), lambda i, j: (i, 0))]
    args = [x2d]
    if has_pos:
        in_specs.append(pl.BlockSpec((tm, D_MODEL), lambda i, j: (i % tiles_per_seq, 0)))
        args.append(pos)
    in_specs += [
        pl.BlockSpec((1, 6, D_MODEL), mod_idx),
        pl.BlockSpec((D_MODEL, tn), lambda i, j: (0, jnp.minimum(j, N_TILES_A - 1))),
        pl.BlockSpec((D_MODEL, tn), lambda i, j: (0, jnp.maximum(j - N_TILES_A, 0))),
        _resident((D_MODEL, GATE_LANES), lambda i, j: (0, 0)),
        _resident((1, GATE_LANES), lambda i, j: (0, 0)),
    ]
    args += [mod6, w_a, w_b, w_gate, b_gate]
    return pl.pallas_call(
        functools.partial(_inproj_kernel, tm=tm, has_pos=has_pos),
        grid=(n_tok // tm, N_TILES_A + N_TILES_B),
        in_specs=in_specs,
        out_specs=[
            pl.BlockSpec((tm, tn), lambda i, j: (i, _inproj_out_tile(j))),
            pl.BlockSpec((tm, GATE_LANES), lambda i, j: (i, 0)),
        ],
        out_shape=[
            jax.ShapeDtypeStruct((n_tok, D_MAIN), BF16),
            jax.ShapeDtypeStruct((n_tok, GATE_LANES), F32),
        ],
        scratch_shapes=[pltpu.VMEM((tm, D_MODEL), BF16)],
        compiler_params=_params(("parallel", "arbitrary")),
        name="inproj",
    )(*args)


def _log_sigmoid(x):
    return jnp.minimum(x, 0.0) - jnp.log1p(jnp.exp(-jnp.abs(x)))


def _scan_chunk(qb, kb, vb, ig_row, fg_row, ig_col, fg_col, C, n, m, fwd):
    L = CHUNK
    t_idx = lax.broadcasted_iota(jnp.int32, (L, L), 0)
    s_idx = lax.broadcasted_iota(jnp.int32, (L, L), 1)
    if fwd:
        mask = s_idx <= t_idx
        mask_t = t_idx <= s_idx
    else:
        mask = s_idx >= t_idx
        mask_t = t_idx >= s_idx
    lf_row = _log_sigmoid(fg_row)
    lf_col = _log_sigmoid(fg_col)
    b_col = jnp.sum(jnp.where(mask, lf_row, 0.0), axis=1, keepdims=True)
    b_row = jnp.sum(jnp.where(mask_t, lf_col, 0.0), axis=0, keepdims=True)
    dmat = jnp.where(mask, b_col - b_row + ig_row, -jnp.inf)
    m_col = jnp.maximum(b_col + m, jnp.max(dmat, axis=1, keepdims=True))
    w = jnp.exp(dmat - m_col)
    wi = jnp.exp(b_col + m - m_col)
    s = _dot_nt(qb, kb) * w
    num = _dot(s.astype(BF16), vb) + wi * _dot(qb, C.astype(BF16))
    den = jnp.sum(s, axis=1, keepdims=True) + wi * jnp.sum(qb.astype(F32) * n, axis=1, keepdims=True)
    h = num * (1.0 / jnp.maximum(jnp.abs(den), jnp.exp(-m_col)))
    b_last = jnp.sum(lf_row, axis=1, keepdims=True)
    g_col = b_last - b_col + ig_col
    m_new = jnp.maximum(b_last + m, jnp.max(g_col, axis=0, keepdims=True))
    wk = jnp.exp(g_col - m_new)
    decay = jnp.exp(b_last + m - m_new)
    C_new = decay * C + _dot_tn(kb, (wk * vb.astype(F32)).astype(BF16))
    n_new = decay * n + jnp.sum(wk * kb.astype(F32), axis=0, keepdims=True)
    return h, C_new, n_new, m_new


def _head_out(h, o, wn):
    return (jax.nn.sigmoid(o.astype(F32)) * (_norm(h) * wn)).astype(BF16)


def _mlstm_kernel(*refs, seq_len, has_state, emit_state):
    refs = list(refs)
    q_ref, k_ref, v_ref, o_ref, g_ref, wn_ref = refs[:6]
    refs = refs[6:]
    if has_state:
        m0_ref, c0_ref, n0_ref = refs[:3]
        refs = refs[3:]
    hm_ref = refs[0]
    refs = refs[1:]
    if emit_state:
        cout_ref, nout_ref, mout_ref = refs[:3]
        refs = refs[3:]
    nc = seq_len // CHUNK
    L = CHUNK
    scale = HEAD_DIM ** -0.5

    head = pl.program_id(1)
    gate_shift = jnp.where(head == 0, 0, GATE_LANES - 2 * N_DIR * head)

    def load_chunk(start):
        rows = pl.ds(start, L)
        qb = (q_ref[rows, :].astype(F32) * scale).astype(BF16)
        g_col = pltpu.roll(g_ref[rows, :], gate_shift, 1)
        g_row = jnp.transpose(g_col)
        return qb, k_ref[rows, :], v_ref[rows, :], g_col, g_row

    def run_dir(chunk, d, C, n, m):
        qb, kb, vb, g_col, g_row = chunk
        return _scan_chunk(qb, kb, vb,
                           g_row[2 * d:2 * d + 1, :], g_row[2 * d + 1:2 * d + 2, :],
                           g_col[:, 2 * d:2 * d + 1], g_col[:, 2 * d + 1:2 * d + 2],
                           C, n, m, fwd=(d == 0))

    wn = wn_ref[0]

    if nc == 1:
        chunk = load_chunk(0)
        outs = []
        for d in range(N_DIR):
            if has_state:
                bh = pl.program_id(0) * (N_DIR * N_HEADS) + d * N_HEADS + pl.program_id(1)
                C0, n0, m0 = c0_ref[0, d, 0], n0_ref[0, d, 0], jnp.full((1, 1), m0_ref[bh], F32)
            else:
                C0 = jnp.zeros((HEAD_DIM, HEAD_DIM), F32)
                n0 = jnp.zeros((1, HEAD_DIM), F32)
                m0 = jnp.zeros((1, 1), F32)
            outs.append(run_dir(chunk, d, C0, n0, m0))
        hm_ref[...] = _head_out(outs[0][0] + outs[1][0], o_ref[...], wn)
        if emit_state:
            for d in range(N_DIR):
                cout_ref[0, 0, d, 0] = outs[d][1]
                nout_ref[0, 0, d, 0] = outs[d][2]
                mout_ref[0, d, 0] = jnp.broadcast_to(outs[d][3], (1, GATE_LANES))
        return

    c_scr, hf_scr, hb_scr = refs
    ns, ms = [], []
    for d in range(N_DIR):
        if has_state:
            bh = pl.program_id(0) * (N_DIR * N_HEADS) + d * N_HEADS + pl.program_id(1)
            c_scr[d] = c0_ref[0, d, 0]
            ns.append(n0_ref[0, d, 0])
            ms.append(jnp.full((1, 1), m0_ref[bh], F32))
        else:
            c_scr[d] = jnp.zeros((HEAD_DIM, HEAD_DIM), F32)
            ns.append(jnp.zeros((1, HEAD_DIM), F32))
            ms.append(jnp.zeros((1, 1), F32))

    def body(j, carry):
        nf, mf, nb, mb = carry
        start_f = pl.multiple_of(j * L, L)
        start_b = pl.multiple_of((nc - 1 - j) * L, L)
        hf, Cf, nf, mf = run_dir(load_chunk(start_f), 0, c_scr[0], nf, mf)
        c_scr[0] = Cf
        hf_scr[pl.ds(start_f, L), :] = hf
        hb, Cb, nb, mb = run_dir(load_chunk(start_b), 1, c_scr[1], nb, mb)
        c_scr[1] = Cb
        hb_scr[pl.ds(start_b, L), :] = hb
        return nf, mf, nb, mb

    nf, mf, nb, mb = lax.fori_loop(0, nc, body, (ns[0], ms[0], ns[1], ms[1]))

    def finish(j, carry):
        rows = pl.ds(pl.multiple_of(j * L, L), L)
        hm_ref[rows, :] = _head_out(hf_scr[rows, :] + hb_scr[rows, :], o_ref[rows, :], wn)
        return carry

    lax.fori_loop(0, nc, finish, 0)
    if emit_state:
        for d, (nd, md) in enumerate(((nf, mf), (nb, mb))):
            cout_ref[0, 0, d, 0] = c_scr[d]
            nout_ref[0, 0, d, 0] = nd
            mout_ref[0, d, 0] = jnp.broadcast_to(md, (1, GATE_LANES))


def _mlstm(p_main, gates, w_hnorm, batch, seq_len, state=None, emit_state=False):
    n_tok = p_main.shape[0]
    T = seq_len
    blk = lambda col: pl.BlockSpec((T, HEAD_DIM), lambda b, h: (b, col // HEAD_DIM + h))
    in_specs = [blk(COL_Q), blk(COL_K), blk(COL_V), blk(COL_O),
                pl.BlockSpec((T, GATE_LANES), lambda b, h: (b, 0)),
                pl.BlockSpec((1, 1, HEAD_DIM), lambda b, h: (h, 0, 0))]
    args = [p_main, p_main, p_main, p_main, gates, w_hnorm.reshape(N_HEADS, 1, HEAD_DIM)]
    has_state = state is not None
    if has_state:
        C0, n0, m0 = state
        in_specs += [
            pl.BlockSpec(memory_space=pltpu.SMEM),
            pl.BlockSpec((1, N_DIR, 1, HEAD_DIM, HEAD_DIM), lambda b, h: (b, 0, h, 0, 0)),
            pl.BlockSpec((1, N_DIR, 1, 1, HEAD_DIM), lambda b, h: (b, 0, h, 0, 0)),
        ]
        args += [m0.reshape(-1), C0, n0.reshape(batch, N_DIR, N_HEADS, 1, HEAD_DIM)]
    out_specs = [pl.BlockSpec((T, HEAD_DIM), lambda b, h: (b, h))]
    out_shape = [jax.ShapeDtypeStruct((n_tok, D_MLSTM), BF16)]
    if emit_state:
        out_specs += [
            pl.BlockSpec((1, 1, N_DIR, 1, HEAD_DIM, HEAD_DIM), lambda b, h: (b, 0, 0, h, 0, 0)),
            pl.BlockSpec((1, 1, N_DIR, 1, 1, HEAD_DIM), lambda b, h: (b, 0, 0, h, 0, 0)),
            pl.BlockSpec((1, N_DIR, 1, 1, GATE_LANES), lambda b, h: (b, 0, h, 0, 0)),
        ]
        out_shape += [
            jax.ShapeDtypeStruct((batch, DEPTH, N_DIR, N_HEADS, HEAD_DIM, HEAD_DIM), F32),
            jax.ShapeDtypeStruct((batch, DEPTH, N_DIR, N_HEADS, 1, HEAD_DIM), F32),
            jax.ShapeDtypeStruct((batch, N_DIR, N_HEADS, 1, GATE_LANES), F32),
        ]
    scratch = []
    if T // CHUNK > 1:
        scratch = [pltpu.VMEM((N_DIR, HEAD_DIM, HEAD_DIM), F32),
                   pltpu.VMEM((T, HEAD_DIM), F32), pltpu.VMEM((T, HEAD_DIM), F32)]
    return pl.pallas_call(
        functools.partial(_mlstm_kernel, seq_len=T, has_state=has_state, emit_state=emit_state),
        grid=(batch, N_HEADS),
        in_specs=in_specs,
        out_specs=out_specs,
        out_shape=out_shape,
        scratch_shapes=scratch,
        compiler_params=_params(("parallel", "parallel")),
        name="mlstm",
    )(*args)


def _fourier_kernel(x_ref, cs_ref, ct_ref, st_ref, o_ref, *, scale):
    z = _dot(x_ref[...].astype(BF16), cs_ref[...])
    zc = z[:, :FGROUP_DIM].astype(BF16)
    zs = z[:, FGROUP_DIM:].astype(BF16)
    y = _dot(ct_ref[...], zc) - _dot(st_ref[...], zs)
    o_ref[...] = (y * scale).astype(BF16)


def _dft_tables(n):
    k = jnp.arange(n, dtype=jnp.int32)
    ang = ((k[:, None] * k[None, :]) % n).astype(F32) * (2.0 * jnp.pi / n)
    return jnp.cos(ang), jnp.sin(ang)


def _fourier(p_main, batch, seq_len):
    n_tok = p_main.shape[0]
    T = seq_len
    cc, sc = _dft_tables(FGROUP_DIM)
    cs = jnp.concatenate([cc, sc], axis=1).astype(BF16)
    ct, st = _dft_tables(T)
    return pl.pallas_call(
        functools.partial(_fourier_kernel, scale=float((T * FGROUP_DIM) ** -0.5)),
        grid=(batch, N_FGROUPS),
        in_specs=[
            pl.BlockSpec((T, FGROUP_DIM), lambda b, g: (b, COL_FR // FGROUP_DIM + g)),
            _resident((FGROUP_DIM, 2 * FGROUP_DIM), lambda b, g: (0, 0)),
            _resident((T, T), lambda b, g: (0, 0)),
            _resident((T, T), lambda b, g: (0, 0)),
        ],
        out_specs=pl.BlockSpec((T, FGROUP_DIM), lambda b, g: (b, g)),
        out_shape=jax.ShapeDtypeStruct((n_tok, D_FOURIER), BF16),
        compiler_params=_params(("parallel", "parallel")),
        name="fourier",
    )(p_main, cs, ct.astype(BF16), st.astype(BF16))


def _mix_kernel(*refs, has_pos):
    if has_pos:
        (hm_ref, fr_ref, ga_ref, gb_ref, x_ref, pos_ref, mod_ref, wm_ref, wf_ref, wo_ref,
         lg_ref, lb_ref, o_ref) = refs
    else:
        (hm_ref, fr_ref, ga_ref, gb_ref, x_ref, mod_ref, wm_ref, wf_ref, wo_ref,
         lg_ref, lb_ref, o_ref) = refs
    a = _dot(hm_ref[...], wm_ref[...])
    b = _dot(fr_ref[...], wf_ref[...])
    mixed = (jax.nn.sigmoid(ga_ref[...].astype(F32)) * a
             + jax.nn.sigmoid(gb_ref[...].astype(F32)) * b)
    z = _dot(mixed.astype(BF16), wo_ref[...])
    x = x_ref[...]
    if has_pos:
        x = x + pos_ref[...]
    g1 = mod_ref[0, 2:3, :]
    o_ref[...] = _norm(ALPHA * x + g1 * z) * lg_ref[...] + lb_ref[...]


def _mix(hm, fr, p_main, x2d, pos, mod6, w_br_m, w_br_f, w_out, ln_g, ln_b, seq_len, per_batch_mod):
    n_tok = x2d.shape[0]
    tm = 256
    tiles_per_seq = seq_len // tm
    has_pos = pos is not None
    mod_idx = (lambda i: (i // tiles_per_seq, 0, 0)) if per_batch_mod else (lambda i: (0, 0, 0))
    in_specs = [
        pl.BlockSpec((tm, D_MLSTM), lambda i: (i, 0)),
        pl.BlockSpec((tm, D_FOURIER), lambda i: (i, 0)),
        pl.BlockSpec((tm, D_MODEL), lambda i: (i, COL_GA // D_MODEL)),
        pl.BlockSpec((tm, D_MODEL), lambda i: (i, COL_GB // D_MODEL)),
        pl.BlockSpec((tm, D_MODEL), lambda i: (i, 0)),
    ]
    args = [hm, fr, p_main, p_main, x2d]
    if has_pos:
        in_specs.append(pl.BlockSpec((tm, D_MODEL), lambda i: (i % tiles_per_seq, 0)))
        args.append(pos)
    in_specs += [
        pl.BlockSpec((1, 6, D_MODEL), mod_idx),
        _resident((D_MLSTM, D_MODEL), lambda i: (0, 0)),
        _resident((D_FOURIER, D_MODEL), lambda i: (0, 0)),
        _resident((D_MODEL, D_MODEL), lambda i: (0, 0)),
        _resident((1, D_MODEL), lambda i: (0, 0)),
        _resident((1, D_MODEL), lambda i: (0, 0)),
    ]
    args += [mod6, w_br_m, w_br_f, w_out, ln_g.reshape(1, D_MODEL), ln_b.reshape(1, D_MODEL)]
    return pl.pallas_call(
        functools.partial(_mix_kernel, has_pos=has_pos),
        grid=(n_tok // tm,),
        in_specs=in_specs,
        out_specs=pl.BlockSpec((tm, D_MODEL), lambda i: (i, 0)),
        out_shape=jax.ShapeDtypeStruct((n_tok, D_MODEL), F32),
        compiler_params=_params(("parallel",)),
        name="mix",
    )(*args)


def _ffn_kernel(*refs, tm, has_halo, tiles_per_seq):
    if has_halo:
        (x_ref, xp_ref, xn_ref, mod_ref, wv_ref, wg_ref, cwv_ref, cwg_ref, cbv_ref, cbg_ref,
         wd_ref, lg_ref, lb_ref, o_ref, h_scr, acc_scr) = refs
    else:
        (x_ref, mod_ref, wv_ref, wg_ref, cwv_ref, cwg_ref, cbv_ref, cbg_ref,
         wd_ref, lg_ref, lb_ref, o_ref, h_scr, acc_scr) = refs
    f = pl.program_id(1)
    rows = tm + 2 * HALO

    @pl.when(f == 0)
    def _():
        sh = mod_ref[0, 3:4, :]
        sc = mod_ref[0, 4:5, :]
        modulate = lambda x: _norm(x) * (1.0 + sc) + sh
        h_scr[HALO:HALO + tm, :] = modulate(x_ref[...]).astype(BF16)
        if has_halo:
            t = pl.program_id(0) % tiles_per_seq
            hp = jnp.where(t == 0, 0.0, modulate(xp_ref[...]))
            hn = jnp.where(t == tiles_per_seq - 1, 0.0, modulate(xn_ref[...]))
            h_scr[0:HALO, :] = hp.astype(BF16)
            h_scr[HALO + tm:rows, :] = hn.astype(BF16)
        else:
            h_scr[0:HALO, :] = jnp.zeros((HALO, D_MODEL), BF16)
            h_scr[HALO + tm:rows, :] = jnp.zeros((HALO, D_MODEL), BF16)
        acc_scr[...] = jnp.zeros_like(acc_scr)

    h = h_scr[...]

    def conv(u, cw_ref, cb_ref):
        prev = pltpu.roll(u, 1, 0)
        nxt = pltpu.roll(u, rows - 1, 0)
        y = prev * cw_ref[0:1, :] + u * cw_ref[1:2, :] + nxt * cw_ref[2:3, :] + cb_ref[...]
        return y[HALO:HALO + tm, :]

    val = conv(_dot(h, wv_ref[...]), cwv_ref, cbv_ref)
    gate = conv(_dot(h, wg_ref[...]), cwg_ref, cbg_ref)
    act = (gate * jax.nn.sigmoid(gate) * val).astype(BF16)
    acc_scr[...] += _dot(act, wd_ref[...])

    @pl.when(f == pl.num_programs(1) - 1)
    def _():
        g2 = mod_ref[0, 5:6, :]
        o_ref[...] = _norm(ALPHA * x_ref[...] + g2 * acc_scr[...]) * lg_ref[...] + lb_ref[...]


def _ffn(x1, mod6, w_up, w_conv, b_conv, w_down, ln_g, ln_b, seq_len, per_batch_mod):
    n_tok = x1.shape[0]
    tf = 512
    nf = D_FF // tf
    tm = min(seq_len, 512)
    tiles_per_seq = seq_len // tm
    has_halo = tiles_per_seq > 1
    mod_idx = (lambda i, f: (i // tiles_per_seq, 0, 0)) if per_batch_mod else (lambda i, f: (0, 0, 0))
    hb = tm // HALO
    n_hblk = n_tok // HALO
    in_specs = [pl.BlockSpec((tm, D_MODEL), lambda i, f: (i, 0))]
    args = [x1]
    if has_halo:
        in_specs += [
            pl.BlockSpec((HALO, D_MODEL), lambda i, f: (jnp.maximum(i * hb - 1, 0), 0)),
            pl.BlockSpec((HALO, D_MODEL), lambda i, f: (jnp.minimum((i + 1) * hb, n_hblk - 1), 0)),
        ]
        args += [x1, x1]
    in_specs += [
        pl.BlockSpec((1, 6, D_MODEL), mod_idx),
        pl.BlockSpec((D_MODEL, tf), lambda i, f: (0, f)),
        pl.BlockSpec((D_MODEL, tf), lambda i, f: (0, nf + f)),
        pl.BlockSpec((3, tf), lambda i, f: (0, f)),
        pl.BlockSpec((3, tf), lambda i, f: (0, nf + f)),
        pl.BlockSpec((1, tf), lambda i, f: (0, f)),
        pl.BlockSpec((1, tf), lambda i, f: (0, nf + f)),
        pl.BlockSpec((tf, D_MODEL), lambda i, f: (f, 0)),
        _resident((1, D_MODEL), lambda i, f: (0, 0)),
        _resident((1, D_MODEL), lambda i, f: (0, 0)),
    ]
    b_conv2 = b_conv.reshape(1, 2 * D_FF)
    args += [mod6, w_up, w_up, w_conv, w_conv, b_conv2, b_conv2, w_down,
             ln_g.reshape(1, D_MODEL), ln_b.reshape(1, D_MODEL)]
    return pl.pallas_call(
        functools.partial(_ffn_kernel, tm=tm, has_halo=has_halo, tiles_per_seq=tiles_per_seq),
        grid=(n_tok // tm, nf),
        in_specs=in_specs,
        out_specs=pl.BlockSpec((tm, D_MODEL), lambda i, f: (i, 0)),
        out_shape=jax.ShapeDtypeStruct((n_tok, D_MODEL), F32),
        scratch_shapes=[pltpu.VMEM((tm + 2 * HALO, D_MODEL), BF16), pltpu.VMEM((tm, D_MODEL), F32)],
        compiler_params=_params(("parallel", "arbitrary")),
        name="ffn",
    )(*args)


def _grid_posemb(n_tokens):
    rows = n_tokens // GRID_W
    t = jnp.arange(rows * GRID_W)
    r = (t // GRID_W).astype(F32)[:, None]
    col = (t % GRID_W).astype(F32)[:, None]
    quarter = D_MODEL // 4
    freq = 1.0 / (10000.0 ** (jnp.arange(quarter, dtype=F32) / quarter))
    er, ec = r * freq, col * freq
    return jnp.concatenate([jnp.sin(er), jnp.cos(er), jnp.sin(ec), jnp.cos(ec)], -1)


def _split_w_in(w_in, b_gate):
    n_gate = 2 * N_DIR * N_HEADS
    w_a = w_in[:, :4 * D_MLSTM].astype(BF16)
    w_b = w_in[:, 4 * D_MLSTM + n_gate:].astype(BF16)
    gw = w_in[:, 4 * D_MLSTM:4 * D_MLSTM + n_gate].reshape(D_MODEL, N_DIR, 2, N_HEADS)
    gw = gw.transpose(0, 3, 1, 2).reshape(D_MODEL, n_gate)
    gw = jnp.pad(gw, ((0, 0), (0, GATE_LANES - n_gate))).astype(BF16)
    gb = b_gate.astype(F32).reshape(N_DIR, 2, N_HEADS).transpose(2, 0, 1).reshape(1, n_gate)
    gb = jnp.pad(gb, ((0, 0), (0, GATE_LANES - n_gate)))
    return w_a, w_b, gw, gb


def _layer(x2d, pos, mod6, weights, batch, seq_len, per_batch_mod, state, emit_state):
    (w_a, w_b, w_gate, b_gate, w_hnorm, w_br_m, w_br_f, w_out, ln1_g, ln1_b,
     w_up, w_conv, b_conv, w_down, ln2_g, ln2_b) = weights
    p_main, gates = _inproj(x2d, pos, mod6, w_a, w_b, w_gate, b_gate, seq_len, per_batch_mod)
    ml = _mlstm(p_main, gates, w_hnorm, batch, seq_len, state=state, emit_state=emit_state)
    fr = _fourier(p_main, batch, seq_len)
    x1 = _mix(ml[0], fr, p_main, x2d, pos, mod6, w_br_m, w_br_f, w_out, ln1_g, ln1_b,
              seq_len, per_batch_mod)
    x2 = _ffn(x1, mod6, w_up, w_conv, b_conv, w_down, ln2_g, ln2_b, seq_len, per_batch_mod)
    return x2, ml[1:]


def kernel(x_prompt, x_sample, c, state_C, state_n, state_m, c_ctx, w_ada, b_ada, w_in, b_gate,
           w_hnorm, w_br_m, w_br_f, w_out, ln1_g, ln1_b, w_up, w_conv, b_conv, w_down, ln2_g, ln2_b):
    assert w_ada.shape[0] == DEPTH
    B, S, _ = x_prompt.shape
    DB, DS, _ = x_sample.shape
    l = 0
    n_cond = 16
    cond = jnp.zeros((n_cond, D_MODEL), F32).at[0].set(c_ctx).at[1:1 + DB].set(c)
    mod6 = _modulation(cond, w_ada[l], b_ada[l]).reshape(n_cond, 6, D_MODEL)
    w_a, w_b, w_gate, b_gate_l = _split_w_in(w_in[l], b_gate[l])
    weights = (w_a, w_b, w_gate, b_gate_l, w_hnorm[l], w_br_m[l].astype(BF16), w_br_f[l].astype(BF16),
               w_out[l].astype(BF16), ln1_g[l], ln1_b[l], w_up[l].astype(BF16), w_conv[l], b_conv[l],
               w_down[l].astype(BF16), ln2_g[l], ln2_b[l])
    pos = _grid_posemb(DS)

    yp, states = _layer(x_prompt.reshape(B * S, D_MODEL), None, mod6[0:1], weights, B, S,
                        per_batch_mod=False, state=None, emit_state=True)
    ys, _ = _layer(x_sample.reshape(DB * DS, D_MODEL), pos, mod6[1:1 + DB], weights, DB, DS,
                   per_batch_mod=True, state=(state_C[:, l], state_n[:, l], state_m[:, l]),
                   emit_state=False)
    new_C, new_n, new_m = states
    new_n = new_n.reshape(B, DEPTH, N_DIR, N_HEADS, HEAD_DIM)
    new_m = new_m[:, :, :, 0, 0].reshape(B, DEPTH, N_DIR, N_HEADS)
    return (yp.reshape(B, S, D_MODEL), ys.reshape(DB, DS, D_MODEL), new_C, new_n, new_m)
```

```python
import functools

import jax
import jax.numpy as jnp
from jax import lax
from jax.experimental import pallas as pl
from jax.experimental.pallas import tpu as pltpu

D_MODEL = 2048
N_HEADS = 4
HEAD_DIM = 256
D_MLSTM = N_HEADS * HEAD_DIM
N_FGROUPS = 4
FGROUP_DIM = 256
D_FOURIER = N_FGROUPS * FGROUP_DIM
D_FF = 5632
GRID_W = 64
N_DIR = 2
DEPTH = 1
ALPHA = (2.0 * DEPTH) ** 0.25
LN_EPS = 1e-5

F32 = jnp.float32
BF16 = jnp.bfloat16

CHUNK = 256
COL_GA = 0
COL_GB = D_MODEL
COL_Q = 2 * D_MODEL
COL_K = COL_Q + D_MLSTM
COL_V = COL_K + D_MLSTM
COL_O = COL_V + D_MLSTM
COL_FR = COL_O + D_MLSTM
D_MAIN = COL_FR + D_FOURIER
GATE_LANES = 128
HALO = 8
FFN_SUBTILES = 2

VMEM_LIMIT = 56 * 1024 * 1024


def _params(sem, flags=None):
    return pltpu.CompilerParams(dimension_semantics=sem, vmem_limit_bytes=VMEM_LIMIT, flags=flags)


def _resident(shape, index_map):
    return pl.BlockSpec(shape, index_map, pipeline_mode=pl.Buffered(1))


def _norm(x):
    mu = jnp.mean(x, axis=-1, keepdims=True)
    xc = x - mu
    var = jnp.mean(xc * xc, axis=-1, keepdims=True)
    return xc * lax.rsqrt(var + LN_EPS)


def _dot(a, b):
    return jnp.dot(a, b, preferred_element_type=F32)


def _dot_nt(a, b):
    return lax.dot_general(a, b, (((1,), (1,)), ((), ())), preferred_element_type=F32)


def _mod_kernel(c_ref, w_ref, b_ref, o_ref):
    c = c_ref[...]
    s = c * jax.nn.sigmoid(c)
    o_ref[...] = _dot(s.astype(BF16), w_ref[...].astype(BF16)) + b_ref[...]


def _modulation(cond, w_ada, b_ada):
    rows, tn = cond.shape[0], 1024
    n_out = w_ada.shape[1]
    return pl.pallas_call(
        _mod_kernel,
        grid=(n_out // tn,),
        in_specs=[
            _resident((rows, D_MODEL), lambda j: (0, 0)),
            pl.BlockSpec((D_MODEL, tn), lambda j: (0, j)),
            pl.BlockSpec((1, tn), lambda j: (0, j)),
        ],
        out_specs=pl.BlockSpec((rows, tn), lambda j: (0, j)),
        out_shape=jax.ShapeDtypeStruct((rows, n_out), F32),
        compiler_params=_params(("arbitrary",)),
        name="modulation",
    )(cond, w_ada, b_ada.reshape(1, n_out))


INPROJ_TN = 512
N_TILES_A = 4 * D_MLSTM // INPROJ_TN
N_TILES_B = (D_FOURIER + 2 * D_MODEL) // INPROJ_TN
LN_ROWS = 256


def _inproj_kernel(*refs, tm, has_pos):
    if has_pos:
        x_ref, pos_ref, mod_ref, wa_ref, wb_ref, wg_ref, bg_ref, p_ref, g_ref, h_scr = refs
    else:
        x_ref, mod_ref, wa_ref, wb_ref, wg_ref, bg_ref, p_ref, g_ref, h_scr = refs
    j = pl.program_id(1)

    @pl.when(j == 0)
    def _():
        sh = mod_ref[0, 0:1, :]
        sc = mod_ref[0, 1:2, :]

        def ln_rows(r, carry):
            rows = pl.ds(pl.multiple_of(r * LN_ROWS, LN_ROWS), LN_ROWS)
            x = x_ref[rows, :]
            if has_pos:
                x = x + pos_ref[rows, :]
            h = (_norm(x) * (1.0 + sc) + sh).astype(BF16)
            h_scr[rows, :] = h
            g_ref[rows, :] = _dot(h, wg_ref[...]) + bg_ref[...]
            return carry

        lax.fori_loop(0, tm // LN_ROWS, ln_rows, 0)

    @pl.when(j < N_TILES_A)
    def _():
        p_ref[...] = _dot(h_scr[...], wa_ref[...]).astype(BF16)

    @pl.when(j >= N_TILES_A)
    def _():
        p_ref[...] = _dot(h_scr[...], wb_ref[...]).astype(BF16)


def _inproj_out_tile(j):
    n_gab = 2 * D_MODEL // INPROJ_TN
    n_fr = D_FOURIER // INPROJ_TN
    return jnp.where(j < N_TILES_A + n_fr, j + n_gab, j - (N_TILES_A + n_fr))


def _inproj(x2d, pos, mod6, w_a, w_b, w_gate, b_gate, seq_len, per_batch_mod):
    n_tok = x2d.shape[0]
    tm, tn = 1024, INPROJ_TN
    tiles_per_seq = max(seq_len // tm, 1)
    has_pos = pos is not None
    mod_idx = (lambda i, j: (i // tiles_per_seq, 0, 0)) if per_batch_mod else (lambda i, j: (0, 0, 0))
    in_specs = [pl.BlockSpec((tm, D_MODEL), lambda i, j: (i, 0))]
    args = [x2d]
    if has_pos:
        in_specs.append(pl.BlockSpec((tm, D_MODEL), lambda i, j: (i % tiles_per_seq, 0)))
        args.append(pos)
    in_specs += [
        pl.BlockSpec((1, 6, D_MODEL), mod_idx),
        pl.BlockSpec((D_MODEL, tn), lambda i, j: (0, jnp.minimum(j, N_TILES_A - 1))),
        pl.BlockSpec((D_MODEL, tn), lambda i, j: (0, jnp.maximum(j - N_TILES_A, 0))),
        _resident((D_MODEL, GATE_LANES), lambda i, j: (0, 0)),
        _resident((1, GATE_LANES), lambda i, j: (0, 0)),
    ]
    args += [mod6, w_a, w_b, w_gate, b_gate]
    return pl.pallas_call(
        functools.partial(_inproj_kernel, tm=tm, has_pos=has_pos),
        grid=(n_tok // tm, N_TILES_A + N_TILES_B),
        in_specs=in_specs,
        out_specs=[
            pl.BlockSpec((tm, tn), lambda i, j: (i, _inproj_out_tile(j))),
            pl.BlockSpec((tm, GATE_LANES), lambda i, j: (i, 0)),
        ],
        out_shape=[
            jax.ShapeDtypeStruct((n_tok, D_MAIN), BF16),
            jax.ShapeDtypeStruct((n_tok, GATE_LANES), F32),
        ],
        scratch_shapes=[pltpu.VMEM((tm, D_MODEL), BF16)],
        compiler_params=_params(("parallel", "arbitrary")),
        name="inproj",
    )(*args)


def _log_sigmoid(x):
    return jnp.minimum(x, 0.0) - jnp.log1p(jnp.exp(-jnp.abs(x)))


def _head_out(h, o, wn):
    return (jax.nn.sigmoid(o.astype(F32)) * (_norm(h) * wn)).astype(BF16)


EXT = HEAD_DIM + GATE_LANES
PASS1_GROUP = 4


def _lane_tile(x, n):
    return jnp.concatenate([x] * n, axis=1)


def _mlstm_kernel(*refs, seq_len, has_state, emit_state):
    refs = list(refs)
    q_ref, k_ref, v_ref, o_ref, g_ref, wn_ref = refs[:6]
    refs = refs[6:]
    if has_state:
        m0_ref, c0_ref, n0_ref = refs[:3]
        refs = refs[3:]
    hm_ref = refs[0]
    refs = refs[1:]
    if emit_state:
        cout_ref, nout_ref, mout_ref = refs[:3]
        refs = refs[3:]
    num_scr, row_scr, kv_scr, sc_scr, c_scr, h_scr = refs
    L = CHUNK
    nc = seq_len // L
    scale = HEAD_DIM ** -0.5
    head = pl.program_id(1)
    gate_shift = jnp.where(head == 0, 0, GATE_LANES - 2 * N_DIR * head)

    def chunk_rows(c):
        return pl.ds(pl.multiple_of(c * L, L), L)

    def load_q(rows):
        return (q_ref[rows, :].astype(F32) * scale).astype(BF16)

    t_idx = lax.broadcasted_iota(jnp.int32, (L, L), 0)
    s_idx = lax.broadcasted_iota(jnp.int32, (L, L), 1)
    hi_rows = lax.broadcasted_iota(jnp.int32, (16, L), 0) < 8

    masks = [s_idx <= t_idx, s_idx >= t_idx]
    masks_b = [jnp.where(mk, 1.0, 0.0).astype(BF16) for mk in masks]
    group = min(nc, PASS1_GROUP)

    def pass1(grp, carry):
        chunks = [grp * group + i for i in range(group)]
        base, items = [], []
        for c in chunks:
            rows = chunk_rows(c)
            qb, kb, vb = load_q(rows), k_ref[rows, :], v_ref[rows, :]
            v_ext = jnp.concatenate([vb, jnp.ones((L, GATE_LANES), BF16)], axis=1)
            s0 = _dot_nt(qb, kb)
            g_row = jnp.transpose(pltpu.roll(g_ref[rows, :], gate_shift, 1))
            h_scr[rows, :] = jnp.zeros((L, HEAD_DIM), F32)
            base.append((c, rows, kb, v_ext, s0))
            for d in range(N_DIR):
                ig_row = g_row[2 * d:2 * d + 1, :]
                lf = _log_sigmoid(g_row[2 * d + 1:2 * d + 2, :])
                lf_hi = lf.astype(BF16)
                lf_lo = (lf - lf_hi.astype(F32)).astype(BF16)
                lhs = jnp.where(hi_rows, lf_hi.astype(F32), lf_lo.astype(F32)).astype(BF16)
                r16 = _dot_nt(lhs, masks_b[d])
                lf_rep = jnp.concatenate([jnp.broadcast_to(lf_hi, (GATE_LANES, L)),
                                          jnp.broadcast_to(lf_lo, (GATE_LANES, L))], axis=0)
                bb = _dot_nt(masks_b[d], lf_rep)
                items.append((len(base) - 1, d, ig_row, lf, r16, bb))
        states = []
        for bi, d, ig_row, lf, r16, bb in items:
            c, rows, kb, v_ext, s0 = base[bi]
            b_row = r16[0:1, :] + r16[8:9, :]
            b_rep = bb[:, :GATE_LANES] + bb[:, GATE_LANES:]
            c_row = ig_row - b_row
            c_max = jnp.broadcast_to(
                jnp.max(jnp.where(masks[d], c_row, -jnp.inf), axis=1, keepdims=True), (L, L))
            w = jnp.exp(jnp.where(masks[d], c_row - c_max, -jnp.inf))
            pv = _dot((s0 * w).astype(BF16), v_ext)
            num_scr[d, rows, :] = pv[:, :HEAD_DIM]
            row_scr[d, 0, rows, :] = pv[:, HEAD_DIM:]
            row_scr[d, 1, rows, :] = b_rep + c_max[:, :GATE_LANES]
            row_scr[d, 2, rows, :] = b_rep
            states.append((bi, d, ig_row, lf, b_row))
        k_ts = [jnp.transpose(kb.astype(F32)) for (_, _, kb, _, _) in base]
        for bi, d, ig_row, lf, b_row in states:
            c, rows, kb, v_ext, s0 = base[bi]
            b_last = jnp.sum(lf, axis=1, keepdims=True)
            g = b_last - b_row + ig_row
            g_max = jnp.max(g, axis=1, keepdims=True)
            wk = jnp.exp(g - g_max)
            kv_scr[d, c] = _dot((k_ts[bi] * wk).astype(BF16), v_ext)
            sc_scr[d, c, 0] = jnp.broadcast_to(b_last, (8, GATE_LANES))
            sc_scr[d, c, 1] = jnp.broadcast_to(g_max, (8, GATE_LANES))
        return carry

    lax.fori_loop(0, nc // group, pass1, 0)

    ms = []
    for d in range(N_DIR):
        if has_state:
            bh = pl.program_id(0) * (N_DIR * N_HEADS) + d * N_HEADS + head
            n_rep = jnp.transpose(jnp.broadcast_to(n0_ref[0, d, 0], (GATE_LANES, HEAD_DIM)))
            c_scr[d] = jnp.concatenate([c0_ref[0, d, 0], n_rep], axis=1)
            ms.append(jnp.full((1, GATE_LANES), m0_ref[bh], F32))
        else:
            c_scr[d] = jnp.zeros((HEAD_DIM, EXT), F32)
            ms.append(jnp.zeros((1, GATE_LANES), F32))

    def pass2(j, carry):
        new = []
        for d, m in enumerate(carry):
            c = j if d == 0 else nc - 1 - j
            rows = chunk_rows(c)
            c_ext = c_scr[d]
            qcn = _dot(load_q(rows), c_ext.astype(BF16))
            den_i, a_rep, b_rep = row_scr[d, 0, rows, :], row_scr[d, 1, rows, :], row_scr[d, 2, rows, :]
            m_rep = jnp.maximum(b_rep + m, a_rep)
            r_intra = jnp.exp(a_rep - m_rep)
            r_state = jnp.exp(b_rep + m - m_rep)
            den = r_intra * den_i + r_state * qcn[:, HEAD_DIM:]
            inv = 1.0 / jnp.maximum(jnp.abs(den), jnp.exp(-m_rep))
            h = (_lane_tile(r_intra * inv, 2) * num_scr[d, rows, :]
                 + _lane_tile(r_state * inv, 2) * qcn[:, :HEAD_DIM])
            h_scr[rows, :] += h
            b_last, g_max = sc_scr[d, c, 0][0:1, :], sc_scr[d, c, 1][0:1, :]
            m_new = jnp.maximum(b_last + m, g_max)
            decay = jnp.exp(b_last + m - m_new)
            gain = jnp.exp(g_max - m_new)
            c_scr[d] = _lane_tile(decay, 3) * c_ext + _lane_tile(gain, 3) * kv_scr[d, c]
            new.append(m_new)
        return tuple(new)

    ms = lax.fori_loop(0, nc, pass2, tuple(ms))

    wn = wn_ref[0]

    def finish(c, carry):
        rows = chunk_rows(c)
        hm_ref[rows, :] = _head_out(h_scr[rows, :], o_ref[rows, :], wn)
        return carry

    lax.fori_loop(0, nc, finish, 0)
    if emit_state:
        for d in range(N_DIR):
            c_ext = c_scr[d]
            cout_ref[0, 0, d, 0] = c_ext[:, :HEAD_DIM]
            nout_ref[0, 0, d, 0] = jnp.transpose(c_ext[:, HEAD_DIM:])[0:1, :]
            mout_ref[0, d, 0] = ms[d]


def _mlstm(p_main, gates, w_hnorm, batch, seq_len, state=None, emit_state=False):
    n_tok = p_main.shape[0]
    T = seq_len
    blk = lambda col: pl.BlockSpec((T, HEAD_DIM), lambda b, h: (b, col // HEAD_DIM + h))
    in_specs = [blk(COL_Q), blk(COL_K), blk(COL_V), blk(COL_O),
                pl.BlockSpec((T, GATE_LANES), lambda b, h: (b, 0)),
                pl.BlockSpec((1, 1, HEAD_DIM), lambda b, h: (h, 0, 0))]
    args = [p_main, p_main, p_main, p_main, gates, w_hnorm.reshape(N_HEADS, 1, HEAD_DIM)]
    has_state = state is not None
    if has_state:
        C0, n0, m0 = state
        in_specs += [
            pl.BlockSpec(memory_space=pltpu.SMEM),
            pl.BlockSpec((1, N_DIR, 1, HEAD_DIM, HEAD_DIM), lambda b, h: (b, 0, h, 0, 0)),
            pl.BlockSpec((1, N_DIR, 1, 1, HEAD_DIM), lambda b, h: (b, 0, h, 0, 0)),
        ]
        args += [m0.reshape(-1), C0, n0.reshape(batch, N_DIR, N_HEADS, 1, HEAD_DIM)]
    out_specs = [pl.BlockSpec((T, HEAD_DIM), lambda b, h: (b, h))]
    out_shape = [jax.ShapeDtypeStruct((n_tok, D_MLSTM), BF16)]
    if emit_state:
        out_specs += [
            pl.BlockSpec((1, 1, N_DIR, 1, HEAD_DIM, HEAD_DIM), lambda b, h: (b, 0, 0, h, 0, 0)),
            pl.BlockSpec((1, 1, N_DIR, 1, 1, HEAD_DIM), lambda b, h: (b, 0, 0, h, 0, 0)),
            pl.BlockSpec((1, N_DIR, 1, 1, GATE_LANES), lambda b, h: (b, 0, h, 0, 0)),
        ]
        out_shape += [
            jax.ShapeDtypeStruct((batch, DEPTH, N_DIR, N_HEADS, HEAD_DIM, HEAD_DIM), F32),
            jax.ShapeDtypeStruct((batch, DEPTH, N_DIR, N_HEADS, 1, HEAD_DIM), F32),
            jax.ShapeDtypeStruct((batch, N_DIR, N_HEADS, 1, GATE_LANES), F32),
        ]
    nc = T // CHUNK
    scratch = [pltpu.VMEM((N_DIR, T, HEAD_DIM), F32),
               pltpu.VMEM((N_DIR, 3, T, GATE_LANES), F32),
               pltpu.VMEM((N_DIR, nc, HEAD_DIM, EXT), F32),
               pltpu.VMEM((N_DIR, nc, 2, 8, GATE_LANES), F32),
               pltpu.VMEM((N_DIR, HEAD_DIM, EXT), F32),
               pltpu.VMEM((T, HEAD_DIM), F32)]
    return pl.pallas_call(
        functools.partial(_mlstm_kernel, seq_len=T, has_state=has_state, emit_state=emit_state),
        grid=(batch, N_HEADS),
        in_specs=in_specs,
        out_specs=out_specs,
        out_shape=out_shape,
        scratch_shapes=scratch,
        compiler_params=_params(("parallel", "parallel")),
        name="mlstm",
    )(*args)


def _fourier_kernel(x_ref, cs_ref, ct_ref, st_ref, o_ref, *, scale):
    z = _dot(x_ref[...].astype(BF16), cs_ref[...])
    zc = z[:, :FGROUP_DIM].astype(BF16)
    zs = z[:, FGROUP_DIM:].astype(BF16)
    y = _dot(ct_ref[...], zc) - _dot(st_ref[...], zs)
    o_ref[...] = (y * scale).astype(BF16)


def _dft_tables(n):
    k = jnp.arange(n, dtype=jnp.int32)
    ang = ((k[:, None] * k[None, :]) % n).astype(F32) * (2.0 * jnp.pi / n)
    return jnp.cos(ang), jnp.sin(ang)


def _fourier(p_main, batch, seq_len):
    n_tok = p_main.shape[0]
    T = seq_len
    cc, sc = _dft_tables(FGROUP_DIM)
    cs = jnp.concatenate([cc, sc], axis=1).astype(BF16)
    ct, st = _dft_tables(T)
    return pl.pallas_call(
        functools.partial(_fourier_kernel, scale=float((T * FGROUP_DIM) ** -0.5)),
        grid=(batch, N_FGROUPS),
        in_specs=[
            pl.BlockSpec((T, FGROUP_DIM), lambda b, g: (b, COL_FR // FGROUP_DIM + g)),
            _resident((FGROUP_DIM, 2 * FGROUP_DIM), lambda b, g: (0, 0)),
            _resident((T, T), lambda b, g: (0, 0)),
            _resident((T, T), lambda b, g: (0, 0)),
        ],
        out_specs=pl.BlockSpec((T, FGROUP_DIM), lambda b, g: (b, g)),
        out_shape=jax.ShapeDtypeStruct((n_tok, D_FOURIER), BF16),
        compiler_params=_params(("parallel", "parallel")),
        name="fourier",
    )(p_main, cs, ct.astype(BF16), st.astype(BF16))


def _mix_kernel(*refs, has_pos):
    if has_pos:
        (hm_ref, fr_ref, ga_ref, gb_ref, x_ref, pos_ref, mod_ref, wm_ref, wf_ref, wo_ref,
         lg_ref, lb_ref, o_ref) = refs
    else:
        (hm_ref, fr_ref, ga_ref, gb_ref, x_ref, mod_ref, wm_ref, wf_ref, wo_ref,
         lg_ref, lb_ref, o_ref) = refs
    a = _dot(hm_ref[...], wm_ref[...])
    b = _dot(fr_ref[...], wf_ref[...])
    mixed = (jax.nn.sigmoid(ga_ref[...].astype(F32)) * a
             + jax.nn.sigmoid(gb_ref[...].astype(F32)) * b)
    z = _dot(mixed.astype(BF16), wo_ref[...])
    x = x_ref[...]
    if has_pos:
        x = x + pos_ref[...]
    g1 = mod_ref[0, 2:3, :]
    o_ref[...] = _norm(ALPHA * x + g1 * z) * lg_ref[...] + lb_ref[...]


def _mix(hm, fr, p_main, x2d, pos, mod6, w_br_m, w_br_f, w_out, ln_g, ln_b, seq_len, per_batch_mod):
    n_tok = x2d.shape[0]
    tm = 256
    tiles_per_seq = seq_len // tm
    has_pos = pos is not None
    mod_idx = (lambda i: (i // tiles_per_seq, 0, 0)) if per_batch_mod else (lambda i: (0, 0, 0))
    in_specs = [
        pl.BlockSpec((tm, D_MLSTM), lambda i: (i, 0)),
        pl.BlockSpec((tm, D_FOURIER), lambda i: (i, 0)),
        pl.BlockSpec((tm, D_MODEL), lambda i: (i, COL_GA // D_MODEL)),
        pl.BlockSpec((tm, D_MODEL), lambda i: (i, COL_GB // D_MODEL)),
        pl.BlockSpec((tm, D_MODEL), lambda i: (i, 0)),
    ]
    args = [hm, fr, p_main, p_main, x2d]
    if has_pos:
        in_specs.append(pl.BlockSpec((tm, D_MODEL), lambda i: (i % tiles_per_seq, 0)))
        args.append(pos)
    in_specs += [
        pl.BlockSpec((1, 6, D_MODEL), mod_idx),
        _resident((D_MLSTM, D_MODEL), lambda i: (0, 0)),
        _resident((D_FOURIER, D_MODEL), lambda i: (0, 0)),
        _resident((D_MODEL, D_MODEL), lambda i: (0, 0)),
        _resident((1, D_MODEL), lambda i: (0, 0)),
        _resident((1, D_MODEL), lambda i: (0, 0)),
    ]
    args += [mod6, w_br_m, w_br_f, w_out, ln_g.reshape(1, D_MODEL), ln_b.reshape(1, D_MODEL)]
    return pl.pallas_call(
        functools.partial(_mix_kernel, has_pos=has_pos),
        grid=(n_tok // tm,),
        in_specs=in_specs,
        out_specs=pl.BlockSpec((tm, D_MODEL), lambda i: (i, 0)),
        out_shape=jax.ShapeDtypeStruct((n_tok, D_MODEL), F32),
        compiler_params=_params(("parallel",)),
        name="mix",
    )(*args)


def _ffn_kernel(*refs, tm, n_sub, has_halo, tiles_per_seq):
    if has_halo:
        (x_ref, xp_ref, xn_ref, mod_ref, wv_ref, wg_ref, cwv_ref, cwg_ref, cbv_ref, cbg_ref,
         wd_ref, lg_ref, lb_ref, o_ref, h_scr, acc_scr) = refs
    else:
        (x_ref, mod_ref, wv_ref, wg_ref, cwv_ref, cwg_ref, cbv_ref, cbg_ref,
         wd_ref, lg_ref, lb_ref, o_ref, h_scr, acc_scr) = refs
    f = pl.program_id(1)
    rows = tm + 2 * HALO

    @pl.when(f == 0)
    def _():
        sh = mod_ref[0, 3:4, :]
        sc = mod_ref[0, 4:5, :]
        modulate = lambda x: _norm(x) * (1.0 + sc) + sh
        h_scr[HALO:HALO + tm, :] = modulate(x_ref[...]).astype(BF16)
        if has_halo:
            t = pl.program_id(0) % tiles_per_seq
            hp = jnp.where(t == 0, 0.0, modulate(xp_ref[...]))
            hn = jnp.where(t == tiles_per_seq - 1, 0.0, modulate(xn_ref[...]))
            h_scr[0:HALO, :] = hp.astype(BF16)
            h_scr[HALO + tm:rows, :] = hn.astype(BF16)
        else:
            h_scr[0:HALO, :] = jnp.zeros((HALO, D_MODEL), BF16)
            h_scr[HALO + tm:rows, :] = jnp.zeros((HALO, D_MODEL), BF16)
        acc_scr[...] = jnp.zeros_like(acc_scr)

    sub = tm // n_sub
    sub_rows = sub + 2 * HALO

    def conv(u, cw_ref, cb_ref):
        prev = pltpu.roll(u, 1, 0)
        nxt = pltpu.roll(u, sub_rows - 1, 0)
        y = prev * cw_ref[0:1, :] + u * cw_ref[1:2, :] + nxt * cw_ref[2:3, :] + cb_ref[...]
        return y[HALO:HALO + sub, :]

    ups = []
    for s in range(n_sub):
        h = h_scr[s * sub:s * sub + sub_rows, :]
        ups.append((_dot(h, wv_ref[...]), _dot(h, wg_ref[...])))
    for s, (uv, ug) in enumerate(ups):
        val = conv(uv, cwv_ref, cbv_ref)
        gate = conv(ug, cwg_ref, cbg_ref)
        act = (gate * jax.nn.sigmoid(gate) * val).astype(BF16)
        acc_scr[s * sub:(s + 1) * sub, :] += _dot(act, wd_ref[...])

    @pl.when(f == pl.num_programs(1) - 1)
    def _():
        g2 = mod_ref[0, 5:6, :]
        o_ref[...] = _norm(ALPHA * x_ref[...] + g2 * acc_scr[...]) * lg_ref[...] + lb_ref[...]


def _ffn(x1, mod6, w_up, w_conv, b_conv, w_down, ln_g, ln_b, seq_len, per_batch_mod):
    n_tok = x1.shape[0]
    tf = 512
    nf = D_FF // tf
    tm = min(seq_len, 512)
    tiles_per_seq = seq_len // tm
    has_halo = tiles_per_seq > 1
    mod_idx = (lambda i, f: (i // tiles_per_seq, 0, 0)) if per_batch_mod else (lambda i, f: (0, 0, 0))
    hb = tm // HALO
    n_hblk = n_tok // HALO
    in_specs = [pl.BlockSpec((tm, D_MODEL), lambda i, f: (i, 0))]
    args = [x1]
    if has_halo:
        in_specs += [
            pl.BlockSpec((HALO, D_MODEL), lambda i, f: (jnp.maximum(i * hb - 1, 0), 0)),
            pl.BlockSpec((HALO, D_MODEL), lambda i, f: (jnp.minimum((i + 1) * hb, n_hblk - 1), 0)),
        ]
        args += [x1, x1]
    in_specs += [
        pl.BlockSpec((1, 6, D_MODEL), mod_idx),
        pl.BlockSpec((D_MODEL, tf), lambda i, f: (0, f)),
        pl.BlockSpec((D_MODEL, tf), lambda i, f: (0, nf + f)),
        pl.BlockSpec((3, tf), lambda i, f: (0, f)),
        pl.BlockSpec((3, tf), lambda i, f: (0, nf + f)),
        pl.BlockSpec((1, tf), lambda i, f: (0, f)),
        pl.BlockSpec((1, tf), lambda i, f: (0, nf + f)),
        pl.BlockSpec((tf, D_MODEL), lambda i, f: (f, 0)),
        _resident((1, D_MODEL), lambda i, f: (0, 0)),
        _resident((1, D_MODEL), lambda i, f: (0, 0)),
    ]
    b_conv2 = b_conv.reshape(1, 2 * D_FF)
    args += [mod6, w_up, w_up, w_conv, w_conv, b_conv2, b_conv2, w_down,
             ln_g.reshape(1, D_MODEL), ln_b.reshape(1, D_MODEL)]
    return pl.pallas_call(
        functools.partial(_ffn_kernel, tm=tm, n_sub=FFN_SUBTILES, has_halo=has_halo,
                          tiles_per_seq=tiles_per_seq),
        grid=(n_tok // tm, nf),
        in_specs=in_specs,
        out_specs=pl.BlockSpec((tm, D_MODEL), lambda i, f: (i, 0)),
        out_shape=jax.ShapeDtypeStruct((n_tok, D_MODEL), F32),
        scratch_shapes=[pltpu.VMEM((tm + 2 * HALO, D_MODEL), BF16), pltpu.VMEM((tm, D_MODEL), F32)],
        compiler_params=_params(("parallel", "arbitrary")),
        name="ffn",
    )(*args)


def _grid_posemb(n_tokens):
    rows = n_tokens // GRID_W
    t = jnp.arange(rows * GRID_W)
    r = (t // GRID_W).astype(F32)[:, None]
    col = (t % GRID_W).astype(F32)[:, None]
    quarter = D_MODEL // 4
    freq = 1.0 / (10000.0 ** (jnp.arange(quarter, dtype=F32) / quarter))
    er, ec = r * freq, col * freq
    return jnp.concatenate([jnp.sin(er), jnp.cos(er), jnp.sin(ec), jnp.cos(ec)], -1)


def _split_w_in(w_in, b_gate):
    n_gate = 2 * N_DIR * N_HEADS
    w_a = w_in[:, :4 * D_MLSTM].astype(BF16)
    w_b = w_in[:, 4 * D_MLSTM + n_gate:].astype(BF16)
    gw = w_in[:, 4 * D_MLSTM:4 * D_MLSTM + n_gate].reshape(D_MODEL, N_DIR, 2, N_HEADS)
    gw = gw.transpose(0, 3, 1, 2).reshape(D_MODEL, n_gate)
    gw = jnp.pad(gw, ((0, 0), (0, GATE_LANES - n_gate))).astype(BF16)
    gb = b_gate.astype(F32).reshape(N_DIR, 2, N_HEADS).transpose(2, 0, 1).reshape(1, n_gate)
    gb = jnp.pad(gb, ((0, 0), (0, GATE_LANES - n_gate)))
    return w_a, w_b, gw, gb


def _layer(x2d, pos, mod6, weights, batch, seq_len, per_batch_mod, state, emit_state):
    (w_a, w_b, w_gate, b_gate, w_hnorm, w_br_m, w_br_f, w_out, ln1_g, ln1_b,
     w_up, w_conv, b_conv, w_down, ln2_g, ln2_b) = weights
    p_main, gates = _inproj(x2d, pos, mod6, w_a, w_b, w_gate, b_gate, seq_len, per_batch_mod)
    ml = _mlstm(p_main, gates, w_hnorm, batch, seq_len, state=state, emit_state=emit_state)
    fr = _fourier(p_main, batch, seq_len)
    x1 = _mix(ml[0], fr, p_main, x2d, pos, mod6, w_br_m, w_br_f, w_out, ln1_g, ln1_b,
              seq_len, per_batch_mod)
    x2 = _ffn(x1, mod6, w_up, w_conv, b_conv, w_down, ln2_g, ln2_b, seq_len, per_batch_mod)
    return x2, ml[1:]


def kernel(x_prompt, x_sample, c, state_C, state_n, state_m, c_ctx, w_ada, b_ada, w_in, b_gate,
           w_hnorm, w_br_m, w_br_f, w_out, ln1_g, ln1_b, w_up, w_conv, b_conv, w_down, ln2_g, ln2_b):
    assert w_ada.shape[0] == DEPTH
    B, S, _ = x_prompt.shape
    DB, DS, _ = x_sample.shape
    l = 0
    n_cond = 16
    cond = jnp.zeros((n_cond, D_MODEL), F32).at[0].set(c_ctx).at[1:1 + DB].set(c)
    mod6 = _modulation(cond, w_ada[l], b_ada[l]).reshape(n_cond, 6, D_MODEL)
    w_a, w_b, w_gate, b_gate_l = _split_w_in(w_in[l], b_gate[l])
    weights = (w_a, w_b, w_gate, b_gate_l, w_hnorm[l], w_br_m[l].astype(BF16), w_br_f[l].astype(BF16),
               w_out[l].astype(BF16), ln1_g[l], ln1_b[l], w_up[l].astype(BF16), w_conv[l], b_conv[l],
               w_down[l].astype(BF16), ln2_g[l], ln2_b[l])
    pos = _grid_posemb(DS)

    yp, states = _layer(x_prompt.reshape(B * S, D_MODEL), None, mod6[0:1], weights, B, S,
                        per_batch_mod=False, state=None, emit_state=True)
    ys, _ = _layer(x_sample.reshape(DB * DS, D_MODEL), pos, mod6[1:1 + DB], weights, DB, DS,
                   per_batch_mod=True, state=(state_C[:, l], state_n[:, l], state_m[:, l]),
                   emit_state=False)
    new_C, new_n, new_m = states
    new_n = new_n.reshape(B, DEPTH, N_DIR, N_HEADS, HEAD_DIM)
    new_m = new_m[:, :, :, 0, 0].reshape(B, DEPTH, N_DIR, N_HEADS)
    return (yp.reshape(B, S, D_MODEL), ys.reshape(DB, DS, D_MODEL), new_C, new_n, new_m)
```

```python
import functools

import jax
import jax.numpy as jnp
from jax import lax
from jax.experimental import pallas as pl
from jax.experimental.pallas import tpu as pltpu

D_MODEL = 2048
N_HEADS = 4
HEAD_DIM = 256
D_MLSTM = N_HEADS * HEAD_DIM
N_FGROUPS = 4
FGROUP_DIM = 256
D_FOURIER = N_FGROUPS * FGROUP_DIM
D_FF = 5632
GRID_W = 64
N_DIR = 2
DEPTH = 1
ALPHA = (2.0 * DEPTH) ** 0.25
LN_EPS = 1e-5

F32 = jnp.float32
BF16 = jnp.bfloat16

CHUNK = 256
COL_GA = 0
COL_GB = D_MODEL
COL_Q = 2 * D_MODEL
COL_K = COL_Q + D_MLSTM
COL_V = COL_K + D_MLSTM
COL_O = COL_V + D_MLSTM
COL_FR = COL_O + D_MLSTM
D_MAIN = COL_FR + D_FOURIER
GATE_LANES = 128
HALO = 8
FFN_SUBTILES = 1

VMEM_LIMIT = 56 * 1024 * 1024


def _params(sem, flags=None):
    return pltpu.CompilerParams(dimension_semantics=sem, vmem_limit_bytes=VMEM_LIMIT, flags=flags)


def _resident(shape, index_map):
    return pl.BlockSpec(shape, index_map, pipeline_mode=pl.Buffered(1))


def _norm(x):
    mu = jnp.mean(x, axis=-1, keepdims=True)
    xc = x - mu
    var = jnp.mean(xc * xc, axis=-1, keepdims=True)
    return xc * lax.rsqrt(var + LN_EPS)


def _dot(a, b):
    return jnp.dot(a, b, preferred_element_type=F32)


def _dot_nt(a, b):
    return lax.dot_general(a, b, (((1,), (1,)), ((), ())), preferred_element_type=F32)


def _mod_kernel(c_ref, w_ref, b_ref, o_ref):
    c = c_ref[...]
    s = c * jax.nn.sigmoid(c)
    o_ref[...] = _dot(s.astype(BF16), w_ref[...].astype(BF16)) + b_ref[...]


def _modulation(cond, w_ada, b_ada):
    rows, tn = cond.shape[0], 1024
    n_out = w_ada.shape[1]
    return pl.pallas_call(
        _mod_kernel,
        grid=(n_out // tn,),
        in_specs=[
            _resident((rows, D_MODEL), lambda j: (0, 0)),
            pl.BlockSpec((D_MODEL, tn), lambda j: (0, j)),
            pl.BlockSpec((1, tn), lambda j: (0, j)),
        ],
        out_specs=pl.BlockSpec((rows, tn), lambda j: (0, j)),
        out_shape=jax.ShapeDtypeStruct((rows, n_out), F32),
        compiler_params=_params(("arbitrary",)),
        name="modulation",
    )(cond, w_ada, b_ada.reshape(1, n_out))


INPROJ_TN = 1024
N_TILES_QKVOF = (4 * D_MLSTM + D_FOURIER) // INPROJ_TN
LN_ROWS = 256


def _inproj_kernel(*refs, tm, has_pos):
    if has_pos:
        x_ref, pos_ref, mod_ref, w_ref, wg_ref, bg_ref, p_ref, g_ref, h_scr = refs
    else:
        x_ref, mod_ref, w_ref, wg_ref, bg_ref, p_ref, g_ref, h_scr = refs

    @pl.when(pl.program_id(1) == 0)
    def _():
        sh = mod_ref[0, 0:1, :]
        sc = mod_ref[0, 1:2, :]

        def ln_rows(r, carry):
            rows = pl.ds(pl.multiple_of(r * LN_ROWS, LN_ROWS), LN_ROWS)
            x = x_ref[rows, :]
            if has_pos:
                x = x + pos_ref[rows, :]
            h = (_norm(x) * (1.0 + sc) + sh).astype(BF16)
            h_scr[rows, :] = h
            g_ref[rows, :] = _dot(h, wg_ref[...]) + bg_ref[...]
            return carry

        lax.fori_loop(0, tm // LN_ROWS, ln_rows, 0)

    p_ref[...] = _dot(h_scr[...], w_ref[...]).astype(BF16)


def _inproj_out_tile(j):
    n_gab = 2 * D_MODEL // INPROJ_TN
    return jnp.where(j < N_TILES_QKVOF, j + n_gab, j - N_TILES_QKVOF)


def _inproj(x2d, pos, mod6, w_main, w_gate, b_gate, seq_len, per_batch_mod):
    n_tok = x2d.shape[0]
    tm, tn = 1024, INPROJ_TN
    tiles_per_seq = max(seq_len // tm, 1)
    has_pos = pos is not None
    mod_idx = (lambda i, j: (i // tiles_per_seq, 0, 0)) if per_batch_mod else (lambda i, j: (0, 0, 0))
    in_specs = [pl.BlockSpec((tm, D_MODEL), lambda i, j: (i, 0))]
    args = [x2d]
    if has_pos:
        in_specs.append(pl.BlockSpec((tm, D_MODEL), lambda i, j: (i % tiles_per_seq, 0)))
        args.append(pos)
    in_specs += [
        pl.BlockSpec((1, 6, D_MODEL), mod_idx),
        pl.BlockSpec((D_MODEL, tn), lambda i, j: (0, j)),
        _resident((D_MODEL, GATE_LANES), lambda i, j: (0, 0)),
        _resident((1, GATE_LANES), lambda i, j: (0, 0)),
    ]
    args += [mod6, w_main, w_gate, b_gate]
    return pl.pallas_call(
        functools.partial(_inproj_kernel, tm=tm, has_pos=has_pos),
        grid=(n_tok // tm, D_MAIN // tn),
        in_specs=in_specs,
        out_specs=[
            pl.BlockSpec((tm, tn), lambda i, j: (i, _inproj_out_tile(j))),
            pl.BlockSpec((tm, GATE_LANES), lambda i, j: (i, 0)),
        ],
        out_shape=[
            jax.ShapeDtypeStruct((n_tok, D_MAIN), BF16),
            jax.ShapeDtypeStruct((n_tok, GATE_LANES), F32),
        ],
        scratch_shapes=[pltpu.VMEM((tm, D_MODEL), BF16)],
        compiler_params=_params(("parallel", "arbitrary")),
        name="inproj",
    )(*args)


def _log_sigmoid(x):
    return jnp.minimum(x, 0.0) - jnp.log1p(jnp.exp(-jnp.abs(x)))


def _head_out(h, o, wn):
    return (jax.nn.sigmoid(o.astype(F32)) * (_norm(h) * wn)).astype(BF16)


EXT = HEAD_DIM + GATE_LANES
PASS1_GROUP = 4


def _lane_tile(x, n):
    return jnp.concatenate([x] * n, axis=1)


def _mlstm_kernel(*refs, seq_len, has_state, emit_state):
    refs = list(refs)
    q_ref, k_ref, v_ref, o_ref, g_ref, wn_ref = refs[:6]
    refs = refs[6:]
    if has_state:
        m0_ref, c0_ref, n0_ref = refs[:3]
        refs = refs[3:]
    hm_ref = refs[0]
    refs = refs[1:]
    if emit_state:
        cout_ref, nout_ref, mout_ref = refs[:3]
        refs = refs[3:]
    num_scr, row_scr, kv_scr, sc_scr, c_scr, h_scr = refs
    L = CHUNK
    nc = seq_len // L
    scale = HEAD_DIM ** -0.5
    head = pl.program_id(1)
    gate_shift = jnp.where(head == 0, 0, GATE_LANES - 2 * N_DIR * head)

    def chunk_rows(c):
        return pl.ds(pl.multiple_of(c * L, L), L)

    def load_q(rows):
        return (q_ref[rows, :].astype(F32) * scale).astype(BF16)

    t_idx = lax.broadcasted_iota(jnp.int32, (L, L), 0)
    s_idx = lax.broadcasted_iota(jnp.int32, (L, L), 1)
    hi_rows = lax.broadcasted_iota(jnp.int32, (16, L), 0) < 8

    masks = [s_idx <= t_idx, s_idx >= t_idx]
    masks_b = [jnp.where(mk, 1.0, 0.0).astype(BF16) for mk in masks]
    group = min(nc, PASS1_GROUP)

    def pass1(grp, carry):
        chunks = [grp * group + i for i in range(group)]
        base, items = [], []
        for c in chunks:
            rows = chunk_rows(c)
            qb, kb, vb = load_q(rows), k_ref[rows, :], v_ref[rows, :]
            v_ext = jnp.concatenate([vb, jnp.ones((L, GATE_LANES), BF16)], axis=1)
            s0 = _dot_nt(qb, kb)
            g_row = jnp.transpose(pltpu.roll(g_ref[rows, :], gate_shift, 1))
            h_scr[rows, :] = jnp.zeros((L, HEAD_DIM), F32)
            base.append((c, rows, kb, v_ext, s0))
            for d in range(N_DIR):
                ig_row = g_row[2 * d:2 * d + 1, :]
                lf = _log_sigmoid(g_row[2 * d + 1:2 * d + 2, :])
                lf_hi = lf.astype(BF16)
                lf_lo = (lf - lf_hi.astype(F32)).astype(BF16)
                lhs = jnp.where(hi_rows, lf_hi.astype(F32), lf_lo.astype(F32)).astype(BF16)
                r16 = _dot_nt(lhs, masks_b[d])
                lf_rep = jnp.concatenate([jnp.broadcast_to(lf_hi, (GATE_LANES, L)),
                                          jnp.broadcast_to(lf_lo, (GATE_LANES, L))], axis=0)
                bb = _dot_nt(masks_b[d], lf_rep)
                items.append((len(base) - 1, d, ig_row, lf, r16, bb))
        states = []
        for bi, d, ig_row, lf, r16, bb in items:
            c, rows, kb, v_ext, s0 = base[bi]
            b_row = r16[0:1, :] + r16[8:9, :]
            b_rep = bb[:, :GATE_LANES] + bb[:, GATE_LANES:]
            c_row = ig_row - b_row
            c_max = jnp.broadcast_to(
                jnp.max(jnp.where(masks[d], c_row, -jnp.inf), axis=1, keepdims=True), (L, L))
            w = jnp.exp(jnp.where(masks[d], c_row - c_max, -jnp.inf))
            pv = _dot((s0 * w).astype(BF16), v_ext)
            num_scr[d, rows, :] = pv[:, :HEAD_DIM]
            row_scr[d, 0, rows, :] = pv[:, HEAD_DIM:]
            row_scr[d, 1, rows, :] = b_rep + c_max[:, :GATE_LANES]
            row_scr[d, 2, rows, :] = b_rep
            states.append((bi, d, ig_row, lf, b_row))
        k_ts = [jnp.transpose(kb.astype(F32)) for (_, _, kb, _, _) in base]
        for bi, d, ig_row, lf, b_row in states:
            c, rows, kb, v_ext, s0 = base[bi]
            b_last = jnp.sum(lf, axis=1, keepdims=True)
            g = b_last - b_row + ig_row
            g_max = jnp.max(g, axis=1, keepdims=True)
            wk = jnp.exp(g - g_max)
            kv_scr[d, c] = _dot((k_ts[bi] * wk).astype(BF16), v_ext)
            sc_scr[d, c, 0] = jnp.broadcast_to(b_last, (8, GATE_LANES))
            sc_scr[d, c, 1] = jnp.broadcast_to(g_max, (8, GATE_LANES))
        return carry

    lax.fori_loop(0, nc // group, pass1, 0)

    ms = []
    for d in range(N_DIR):
        if has_state:
            bh = pl.program_id(0) * (N_DIR * N_HEADS) + d * N_HEADS + head
            n_rep = jnp.transpose(jnp.broadcast_to(n0_ref[0, d, 0], (GATE_LANES, HEAD_DIM)))
            c_scr[d] = jnp.concatenate([c0_ref[0, d, 0], n_rep], axis=1)
            ms.append(jnp.full((1, GATE_LANES), m0_ref[bh], F32))
        else:
            c_scr[d] = jnp.zeros((HEAD_DIM, EXT), F32)
            ms.append(jnp.zeros((1, GATE_LANES), F32))

    def pass2(j, carry):
        new = []
        for d, m in enumerate(carry):
            c = j if d == 0 else nc - 1 - j
            rows = chunk_rows(c)
            c_ext = c_scr[d]
            qcn = _dot(load_q(rows), c_ext.astype(BF16))
            den_i, a_rep, b_rep = row_scr[d, 0, rows, :], row_scr[d, 1, rows, :], row_scr[d, 2, rows, :]
            m_rep = jnp.maximum(b_rep + m, a_rep)
            r_intra = jnp.exp(a_rep - m_rep)
            r_state = jnp.exp(b_rep + m - m_rep)
            den = r_intra * den_i + r_state * qcn[:, HEAD_DIM:]
            inv = 1.0 / jnp.maximum(jnp.abs(den), jnp.exp(-m_rep))
            h = (_lane_tile(r_intra * inv, 2) * num_scr[d, rows, :]
                 + _lane_tile(r_state * inv, 2) * qcn[:, :HEAD_DIM])
            h_scr[rows, :] += h
            b_last, g_max = sc_scr[d, c, 0][0:1, :], sc_scr[d, c, 1][0:1, :]
            m_new = jnp.maximum(b_last + m, g_max)
            decay = jnp.exp(b_last + m - m_new)
            gain = jnp.exp(g_max - m_new)
            c_scr[d] = _lane_tile(decay, 3) * c_ext + _lane_tile(gain, 3) * kv_scr[d, c]
            new.append(m_new)
        return tuple(new)

    ms = lax.fori_loop(0, nc, pass2, tuple(ms))

    wn = wn_ref[0]

    def finish(c, carry):
        rows = chunk_rows(c)
        hm_ref[rows, :] = _head_out(h_scr[rows, :], o_ref[rows, :], wn)
        return carry

    lax.fori_loop(0, nc, finish, 0)
    if emit_state:
        for d in range(N_DIR):
            c_ext = c_scr[d]
            cout_ref[0, 0, d, 0] = c_ext[:, :HEAD_DIM]
            nout_ref[0, 0, d, 0] = jnp.transpose(c_ext[:, HEAD_DIM:])[0:1, :]
            mout_ref[0, d, 0] = ms[d]


def _mlstm(p_main, gates, w_hnorm, batch, seq_len, state=None, emit_state=False):
    n_tok = p_main.shape[0]
    T = seq_len
    blk = lambda col: pl.BlockSpec((T, HEAD_DIM), lambda b, h: (b, col // HEAD_DIM + h))
    in_specs = [blk(COL_Q), blk(COL_K), blk(COL_V), blk(COL_O),
                pl.BlockSpec((T, GATE_LANES), lambda b, h: (b, 0)),
                pl.BlockSpec((1, 1, HEAD_DIM), lambda b, h: (h, 0, 0))]
    args = [p_main, p_main, p_main, p_main, gates, w_hnorm.reshape(N_HEADS, 1, HEAD_DIM)]
    has_state = state is not None
    if has_state:
        C0, n0, m0 = state
        in_specs += [
            pl.BlockSpec(memory_space=pltpu.SMEM),
            pl.BlockSpec((1, N_DIR, 1, HEAD_DIM, HEAD_DIM), lambda b, h: (b, 0, h, 0, 0)),
            pl.BlockSpec((1, N_DIR, 1, 1, HEAD_DIM), lambda b, h: (b, 0, h, 0, 0)),
        ]
        args += [m0.reshape(-1), C0, n0.reshape(batch, N_DIR, N_HEADS, 1, HEAD_DIM)]
    out_specs = [pl.BlockSpec((T, HEAD_DIM), lambda b, h: (b, h))]
    out_shape = [jax.ShapeDtypeStruct((n_tok, D_MLSTM), BF16)]
    if emit_state:
        out_specs += [
            pl.BlockSpec((1, 1, N_DIR, 1, HEAD_DIM, HEAD_DIM), lambda b, h: (b, 0, 0, h, 0, 0)),
            pl.BlockSpec((1, 1, N_DIR, 1, 1, HEAD_DIM), lambda b, h: (b, 0, 0, h, 0, 0)),
            pl.BlockSpec((1, N_DIR, 1, 1, GATE_LANES), lambda b, h: (b, 0, h, 0, 0)),
        ]
        out_shape += [
            jax.ShapeDtypeStruct((batch, DEPTH, N_DIR, N_HEADS, HEAD_DIM, HEAD_DIM), F32),
            jax.ShapeDtypeStruct((batch, DEPTH, N_DIR, N_HEADS, 1, HEAD_DIM), F32),
            jax.ShapeDtypeStruct((batch, N_DIR, N_HEADS, 1, GATE_LANES), F32),
        ]
    nc = T // CHUNK
    scratch = [pltpu.VMEM((N_DIR, T, HEAD_DIM), F32),
               pltpu.VMEM((N_DIR, 3, T, GATE_LANES), F32),
               pltpu.VMEM((N_DIR, nc, HEAD_DIM, EXT), F32),
               pltpu.VMEM((N_DIR, nc, 2, 8, GATE_LANES), F32),
               pltpu.VMEM((N_DIR, HEAD_DIM, EXT), F32),
               pltpu.VMEM((T, HEAD_DIM), F32)]
    return pl.pallas_call(
        functools.partial(_mlstm_kernel, seq_len=T, has_state=has_state, emit_state=emit_state),
        grid=(batch, N_HEADS),
        in_specs=in_specs,
        out_specs=out_specs,
        out_shape=out_shape,
        scratch_shapes=scratch,
        compiler_params=_params(("parallel", "parallel")),
        name="mlstm",
    )(*args)


def _fourier_kernel(x_ref, cs_ref, ct_ref, st_ref, o_ref, *, scale):
    z = _dot(x_ref[...], cs_ref[...])
    zc = z[:, :FGROUP_DIM].astype(BF16)
    zs = z[:, FGROUP_DIM:].astype(BF16)
    y = _dot(ct_ref[...], zc) - _dot(st_ref[...], zs)
    o_ref[...] = (y * scale).astype(BF16)


def _dft_tables(n):
    k = jnp.arange(n, dtype=jnp.int32)
    ang = ((k[:, None] * k[None, :]) % n).astype(F32) * (2.0 * jnp.pi / n)
    return jnp.cos(ang), jnp.sin(ang)


DFT_BLOCK = 256
DFT_RADIX = 8
HALF_LANES = 128


def _cadd(a, b):
    return a[0] + b[0], a[1] + b[1]


def _csub(a, b):
    return a[0] - b[0], a[1] - b[1]


def _mul_neg_i(a):
    return a[1], -a[0]


def _dft4(y):
    c0, c1 = _cadd(y[0], y[2]), _cadd(y[1], y[3])
    d0, d1 = _csub(y[0], y[2]), _mul_neg_i(_csub(y[1], y[3]))
    return [_cadd(c0, c1), _cadd(d0, d1), _csub(c0, c1), _csub(d0, d1)]


def _dft8(x):
    r = 0.5 ** 0.5
    a = [_cadd(x[n], x[n + 4]) for n in range(4)]
    b = [_csub(x[n], x[n + 4]) for n in range(4)]
    b[1] = ((b[1][0] + b[1][1]) * r, (b[1][1] - b[1][0]) * r)
    b[2] = _mul_neg_i(b[2])
    b[3] = ((b[3][1] - b[3][0]) * r, -(b[3][0] + b[3][1]) * r)
    even, odd = _dft4(a), _dft4(b)
    return [even[k // 2] if k % 2 == 0 else odd[k // 2] for k in range(8)]


def _fourier_long_kernel(x_ref, cs_ref, twc_ref, tws_ref, o_ref, w_scr, *, scale):
    nb = DFT_BLOCK
    cs = cs_ref[...]
    c_tab, s_tab = cs[:, :nb], cs[:, nb:]
    z = []
    for t1 in range(DFT_RADIX):
        zz = _dot(x_ref[t1 * nb:(t1 + 1) * nb, :], cs)
        z.append((zz[:, :nb], -zz[:, nb:]))
    a = _dft8(z)
    for u1 in range(DFT_RADIX):
        ar, ai = a[u1]
        if u1 > 0:
            twc = twc_ref[u1 * nb:(u1 + 1) * nb, :]
            tws = tws_ref[u1 * nb:(u1 + 1) * nb, :]
            ar, ai = ar * twc + ai * tws, ai * twc - ar * tws
        y = (_dot(c_tab, ar.astype(BF16)) + _dot(s_tab, ai.astype(BF16))) * scale
        for half in range(FGROUP_DIM // HALF_LANES):
            w_scr[half, pl.ds(u1, nb, stride=DFT_RADIX), :] = y[:, half * HALF_LANES:(half + 1) * HALF_LANES]
    for half in range(FGROUP_DIM // HALF_LANES):
        o_ref[:, half * HALF_LANES:(half + 1) * HALF_LANES] = w_scr[half].astype(BF16)


def _fourier_long(p_main, batch, seq_len):
    n_tok = p_main.shape[0]
    T = seq_len
    assert T == DFT_RADIX * DFT_BLOCK and FGROUP_DIM == DFT_BLOCK
    cc, sc = _dft_tables(FGROUP_DIM)
    cs = jnp.concatenate([cc, sc], axis=1).astype(BF16)
    u1 = jnp.repeat(jnp.arange(DFT_RADIX, dtype=jnp.int32), DFT_BLOCK)
    t2 = jnp.tile(jnp.arange(DFT_BLOCK, dtype=jnp.int32), DFT_RADIX)
    ang = ((u1 * t2) % T).astype(F32) * (2.0 * jnp.pi / T)
    twc = jnp.broadcast_to(jnp.cos(ang)[:, None], (T, FGROUP_DIM))
    tws = jnp.broadcast_to(jnp.sin(ang)[:, None], (T, FGROUP_DIM))
    return pl.pallas_call(
        functools.partial(_fourier_long_kernel, scale=float((T * FGROUP_DIM) ** -0.5)),
        grid=(batch, N_FGROUPS),
        in_specs=[
            pl.BlockSpec((T, FGROUP_DIM), lambda b, g: (b, COL_FR // FGROUP_DIM + g)),
            _resident((FGROUP_DIM, 2 * FGROUP_DIM), lambda b, g: (0, 0)),
            _resident((T, FGROUP_DIM), lambda b, g: (0, 0)),
            _resident((T, FGROUP_DIM), lambda b, g: (0, 0)),
        ],
        out_specs=pl.BlockSpec((T, FGROUP_DIM), lambda b, g: (b, g)),
        out_shape=jax.ShapeDtypeStruct((n_tok, D_FOURIER), BF16),
        scratch_shapes=[pltpu.VMEM((FGROUP_DIM // HALF_LANES, T, HALF_LANES), F32)],
        compiler_params=_params(("parallel", "parallel")),
        name="fourier_long",
    )(p_main, cs, twc, tws)


def _fourier(p_main, batch, seq_len):
    if seq_len == DFT_RADIX * DFT_BLOCK:
        return _fourier_long(p_main, batch, seq_len)
    n_tok = p_main.shape[0]
    T = seq_len
    cc, sc = _dft_tables(FGROUP_DIM)
    cs = jnp.concatenate([cc, sc], axis=1).astype(BF16)
    ct, st = _dft_tables(T)
    return pl.pallas_call(
        functools.partial(_fourier_kernel, scale=float((T * FGROUP_DIM) ** -0.5)),
        grid=(batch, N_FGROUPS),
        in_specs=[
            pl.BlockSpec((T, FGROUP_DIM), lambda b, g: (b, COL_FR // FGROUP_DIM + g)),
            _resident((FGROUP_DIM, 2 * FGROUP_DIM), lambda b, g: (0, 0)),
            _resident((T, T), lambda b, g: (0, 0)),
            _resident((T, T), lambda b, g: (0, 0)),
        ],
        out_specs=pl.BlockSpec((T, FGROUP_DIM), lambda b, g: (b, g)),
        out_shape=jax.ShapeDtypeStruct((n_tok, D_FOURIER), BF16),
        compiler_params=_params(("parallel", "parallel")),
        name="fourier",
    )(p_main, cs, ct.astype(BF16), st.astype(BF16))


def _mix_kernel(*refs, has_pos):
    if has_pos:
        (hm_ref, fr_ref, ga_ref, gb_ref, x_ref, pos_ref, mod_ref, wm_ref, wf_ref, wo_ref,
         lg_ref, lb_ref, o_ref) = refs
    else:
        (hm_ref, fr_ref, ga_ref, gb_ref, x_ref, mod_ref, wm_ref, wf_ref, wo_ref,
         lg_ref, lb_ref, o_ref) = refs
    a = _dot(hm_ref[...], wm_ref[...])
    b = _dot(fr_ref[...], wf_ref[...])
    mixed = (jax.nn.sigmoid(ga_ref[...].astype(F32)) * a
             + jax.nn.sigmoid(gb_ref[...].astype(F32)) * b)
    z = _dot(mixed.astype(BF16), wo_ref[...])
    x = x_ref[...]
    if has_pos:
        x = x + pos_ref[...]
    g1 = mod_ref[0, 2:3, :]
    o_ref[...] = _norm(ALPHA * x + g1 * z) * lg_ref[...] + lb_ref[...]


def _mix(hm, fr, p_main, x2d, pos, mod6, w_br_m, w_br_f, w_out, ln_g, ln_b, seq_len, per_batch_mod):
    n_tok = x2d.shape[0]
    tm = 256
    tiles_per_seq = seq_len // tm
    has_pos = pos is not None
    mod_idx = (lambda i: (i // tiles_per_seq, 0, 0)) if per_batch_mod else (lambda i: (0, 0, 0))
    in_specs = [
        pl.BlockSpec((tm, D_MLSTM), lambda i: (i, 0)),
        pl.BlockSpec((tm, D_FOURIER), lambda i: (i, 0)),
        pl.BlockSpec((tm, D_MODEL), lambda i: (i, COL_GA // D_MODEL)),
        pl.BlockSpec((tm, D_MODEL), lambda i: (i, COL_GB // D_MODEL)),
        pl.BlockSpec((tm, D_MODEL), lambda i: (i, 0)),
    ]
    args = [hm, fr, p_main, p_main, x2d]
    if has_pos:
        in_specs.append(pl.BlockSpec((tm, D_MODEL), lambda i: (i % tiles_per_seq, 0)))
        args.append(pos)
    in_specs += [
        pl.BlockSpec((1, 6, D_MODEL), mod_idx),
        _resident((D_MLSTM, D_MODEL), lambda i: (0, 0)),
        _resident((D_FOURIER, D_MODEL), lambda i: (0, 0)),
        _resident((D_MODEL, D_MODEL), lambda i: (0, 0)),
        _resident((1, D_MODEL), lambda i: (0, 0)),
        _resident((1, D_MODEL), lambda i: (0, 0)),
    ]
    args += [mod6, w_br_m, w_br_f, w_out, ln_g.reshape(1, D_MODEL), ln_b.reshape(1, D_MODEL)]
    return pl.pallas_call(
        functools.partial(_mix_kernel, has_pos=has_pos),
        grid=(n_tok // tm,),
        in_specs=in_specs,
        out_specs=pl.BlockSpec((tm, D_MODEL), lambda i: (i, 0)),
        out_shape=jax.ShapeDtypeStruct((n_tok, D_MODEL), F32),
        compiler_params=_params(("parallel",)),
        name="mix",
    )(*args)


def _ffn_kernel(*refs, tm, n_sub, has_halo, tiles_per_seq):
    if has_halo:
        (x_ref, xp_ref, xn_ref, mod_ref, wv_ref, wg_ref, cwv_ref, cwg_ref, cbv_ref, cbg_ref,
         wd_ref, lg_ref, lb_ref, o_ref, h_scr, acc_scr) = refs
    else:
        (x_ref, mod_ref, wv_ref, wg_ref, cwv_ref, cwg_ref, cbv_ref, cbg_ref,
         wd_ref, lg_ref, lb_ref, o_ref, h_scr, acc_scr) = refs
    f = pl.program_id(1)
    rows = tm + 2 * HALO

    @pl.when(f == 0)
    def _():
        sh = mod_ref[0, 3:4, :]
        sc = mod_ref[0, 4:5, :]
        modulate = lambda x: _norm(x) * (1.0 + sc) + sh
        h_scr[HALO:HALO + tm, :] = modulate(x_ref[...]).astype(BF16)
        if has_halo:
            t = pl.program_id(0) % tiles_per_seq
            hp = jnp.where(t == 0, 0.0, modulate(xp_ref[...]))
            hn = jnp.where(t == tiles_per_seq - 1, 0.0, modulate(xn_ref[...]))
            h_scr[0:HALO, :] = hp.astype(BF16)
            h_scr[HALO + tm:rows, :] = hn.astype(BF16)
        else:
            h_scr[0:HALO, :] = jnp.zeros((HALO, D_MODEL), BF16)
            h_scr[HALO + tm:rows, :] = jnp.zeros((HALO, D_MODEL), BF16)
        acc_scr[...] = jnp.zeros_like(acc_scr)

    sub = tm // n_sub
    sub_rows = sub + 2 * HALO

    def conv(u, cw_ref, cb_ref):
        prev = pltpu.roll(u, 1, 0)
        nxt = pltpu.roll(u, sub_rows - 1, 0)
        y = prev * cw_ref[0:1, :] + u * cw_ref[1:2, :] + nxt * cw_ref[2:3, :] + cb_ref[...]
        return y[HALO:HALO + sub, :]

    ups = []
    for s in range(n_sub):
        h = h_scr[s * sub:s * sub + sub_rows, :]
        ups.append((_dot(h, wv_ref[...]), _dot(h, wg_ref[...])))
    for s, (uv, ug) in enumerate(ups):
        val = conv(uv, cwv_ref, cbv_ref)
        gate = conv(ug, cwg_ref, cbg_ref)
        act = (gate * jax.nn.sigmoid(gate) * val).astype(BF16)
        acc_scr[s * sub:(s + 1) * sub, :] += _dot(act, wd_ref[...])

    @pl.when(f == pl.num_programs(1) - 1)
    def _():
        g2 = mod_ref[0, 5:6, :]
        o_ref[...] = _norm(ALPHA * x_ref[...] + g2 * acc_scr[...]) * lg_ref[...] + lb_ref[...]


def _ffn(x1, mod6, w_up, w_conv, b_conv, w_down, ln_g, ln_b, seq_len, per_batch_mod):
    n_tok = x1.shape[0]
    tf = 512
    nf = D_FF // tf
    tm = min(seq_len, 512)
    tiles_per_seq = seq_len // tm
    has_halo = tiles_per_seq > 1
    mod_idx = (lambda i, f: (i // tiles_per_seq, 0, 0)) if per_batch_mod else (lambda i, f: (0, 0, 0))
    hb = tm // HALO
    n_hblk = n_tok // HALO
    in_specs = [pl.BlockSpec((tm, D_MODEL), lambda i, f: (i, 0))]
    args = [x1]
    if has_halo:
        in_specs += [
            pl.BlockSpec((HALO, D_MODEL), lambda i, f: (jnp.maximum(i * hb - 1, 0), 0)),
            pl.BlockSpec((HALO, D_MODEL), lambda i, f: (jnp.minimum((i + 1) * hb, n_hblk - 1), 0)),
        ]
        args += [x1, x1]
    in_specs += [
        pl.BlockSpec((1, 6, D_MODEL), mod_idx),
        pl.BlockSpec((D_MODEL, tf), lambda i, f: (0, f)),
        pl.BlockSpec((D_MODEL, tf), lambda i, f: (0, nf + f)),
        pl.BlockSpec((3, tf), lambda i, f: (0, f)),
        pl.BlockSpec((3, tf), lambda i, f: (0, nf + f)),
        pl.BlockSpec((1, tf), lambda i, f: (0, f)),
        pl.BlockSpec((1, tf), lambda i, f: (0, nf + f)),
        pl.BlockSpec((tf, D_MODEL), lambda i, f: (f, 0)),
        _resident((1, D_MODEL), lambda i, f: (0, 0)),
        _resident((1, D_MODEL), lambda i, f: (0, 0)),
    ]
    b_conv2 = b_conv.reshape(1, 2 * D_FF)
    args += [mod6, w_up, w_up, w_conv, w_conv, b_conv2, b_conv2, w_down,
             ln_g.reshape(1, D_MODEL), ln_b.reshape(1, D_MODEL)]
    return pl.pallas_call(
        functools.partial(_ffn_kernel, tm=tm, n_sub=FFN_SUBTILES, has_halo=has_halo,
                          tiles_per_seq=tiles_per_seq),
        grid=(n_tok // tm, nf),
        in_specs=in_specs,
        out_specs=pl.BlockSpec((tm, D_MODEL), lambda i, f: (i, 0)),
        out_shape=jax.ShapeDtypeStruct((n_tok, D_MODEL), F32),
        scratch_shapes=[pltpu.VMEM((tm + 2 * HALO, D_MODEL), BF16), pltpu.VMEM((tm, D_MODEL), F32)],
        compiler_params=_params(("parallel", "arbitrary")),
        name="ffn",
    )(*args)


def _grid_posemb(n_tokens):
    rows = n_tokens // GRID_W
    t = jnp.arange(rows * GRID_W)
    r = (t // GRID_W).astype(F32)[:, None]
    col = (t % GRID_W).astype(F32)[:, None]
    quarter = D_MODEL // 4
    freq = 1.0 / (10000.0 ** (jnp.arange(quarter, dtype=F32) / quarter))
    er, ec = r * freq, col * freq
    return jnp.concatenate([jnp.sin(er), jnp.cos(er), jnp.sin(ec), jnp.cos(ec)], -1)


def _split_w_in(w_in, b_gate):
    n_gate = 2 * N_DIR * N_HEADS
    w_main = jnp.concatenate([w_in[:, :4 * D_MLSTM], w_in[:, 4 * D_MLSTM + n_gate:]], axis=1).astype(BF16)
    gw = w_in[:, 4 * D_MLSTM:4 * D_MLSTM + n_gate].reshape(D_MODEL, N_DIR, 2, N_HEADS)
    gw = gw.transpose(0, 3, 1, 2).reshape(D_MODEL, n_gate)
    gw = jnp.pad(gw, ((0, 0), (0, GATE_LANES - n_gate))).astype(BF16)
    gb = b_gate.astype(F32).reshape(N_DIR, 2, N_HEADS).transpose(2, 0, 1).reshape(1, n_gate)
    gb = jnp.pad(gb, ((0, 0), (0, GATE_LANES - n_gate)))
    return w_main, gw, gb


def _layer(x2d, pos, mod6, weights, batch, seq_len, per_batch_mod, state, emit_state):
    (w_main, w_gate, b_gate, w_hnorm, w_br_m, w_br_f, w_out, ln1_g, ln1_b,
     w_up, w_conv, b_conv, w_down, ln2_g, ln2_b) = weights
    p_main, gates = _inproj(x2d, pos, mod6, w_main, w_gate, b_gate, seq_len, per_batch_mod)
    ml = _mlstm(p_main, gates, w_hnorm, batch, seq_len, state=state, emit_state=emit_state)
    fr = _fourier(p_main, batch, seq_len)
    x1 = _mix(ml[0], fr, p_main, x2d, pos, mod6, w_br_m, w_br_f, w_out, ln1_g, ln1_b,
              seq_len, per_batch_mod)
    x2 = _ffn(x1, mod6, w_up, w_conv, b_conv, w_down, ln2_g, ln2_b, seq_len, per_batch_mod)
    return x2, ml[1:]


def kernel(x_prompt, x_sample, c, state_C, state_n, state_m, c_ctx, w_ada, b_ada, w_in, b_gate,
           w_hnorm, w_br_m, w_br_f, w_out, ln1_g, ln1_b, w_up, w_conv, b_conv, w_down, ln2_g, ln2_b):
    assert w_ada.shape[0] == DEPTH
    B, S, _ = x_prompt.shape
    DB, DS, _ = x_sample.shape
    l = 0
    n_cond = 16
    cond = jnp.zeros((n_cond, D_MODEL), F32).at[0].set(c_ctx).at[1:1 + DB].set(c)
    mod6 = _modulation(cond, w_ada[l], b_ada[l]).reshape(n_cond, 6, D_MODEL)
    w_main, w_gate, b_gate_l = _split_w_in(w_in[l], b_gate[l])
    weights = (w_main, w_gate, b_gate_l, w_hnorm[l], w_br_m[l].astype(BF16), w_br_f[l].astype(BF16),
               w_out[l].astype(BF16), ln1_g[l], ln1_b[l], w_up[l].astype(BF16), w_conv[l], b_conv[l],
               w_down[l].astype(BF16), ln2_g[l], ln2_b[l])
    pos = _grid_posemb(DS)

    yp, states = _layer(x_prompt.reshape(B * S, D_MODEL), None, mod6[0:1], weights, B, S,
                        per_batch_mod=False, state=None, emit_state=True)
    ys, _ = _layer(x_sample.reshape(DB * DS, D_MODEL), pos, mod6[1:1 + DB], weights, DB, DS,
                   per_batch_mod=True, state=(state_C[:, l], state_n[:, l], state_m[:, l]),
                   emit_state=False)
    new_C, new_n, new_m = states
    new_n = new_n.reshape(B, DEPTH, N_DIR, N_HEADS, HEAD_DIM)
    new_m = new_m[:, :, :, 0, 0].reshape(B, DEPTH, N_DIR, N_HEADS)
    return (yp.reshape(B, S, D_MODEL), ys.reshape(DB, DS, D_MODEL), new_C, new_n, new_m)
```

```python
import functools

import jax
import jax.numpy as jnp
from jax import lax
from jax.experimental import pallas as pl
from jax.experimental.pallas import tpu as pltpu

D_MODEL = 2048
N_HEADS = 4
HEAD_DIM = 256
D_MLSTM = N_HEADS * HEAD_DIM
N_FGROUPS = 4
FGROUP_DIM = 256
D_FOURIER = N_FGROUPS * FGROUP_DIM
D_FF = 5632
GRID_W = 64
N_DIR = 2
DEPTH = 1
ALPHA = (2.0 * DEPTH) ** 0.25
LN_EPS = 1e-5

F32 = jnp.float32
BF16 = jnp.bfloat16

CHUNK = 256
COL_Q = 0
COL_K = COL_Q + D_MLSTM
COL_V = COL_K + D_MLSTM
COL_O = COL_V + D_MLSTM
COL_FR = COL_O + D_MLSTM
COL_GA = COL_FR + D_FOURIER
COL_GB = COL_GA + D_MODEL
D_MAIN = COL_GB + D_MODEL
GATE_BLOCK = 1024
GATE_LANES = 128
HALO = 8

VMEM_LIMIT = 56 * 1024 * 1024


def _params(sem, flags=None):
    return pltpu.CompilerParams(dimension_semantics=sem, vmem_limit_bytes=VMEM_LIMIT, flags=flags)


def _resident(shape, index_map):
    return pl.BlockSpec(shape, index_map, pipeline_mode=pl.Buffered(1))


def _norm(x):
    mu = jnp.mean(x, axis=-1, keepdims=True)
    xc = x - mu
    var = jnp.mean(xc * xc, axis=-1, keepdims=True)
    return xc * lax.rsqrt(var + LN_EPS)


def _dot(a, b):
    return jnp.dot(a, b, preferred_element_type=F32)


def _dot_nt(a, b):
    return lax.dot_general(a, b, (((1,), (1,)), ((), ())), preferred_element_type=F32)


def _mod_kernel(c_ref, w_ref, b_ref, o_ref):
    c = c_ref[...]
    s = c * jax.nn.sigmoid(c)
    o_ref[...] = _dot(s.astype(BF16), w_ref[...].astype(BF16)) + b_ref[...]


def _modulation(cond, w_ada, b_ada):
    rows, tn = cond.shape[0], 1024
    n_out = w_ada.shape[1]
    return pl.pallas_call(
        _mod_kernel,
        grid=(n_out // tn,),
        in_specs=[
            _resident((rows, D_MODEL), lambda j: (0, 0)),
            pl.BlockSpec((D_MODEL, tn), lambda j: (0, j)),
            pl.BlockSpec((1, tn), lambda j: (0, j)),
        ],
        out_specs=pl.BlockSpec((rows, tn), lambda j: (0, j)),
        out_shape=jax.ShapeDtypeStruct((rows, n_out), F32),
        compiler_params=_params(("arbitrary",)),
        name="modulation",
    )(cond, w_ada, b_ada.reshape(1, n_out))


INPROJ_TN = 2304
LN_ROWS = 256


def _add_grid_pos(x, pr_ref, pc_ref, tok0):
    half = D_MODEL // 2
    out = []
    for k in range(x.shape[0] // GRID_W):
        blk = x[k * GRID_W:(k + 1) * GRID_W, :]
        pr = pr_ref[pl.ds(tok0 // GRID_W + k, 1), :]
        out.append(jnp.concatenate([blk[:, :half] + pr, blk[:, half:] + pc_ref[...]], axis=1))
    return jnp.concatenate(out, axis=0)


def _inproj_kernel(*refs, tm, tiles_per_seq, has_pos):
    if has_pos:
        x_ref, pr_ref, pc_ref, mod_ref, w_ref, wg_ref, bg_ref, p_ref, g_ref, h_scr = refs
    else:
        x_ref, mod_ref, w_ref, wg_ref, bg_ref, p_ref, g_ref, h_scr = refs

    @pl.when(pl.program_id(1) == 0)
    def _():
        sh = mod_ref[0, 0:1, :]
        sc = mod_ref[0, 1:2, :]

        def ln_rows(r, carry):
            rows = pl.ds(pl.multiple_of(r * LN_ROWS, LN_ROWS), LN_ROWS)
            x = x_ref[rows, :]
            if has_pos:
                x = _add_grid_pos(x, pr_ref, pc_ref,
                                  (pl.program_id(0) % tiles_per_seq) * tm + r * LN_ROWS)
            h = (_norm(x) * (1.0 + sc) + sh).astype(BF16)
            h_scr[rows, :] = h
            g_ref[rows, :] = _dot(h, wg_ref[...]) + bg_ref[...]
            return carry

        lax.fori_loop(0, tm // LN_ROWS, ln_rows, 0)

    p_ref[...] = _dot(h_scr[...], w_ref[...]).astype(BF16)


def _inproj(x2d, pos, mod6, w_main, w_gate, b_gate, seq_len, per_batch_mod):
    n_tok = x2d.shape[0]
    tm, tn = 1024, INPROJ_TN
    tiles_per_seq = max(seq_len // tm, 1)
    has_pos = pos is not None
    mod_idx = (lambda i, j: (i // tiles_per_seq, 0, 0)) if per_batch_mod else (lambda i, j: (0, 0, 0))
    in_specs = [pl.BlockSpec((tm, D_MODEL), lambda i, j: (i, 0))]
    args = [x2d]
    if has_pos:
        in_specs += [_resident(p.shape, lambda i, j: (0, 0)) for p in pos]
        args += list(pos)
    in_specs += [
        pl.BlockSpec((1, 6, D_MODEL), mod_idx),
        pl.BlockSpec((D_MODEL, tn), lambda i, j: (0, j)),
        _resident((D_MODEL, GATE_LANES), lambda i, j: (0, 0)),
        _resident((1, GATE_LANES), lambda i, j: (0, 0)),
    ]
    args += [mod6, w_main, w_gate, b_gate]
    return pl.pallas_call(
        functools.partial(_inproj_kernel, tm=tm, tiles_per_seq=tiles_per_seq, has_pos=has_pos),
        grid=(n_tok // tm, D_MAIN // tn),
        in_specs=in_specs,
        out_specs=[
            pl.BlockSpec((tm, tn), lambda i, j: (i, j)),
            pl.BlockSpec((tm, GATE_LANES), lambda i, j: (i, 0)),
        ],
        out_shape=[
            jax.ShapeDtypeStruct((n_tok, D_MAIN), BF16),
            jax.ShapeDtypeStruct((n_tok, GATE_LANES), F32),
        ],
        scratch_shapes=[pltpu.VMEM((tm, D_MODEL), BF16)],
        compiler_params=_params(("parallel", "arbitrary")),
        name="inproj",
    )(*args)


def _log_sigmoid(x):
    return jnp.minimum(x, 0.0) - jnp.log1p(jnp.exp(-jnp.abs(x)))


def _head_out(h, o, wn):
    return (jax.nn.sigmoid(o.astype(F32)) * (_norm(h) * wn)).astype(BF16)


EXT = HEAD_DIM + GATE_LANES
PASS1_GROUP = 4


def _lane_tile(x, n):
    return jnp.concatenate([x] * n, axis=1)


def _mlstm_kernel(*refs, seq_len, has_state, emit_state):
    refs = list(refs)
    q_ref, k_ref, v_ref, o_ref, g_ref, wn_ref = refs[:6]
    refs = refs[6:]
    if has_state:
        m0_ref, c0_ref, n0_ref = refs[:3]
        refs = refs[3:]
    hm_ref = refs[0]
    refs = refs[1:]
    if emit_state:
        cout_ref, nout_ref, mout_ref = refs[:3]
        refs = refs[3:]
    num_scr, row_scr, kv_scr, sc_scr, c_scr, h_scr = refs
    L = CHUNK
    nc = seq_len // L
    scale = HEAD_DIM ** -0.5
    head = pl.program_id(1)
    gate_shift = jnp.where(head == 0, 0, GATE_LANES - 2 * N_DIR * head)

    def chunk_rows(c):
        return pl.ds(pl.multiple_of(c * L, L), L)

    def load_q(rows):
        return (q_ref[rows, :].astype(F32) * scale).astype(BF16)

    t_idx = lax.broadcasted_iota(jnp.int32, (L, L), 0)
    s_idx = lax.broadcasted_iota(jnp.int32, (L, L), 1)
    hi_rows = lax.broadcasted_iota(jnp.int32, (16, L), 0) < 8

    masks = [s_idx <= t_idx, s_idx >= t_idx]
    masks_b = [jnp.where(mk, 1.0, 0.0).astype(BF16) for mk in masks]
    group = min(nc, PASS1_GROUP)

    def pass1(grp, carry):
        chunks = [grp * group + i for i in range(group)]
        base, items = [], []
        for c in chunks:
            rows = chunk_rows(c)
            qb, kb, vb = load_q(rows), k_ref[rows, :], v_ref[rows, :]
            v_ext = jnp.concatenate([vb, jnp.ones((L, GATE_LANES), BF16)], axis=1)
            s0 = _dot_nt(qb, kb)
            g_row = jnp.transpose(pltpu.roll(g_ref[rows, :], gate_shift, 1))
            h_scr[rows, :] = jnp.zeros((L, HEAD_DIM), F32)
            base.append((c, rows, kb, v_ext, s0))
            for d in range(N_DIR):
                ig_row = g_row[2 * d:2 * d + 1, :]
                lf = _log_sigmoid(g_row[2 * d + 1:2 * d + 2, :])
                lf_hi = lf.astype(BF16)
                lf_lo = (lf - lf_hi.astype(F32)).astype(BF16)
                lhs = jnp.where(hi_rows, lf_hi.astype(F32), lf_lo.astype(F32)).astype(BF16)
                r16 = _dot_nt(lhs, masks_b[d])
                lf_rep = jnp.concatenate([jnp.broadcast_to(lf_hi, (GATE_LANES, L)),
                                          jnp.broadcast_to(lf_lo, (GATE_LANES, L))], axis=0)
                bb = _dot_nt(masks_b[d], lf_rep)
                items.append((len(base) - 1, d, ig_row, lf, r16, bb))
        states = []
        for bi, d, ig_row, lf, r16, bb in items:
            c, rows, kb, v_ext, s0 = base[bi]
            b_row = r16[0:1, :] + r16[8:9, :]
            b_rep = bb[:, :GATE_LANES] + bb[:, GATE_LANES:]
            c_row = ig_row - b_row
            c_max = jnp.broadcast_to(
                jnp.max(jnp.where(masks[d], c_row, -jnp.inf), axis=1, keepdims=True), (L, L))
            w = jnp.exp(jnp.where(masks[d], c_row - c_max, -jnp.inf))
            pv = _dot((s0 * w).astype(BF16), v_ext)
            num_scr[d, rows, :] = pv[:, :HEAD_DIM]
            row_scr[d, 0, rows, :] = pv[:, HEAD_DIM:]
            row_scr[d, 1, rows, :] = b_rep + c_max[:, :GATE_LANES]
            row_scr[d, 2, rows, :] = b_rep
            states.append((bi, d, ig_row, lf, b_row))
        k_ts = [jnp.transpose(kb.astype(F32)) for (_, _, kb, _, _) in base]
        for bi, d, ig_row, lf, b_row in states:
            c, rows, kb, v_ext, s0 = base[bi]
            b_last = jnp.sum(lf, axis=1, keepdims=True)
            g = b_last - b_row + ig_row
            g_max = jnp.max(g, axis=1, keepdims=True)
            wk = jnp.exp(g - g_max)
            kv_scr[d, c] = _dot((k_ts[bi] * wk).astype(BF16), v_ext)
            sc_scr[d, c, 0] = jnp.broadcast_to(b_last, (8, GATE_LANES))
            sc_scr[d, c, 1] = jnp.broadcast_to(g_max, (8, GATE_LANES))
        return carry

    lax.fori_loop(0, nc // group, pass1, 0)

    ms = []
    for d in range(N_DIR):
        if has_state:
            bh = pl.program_id(0) * (N_DIR * N_HEADS) + d * N_HEADS + head
            n_rep = jnp.transpose(jnp.broadcast_to(n0_ref[0, d, 0], (GATE_LANES, HEAD_DIM)))
            c_scr[d] = jnp.concatenate([c0_ref[0, d, 0], n_rep], axis=1)
            ms.append(jnp.full((1, GATE_LANES), m0_ref[bh], F32))
        else:
            c_scr[d] = jnp.zeros((HEAD_DIM, EXT), F32)
            ms.append(jnp.zeros((1, GATE_LANES), F32))

    def pass2(j, carry):
        new = []
        for d, m in enumerate(carry):
            c = j if d == 0 else nc - 1 - j
            rows = chunk_rows(c)
            c_ext = c_scr[d]
            qcn = _dot(load_q(rows), c_ext.astype(BF16))
            den_i, a_rep, b_rep = row_scr[d, 0, rows, :], row_scr[d, 1, rows, :], row_scr[d, 2, rows, :]
            m_rep = jnp.maximum(b_rep + m, a_rep)
            r_intra = jnp.exp(a_rep - m_rep)
            r_state = jnp.exp(b_rep + m - m_rep)
            den = r_intra * den_i + r_state * qcn[:, HEAD_DIM:]
            inv = 1.0 / jnp.maximum(jnp.abs(den), jnp.exp(-m_rep))
            h = (_lane_tile(r_intra * inv, 2) * num_scr[d, rows, :]
                 + _lane_tile(r_state * inv, 2) * qcn[:, :HEAD_DIM])
            h_scr[rows, :] += h
            b_last, g_max = sc_scr[d, c, 0][0:1, :], sc_scr[d, c, 1][0:1, :]
            m_new = jnp.maximum(b_last + m, g_max)
            decay = jnp.exp(b_last + m - m_new)
            gain = jnp.exp(g_max - m_new)
            c_scr[d] = _lane_tile(decay, 3) * c_ext + _lane_tile(gain, 3) * kv_scr[d, c]
            new.append(m_new)
        return tuple(new)

    ms = lax.fori_loop(0, nc, pass2, tuple(ms))

    wn = wn_ref[0]

    def finish(c, carry):
        rows = chunk_rows(c)
        hm_ref[rows, :] = _head_out(h_scr[rows, :], o_ref[rows, :], wn)
        return carry

    lax.fori_loop(0, nc, finish, 0)
    if emit_state:
        for d in range(N_DIR):
            c_ext = c_scr[d]
            cout_ref[0, 0, d, 0] = c_ext[:, :HEAD_DIM]
            nout_ref[0, 0, d, 0] = jnp.transpose(c_ext[:, HEAD_DIM:])[0:1, :]
            mout_ref[0, d, 0] = ms[d]


def _mlstm(p_main, gates, w_hnorm, batch, seq_len, state=None, emit_state=False):
    n_tok = p_main.shape[0]
    T = seq_len
    blk = lambda col: pl.BlockSpec((T, HEAD_DIM), lambda b, h: (b, col // HEAD_DIM + h))
    in_specs = [blk(COL_Q), blk(COL_K), blk(COL_V), blk(COL_O),
                pl.BlockSpec((T, GATE_LANES), lambda b, h: (b, 0)),
                pl.BlockSpec((1, 1, HEAD_DIM), lambda b, h: (h, 0, 0))]
    args = [p_main, p_main, p_main, p_main, gates, w_hnorm.reshape(N_HEADS, 1, HEAD_DIM)]
    has_state = state is not None
    if has_state:
        C0, n0, m0 = state
        in_specs += [
            pl.BlockSpec(memory_space=pltpu.SMEM),
            pl.BlockSpec((1, N_DIR, 1, HEAD_DIM, HEAD_DIM), lambda b, h: (b, 0, h, 0, 0)),
            pl.BlockSpec((1, N_DIR, 1, 1, HEAD_DIM), lambda b, h: (b, 0, h, 0, 0)),
        ]
        args += [m0.reshape(-1), C0, n0.reshape(batch, N_DIR, N_HEADS, 1, HEAD_DIM)]
    out_specs = [pl.BlockSpec((T, HEAD_DIM), lambda b, h: (b, h))]
    out_shape = [jax.ShapeDtypeStruct((n_tok, D_MLSTM), BF16)]
    if emit_state:
        out_specs += [
            pl.BlockSpec((1, 1, N_DIR, 1, HEAD_DIM, HEAD_DIM), lambda b, h: (b, 0, 0, h, 0, 0)),
            pl.BlockSpec((1, 1, N_DIR, 1, 1, HEAD_DIM), lambda b, h: (b, 0, 0, h, 0, 0)),
            pl.BlockSpec((1, N_DIR, 1, 1, GATE_LANES), lambda b, h: (b, 0, h, 0, 0)),
        ]
        out_shape += [
            jax.ShapeDtypeStruct((batch, DEPTH, N_DIR, N_HEADS, HEAD_DIM, HEAD_DIM), F32),
            jax.ShapeDtypeStruct((batch, DEPTH, N_DIR, N_HEADS, 1, HEAD_DIM), F32),
            jax.ShapeDtypeStruct((batch, N_DIR, N_HEADS, 1, GATE_LANES), F32),
        ]
    nc = T // CHUNK
    scratch = [pltpu.VMEM((N_DIR, T, HEAD_DIM), F32),
               pltpu.VMEM((N_DIR, 3, T, GATE_LANES), F32),
               pltpu.VMEM((N_DIR, nc, HEAD_DIM, EXT), F32),
               pltpu.VMEM((N_DIR, nc, 2, 8, GATE_LANES), F32),
               pltpu.VMEM((N_DIR, HEAD_DIM, EXT), F32),
               pltpu.VMEM((T, HEAD_DIM), F32)]
    return pl.pallas_call(
        functools.partial(_mlstm_kernel, seq_len=T, has_state=has_state, emit_state=emit_state),
        grid=(batch, N_HEADS),
        in_specs=in_specs,
        out_specs=out_specs,
        out_shape=out_shape,
        scratch_shapes=scratch,
        compiler_params=_params(("parallel", "parallel")),
        name="mlstm",
    )(*args)


def _fourier_kernel(x_ref, cs_ref, ct_ref, st_ref, o_ref, *, scale):
    for g in range(N_FGROUPS):
        cols = slice(g * FGROUP_DIM, (g + 1) * FGROUP_DIM)
        z = _dot(x_ref[:, cols], cs_ref[...])
        zc = z[:, :FGROUP_DIM].astype(BF16)
        zs = z[:, FGROUP_DIM:].astype(BF16)
        y = _dot(ct_ref[...], zc) - _dot(st_ref[...], zs)
        o_ref[:, cols] = (y * scale).astype(BF16)


def _dft_tables(n):
    k = jnp.arange(n, dtype=jnp.int32)
    ang = ((k[:, None] * k[None, :]) % n).astype(F32) * (2.0 * jnp.pi / n)
    return jnp.cos(ang), jnp.sin(ang)


DFT_BLOCK = 256
DFT_RADIX = 8
HALF_LANES = 128


def _cadd(a, b):
    return a[0] + b[0], a[1] + b[1]


def _csub(a, b):
    return a[0] - b[0], a[1] - b[1]


def _mul_neg_i(a):
    return a[1], -a[0]


def _dft4(y):
    c0, c1 = _cadd(y[0], y[2]), _cadd(y[1], y[3])
    d0, d1 = _csub(y[0], y[2]), _mul_neg_i(_csub(y[1], y[3]))
    return [_cadd(c0, c1), _cadd(d0, d1), _csub(c0, c1), _csub(d0, d1)]


def _dft8(x):
    r = 0.5 ** 0.5
    a = [_cadd(x[n], x[n + 4]) for n in range(4)]
    b = [_csub(x[n], x[n + 4]) for n in range(4)]
    b[1] = ((b[1][0] + b[1][1]) * r, (b[1][1] - b[1][0]) * r)
    b[2] = _mul_neg_i(b[2])
    b[3] = ((b[3][1] - b[3][0]) * r, -(b[3][0] + b[3][1]) * r)
    even, odd = _dft4(a), _dft4(b)
    return [even[k // 2] if k % 2 == 0 else odd[k // 2] for k in range(8)]


def _fourier_long_kernel(x_ref, cs_ref, twc_ref, tws_ref, o_ref, w_scr, *, scale):
    nb = DFT_BLOCK
    cs = cs_ref[...]
    c_tab, s_tab = cs[:, :nb], cs[:, nb:]
    z = []
    for t1 in range(DFT_RADIX):
        zz = _dot(x_ref[t1 * nb:(t1 + 1) * nb, :], cs)
        z.append((zz[:, :nb], -zz[:, nb:]))
    a = _dft8(z)
    for u1 in range(DFT_RADIX):
        ar, ai = a[u1]
        if u1 > 0:
            twc = twc_ref[u1 * nb:(u1 + 1) * nb, :]
            tws = tws_ref[u1 * nb:(u1 + 1) * nb, :]
            ar, ai = ar * twc + ai * tws, ai * twc - ar * tws
        y = (_dot(c_tab, ar.astype(BF16)) + _dot(s_tab, ai.astype(BF16))) * scale
        for half in range(FGROUP_DIM // HALF_LANES):
            w_scr[half, pl.ds(u1, nb, stride=DFT_RADIX), :] = y[:, half * HALF_LANES:(half + 1) * HALF_LANES]
    for half in range(FGROUP_DIM // HALF_LANES):
        o_ref[:, half * HALF_LANES:(half + 1) * HALF_LANES] = w_scr[half].astype(BF16)


def _fourier_long(p_main, batch, seq_len):
    n_tok = p_main.shape[0]
    T = seq_len
    assert T == DFT_RADIX * DFT_BLOCK and FGROUP_DIM == DFT_BLOCK
    cc, sc = _dft_tables(FGROUP_DIM)
    cs = jnp.concatenate([cc, sc], axis=1).astype(BF16)
    u1 = jnp.repeat(jnp.arange(DFT_RADIX, dtype=jnp.int32), DFT_BLOCK)
    t2 = jnp.tile(jnp.arange(DFT_BLOCK, dtype=jnp.int32), DFT_RADIX)
    ang = ((u1 * t2) % T).astype(F32) * (2.0 * jnp.pi / T)
    twc = jnp.broadcast_to(jnp.cos(ang)[:, None], (T, FGROUP_DIM))
    tws = jnp.broadcast_to(jnp.sin(ang)[:, None], (T, FGROUP_DIM))
    return pl.pallas_call(
        functools.partial(_fourier_long_kernel, scale=float((T * FGROUP_DIM) ** -0.5)),
        grid=(batch, N_FGROUPS),
        in_specs=[
            pl.BlockSpec((T, FGROUP_DIM), lambda b, g: (b, COL_FR // FGROUP_DIM + g)),
            _resident((FGROUP_DIM, 2 * FGROUP_DIM), lambda b, g: (0, 0)),
            _resident((T, FGROUP_DIM), lambda b, g: (0, 0)),
            _resident((T, FGROUP_DIM), lambda b, g: (0, 0)),
        ],
        out_specs=pl.BlockSpec((T, FGROUP_DIM), lambda b, g: (b, g)),
        out_shape=jax.ShapeDtypeStruct((n_tok, D_FOURIER), BF16),
        scratch_shapes=[pltpu.VMEM((FGROUP_DIM // HALF_LANES, T, HALF_LANES), F32)],
        compiler_params=_params(("parallel", "parallel")),
        name="fourier_long",
    )(p_main, cs, twc, tws)


def _fourier(p_main, batch, seq_len):
    if seq_len == DFT_RADIX * DFT_BLOCK:
        return _fourier_long(p_main, batch, seq_len)
    n_tok = p_main.shape[0]
    T = seq_len
    cc, sc = _dft_tables(FGROUP_DIM)
    cs = jnp.concatenate([cc, sc], axis=1).astype(BF16)
    ct, st = _dft_tables(T)
    return pl.pallas_call(
        functools.partial(_fourier_kernel, scale=float((T * FGROUP_DIM) ** -0.5)),
        grid=(batch,),
        in_specs=[
            pl.BlockSpec((T, D_FOURIER), lambda b: (b, COL_FR // D_FOURIER)),
            _resident((FGROUP_DIM, 2 * FGROUP_DIM), lambda b: (0, 0)),
            _resident((T, T), lambda b: (0, 0)),
            _resident((T, T), lambda b: (0, 0)),
        ],
        out_specs=pl.BlockSpec((T, D_FOURIER), lambda b: (b, 0)),
        out_shape=jax.ShapeDtypeStruct((n_tok, D_FOURIER), BF16),
        compiler_params=_params(("parallel",)),
        name="fourier",
    )(p_main, cs, ct.astype(BF16), st.astype(BF16))


def _mix_kernel(*refs, tm, tiles_per_seq, has_pos):
    if has_pos:
        (hm_ref, fr_ref, ga0_ref, ga1_ref, gb0_ref, gb1_ref, x_ref, pr_ref, pc_ref, mod_ref,
         wm_ref, wf_ref, wo_ref, lg_ref, lb_ref, o_ref) = refs
    else:
        (hm_ref, fr_ref, ga0_ref, ga1_ref, gb0_ref, gb1_ref, x_ref, mod_ref,
         wm_ref, wf_ref, wo_ref, lg_ref, lb_ref, o_ref) = refs
    a = _dot(hm_ref[...], wm_ref[...])
    b = _dot(fr_ref[...], wf_ref[...])
    gate = lambda g0, g1: jax.nn.sigmoid(jnp.concatenate([g0[...], g1[...]], axis=1).astype(F32))
    mixed = gate(ga0_ref, ga1_ref) * a + gate(gb0_ref, gb1_ref) * b
    z = _dot(mixed.astype(BF16), wo_ref[...])
    x = x_ref[...]
    if has_pos:
        x = _add_grid_pos(x, pr_ref, pc_ref, (pl.program_id(0) % tiles_per_seq) * tm)
    g1 = mod_ref[0, 2:3, :]
    o_ref[...] = _norm(ALPHA * x + g1 * z) * lg_ref[...] + lb_ref[...]


def _mix(hm, fr, p_main, x2d, pos, mod6, w_br_m, w_br_f, w_out, ln_g, ln_b, seq_len, per_batch_mod):
    n_tok = x2d.shape[0]
    tm = 256
    tiles_per_seq = seq_len // tm
    has_pos = pos is not None
    mod_idx = (lambda i: (i // tiles_per_seq, 0, 0)) if per_batch_mod else (lambda i: (0, 0, 0))
    in_specs = [
        pl.BlockSpec((tm, D_MLSTM), lambda i: (i, 0)),
        pl.BlockSpec((tm, D_FOURIER), lambda i: (i, 0)),
        pl.BlockSpec((tm, GATE_BLOCK), lambda i: (i, COL_GA // GATE_BLOCK)),
        pl.BlockSpec((tm, GATE_BLOCK), lambda i: (i, COL_GA // GATE_BLOCK + 1)),
        pl.BlockSpec((tm, GATE_BLOCK), lambda i: (i, COL_GB // GATE_BLOCK)),
        pl.BlockSpec((tm, GATE_BLOCK), lambda i: (i, COL_GB // GATE_BLOCK + 1)),
        pl.BlockSpec((tm, D_MODEL), lambda i: (i, 0)),
    ]
    args = [hm, fr, p_main, p_main, p_main, p_main, x2d]
    if has_pos:
        in_specs += [_resident(p.shape, lambda i: (0, 0)) for p in pos]
        args += list(pos)
    in_specs += [
        pl.BlockSpec((1, 6, D_MODEL), mod_idx),
        _resident((D_MLSTM, D_MODEL), lambda i: (0, 0)),
        _resident((D_FOURIER, D_MODEL), lambda i: (0, 0)),
        _resident((D_MODEL, D_MODEL), lambda i: (0, 0)),
        _resident((1, D_MODEL), lambda i: (0, 0)),
        _resident((1, D_MODEL), lambda i: (0, 0)),
    ]
    args += [mod6, w_br_m, w_br_f, w_out, ln_g.reshape(1, D_MODEL), ln_b.reshape(1, D_MODEL)]
    return pl.pallas_call(
        functools.partial(_mix_kernel, tm=tm, tiles_per_seq=tiles_per_seq, has_pos=has_pos),
        grid=(n_tok // tm,),
        in_specs=in_specs,
        out_specs=pl.BlockSpec((tm, D_MODEL), lambda i: (i, 0)),
        out_shape=jax.ShapeDtypeStruct((n_tok, D_MODEL), F32),
        compiler_params=_params(("parallel",)),
        name="mix",
    )(*args)


def _ffn_kernel(*refs, tm, seg_len, has_halo, tiles_per_seq):
    if has_halo:
        (x_ref, xp_ref, xn_ref, mod_ref, wv_ref, wg_ref, cwv_ref, cwg_ref, cbv_ref, cbg_ref,
         wd_ref, lg_ref, lb_ref, o_ref, h_scr, acc_scr) = refs
    else:
        (x_ref, mod_ref, wv_ref, wg_ref, cwv_ref, cwg_ref, cbv_ref, cbg_ref,
         wd_ref, lg_ref, lb_ref, o_ref, h_scr, acc_scr) = refs
    f = pl.program_id(1)
    n_seg = tm // seg_len
    stride = seg_len + 2 * HALO
    rows = n_seg * stride
    zeros = jnp.zeros((HALO, D_MODEL), BF16)

    @pl.when(f == 0)
    def _():
        sh = mod_ref[0, 3:4, :]
        sc = mod_ref[0, 4:5, :]
        modulate = lambda x: _norm(x) * (1.0 + sc) + sh
        for s in range(n_seg):
            h_scr[s * stride + HALO:(s + 1) * stride - HALO, :] = (
                modulate(x_ref[s * seg_len:(s + 1) * seg_len, :]).astype(BF16))
            h_scr[s * stride:s * stride + HALO, :] = zeros
            h_scr[(s + 1) * stride - HALO:(s + 1) * stride, :] = zeros
        if has_halo:
            t = pl.program_id(0) % tiles_per_seq
            hp = jnp.where(t == 0, 0.0, modulate(xp_ref[...]))
            hn = jnp.where(t == tiles_per_seq - 1, 0.0, modulate(xn_ref[...]))
            h_scr[0:HALO, :] = hp.astype(BF16)
            h_scr[rows - HALO:rows, :] = hn.astype(BF16)
        acc_scr[...] = jnp.zeros_like(acc_scr)

    def conv(u, cw_ref, cb_ref):
        prev = pltpu.roll(u, 1, 0)
        nxt = pltpu.roll(u, rows - 1, 0)
        y = prev * cw_ref[0:1, :] + u * cw_ref[1:2, :] + nxt * cw_ref[2:3, :] + cb_ref[...]
        return y[HALO:rows - HALO, :]

    h = h_scr[...]
    val = conv(_dot(h, wv_ref[...]), cwv_ref, cbv_ref)
    gate = conv(_dot(h, wg_ref[...]), cwg_ref, cbg_ref)
    act = (gate * jax.nn.sigmoid(gate) * val).astype(BF16)
    acc_scr[...] += _dot(act, wd_ref[...])

    @pl.when(f == pl.num_programs(1) - 1)
    def _():
        g2 = mod_ref[0, 5:6, :]
        for s in range(n_seg):
            x = x_ref[s * seg_len:(s + 1) * seg_len, :]
            y = acc_scr[s * stride:s * stride + seg_len, :]
            o_ref[s * seg_len:(s + 1) * seg_len, :] = _norm(ALPHA * x + g2 * y) * lg_ref[...] + lb_ref[...]


def _ffn(x1, mod6, w_up, w_conv, b_conv, w_down, ln_g, ln_b, seq_len, per_batch_mod):
    n_tok = x1.shape[0]
    tf = 512
    nf = D_FF // tf
    tm = 512
    seg_len = min(seq_len, tm)
    tiles_per_seq = max(seq_len // tm, 1)
    has_halo = seq_len > tm
    h_rows = (tm // seg_len) * (seg_len + 2 * HALO)
    mod_idx = (lambda i, f: (i // tiles_per_seq, 0, 0)) if per_batch_mod else (lambda i, f: (0, 0, 0))
    hb = tm // HALO
    n_hblk = n_tok // HALO
    in_specs = [pl.BlockSpec((tm, D_MODEL), lambda i, f: (i, 0))]
    args = [x1]
    if has_halo:
        in_specs += [
            pl.BlockSpec((HALO, D_MODEL), lambda i, f: (jnp.maximum(i * hb - 1, 0), 0)),
            pl.BlockSpec((HALO, D_MODEL), lambda i, f: (jnp.minimum((i + 1) * hb, n_hblk - 1), 0)),
        ]
        args += [x1, x1]
    in_specs += [
        pl.BlockSpec((1, 6, D_MODEL), mod_idx),
        pl.BlockSpec((D_MODEL, tf), lambda i, f: (0, f)),
        pl.BlockSpec((D_MODEL, tf), lambda i, f: (0, nf + f)),
        pl.BlockSpec((3, tf), lambda i, f: (0, f)),
        pl.BlockSpec((3, tf), lambda i, f: (0, nf + f)),
        pl.BlockSpec((1, tf), lambda i, f: (0, f)),
        pl.BlockSpec((1, tf), lambda i, f: (0, nf + f)),
        pl.BlockSpec((tf, D_MODEL), lambda i, f: (f, 0)),
        _resident((1, D_MODEL), lambda i, f: (0, 0)),
        _resident((1, D_MODEL), lambda i, f: (0, 0)),
    ]
    b_conv2 = b_conv.reshape(1, 2 * D_FF)
    args += [mod6, w_up, w_up, w_conv, w_conv, b_conv2, b_conv2, w_down,
             ln_g.reshape(1, D_MODEL), ln_b.reshape(1, D_MODEL)]
    return pl.pallas_call(
        functools.partial(_ffn_kernel, tm=tm, seg_len=seg_len, has_halo=has_halo,
                          tiles_per_seq=tiles_per_seq),
        grid=(n_tok // tm, nf),
        in_specs=in_specs,
        out_specs=pl.BlockSpec((tm, D_MODEL), lambda i, f: (i, 0)),
        out_shape=jax.ShapeDtypeStruct((n_tok, D_MODEL), F32),
        scratch_shapes=[pltpu.VMEM((h_rows, D_MODEL), BF16), pltpu.VMEM((h_rows - 2 * HALO, D_MODEL), F32)],
        compiler_params=_params(("parallel", "arbitrary")),
        name="ffn",
    )(*args)


def _grid_pos_tables(n_tokens):
    quarter = D_MODEL // 4
    freq = 1.0 / (10000.0 ** (jnp.arange(quarter, dtype=F32) / quarter))
    er = jnp.arange(n_tokens // GRID_W, dtype=F32)[:, None] * freq
    ec = jnp.arange(GRID_W, dtype=F32)[:, None] * freq
    return (jnp.concatenate([jnp.sin(er), jnp.cos(er)], -1),
            jnp.concatenate([jnp.sin(ec), jnp.cos(ec)], -1))


def _split_w_in(w_in, b_gate):
    n_gate = 2 * N_DIR * N_HEADS
    w_main = jnp.concatenate([w_in[:, :4 * D_MLSTM], w_in[:, 4 * D_MLSTM + n_gate:]], axis=1).astype(BF16)
    gw = w_in[:, 4 * D_MLSTM:4 * D_MLSTM + n_gate].reshape(D_MODEL, N_DIR, 2, N_HEADS)
    gw = gw.transpose(0, 3, 1, 2).reshape(D_MODEL, n_gate)
    gw = jnp.pad(gw, ((0, 0), (0, GATE_LANES - n_gate))).astype(BF16)
    gb = b_gate.astype(F32).reshape(N_DIR, 2, N_HEADS).transpose(2, 0, 1).reshape(1, n_gate)
    gb = jnp.pad(gb, ((0, 0), (0, GATE_LANES - n_gate)))
    return w_main, gw, gb


def _layer(x2d, pos, mod6, weights, batch, seq_len, per_batch_mod, state, emit_state):
    (w_main, w_gate, b_gate, w_hnorm, w_br_m, w_br_f, w_out, ln1_g, ln1_b,
     w_up, w_conv, b_conv, w_down, ln2_g, ln2_b) = weights
    p_main, gates = _inproj(x2d, pos, mod6, w_main, w_gate, b_gate, seq_len, per_batch_mod)
    ml = _mlstm(p_main, gates, w_hnorm, batch, seq_len, state=state, emit_state=emit_state)
    fr = _fourier(p_main, batch, seq_len)
    x1 = _mix(ml[0], fr, p_main, x2d, pos, mod6, w_br_m, w_br_f, w_out, ln1_g, ln1_b,
              seq_len, per_batch_mod)
    x2 = _ffn(x1, mod6, w_up, w_conv, b_conv, w_down, ln2_g, ln2_b, seq_len, per_batch_mod)
    return x2, ml[1:]


def kernel(x_prompt, x_sample, c, state_C, state_n, state_m, c_ctx, w_ada, b_ada, w_in, b_gate,
           w_hnorm, w_br_m, w_br_f, w_out, ln1_g, ln1_b, w_up, w_conv, b_conv, w_down, ln2_g, ln2_b):
    assert w_ada.shape[0] == DEPTH
    B, S, _ = x_prompt.shape
    DB, DS, _ = x_sample.shape
    l = 0
    n_cond = 16
    cond = jnp.zeros((n_cond, D_MODEL), F32).at[0].set(c_ctx).at[1:1 + DB].set(c)
    mod6 = _modulation(cond, w_ada[l], b_ada[l]).reshape(n_cond, 6, D_MODEL)
    w_main, w_gate, b_gate_l = _split_w_in(w_in[l], b_gate[l])
    weights = (w_main, w_gate, b_gate_l, w_hnorm[l], w_br_m[l].astype(BF16), w_br_f[l].astype(BF16),
               w_out[l].astype(BF16), ln1_g[l], ln1_b[l], w_up[l].astype(BF16), w_conv[l], b_conv[l],
               w_down[l].astype(BF16), ln2_g[l], ln2_b[l])
    pos = _grid_pos_tables(DS)

    yp, states = _layer(x_prompt.reshape(B * S, D_MODEL), None, mod6[0:1], weights, B, S,
                        per_batch_mod=False, state=None, emit_state=True)
    ys, _ = _layer(x_sample.reshape(DB * DS, D_MODEL), pos, mod6[1:1 + DB], weights, DB, DS,
                   per_batch_mod=True, state=(state_C[:, l], state_n[:, l], state_m[:, l]),
                   emit_state=False)
    new_C, new_n, new_m = states
    new_n = new_n.reshape(B, DEPTH, N_DIR, N_HEADS, HEAD_DIM)
    new_m = new_m[:, :, :, 0, 0].reshape(B, DEPTH, N_DIR, N_HEADS)
    return (yp.reshape(B, S, D_MODEL), ys.reshape(DB, DS, D_MODEL), new_C, new_n, new_m)
```

```python
import functools

import jax
import jax.numpy as jnp
from jax import lax
from jax.experimental import pallas as pl
from jax.experimental.pallas import tpu as pltpu

D_MODEL = 2048
N_HEADS = 4
HEAD_DIM = 256
D_MLSTM = N_HEADS * HEAD_DIM
N_FGROUPS = 4
FGROUP_DIM = 256
D_FOURIER = N_FGROUPS * FGROUP_DIM
D_FF = 5632
GRID_W = 64
N_DIR = 2
DEPTH = 1
ALPHA = (2.0 * DEPTH) ** 0.25
LN_EPS = 1e-5

F32 = jnp.float32
BF16 = jnp.bfloat16

CHUNK = 256
COL_Q = 0
COL_K = COL_Q + D_MLSTM
COL_V = COL_K + D_MLSTM
COL_O = COL_V + D_MLSTM
COL_FR = COL_O + D_MLSTM
COL_GA = COL_FR + D_FOURIER
COL_GB = COL_GA + D_MODEL
D_MAIN = COL_GB + D_MODEL
GATE_BLOCK = 1024
GATE_LANES = 128
HALO = 8

VMEM_LIMIT = 56 * 1024 * 1024


def _params(sem, flags=None):
    return pltpu.CompilerParams(dimension_semantics=sem, vmem_limit_bytes=VMEM_LIMIT, flags=flags)


def _resident(shape, index_map):
    return pl.BlockSpec(shape, index_map, pipeline_mode=pl.Buffered(1))


def _norm(x):
    mu = jnp.mean(x, axis=-1, keepdims=True)
    xc = x - mu
    var = jnp.mean(xc * xc, axis=-1, keepdims=True)
    return xc * lax.rsqrt(var + LN_EPS)


def _dot(a, b):
    return jnp.dot(a, b, preferred_element_type=F32)


def _dot_nt(a, b):
    return lax.dot_general(a, b, (((1,), (1,)), ((), ())), preferred_element_type=F32)


def _mod_kernel(c_ref, w_ref, b_ref, o_ref):
    c = c_ref[...]
    s = c * jax.nn.sigmoid(c)
    o_ref[...] = _dot(s.astype(BF16), w_ref[...].astype(BF16)) + b_ref[...]


def _modulation(cond, w_ada, b_ada):
    rows, tn = cond.shape[0], 1024
    n_out = w_ada.shape[1]
    return pl.pallas_call(
        _mod_kernel,
        grid=(n_out // tn,),
        in_specs=[
            _resident((rows, D_MODEL), lambda j: (0, 0)),
            pl.BlockSpec((D_MODEL, tn), lambda j: (0, j)),
            pl.BlockSpec((1, tn), lambda j: (0, j)),
        ],
        out_specs=pl.BlockSpec((rows, tn), lambda j: (0, j)),
        out_shape=jax.ShapeDtypeStruct((rows, n_out), F32),
        compiler_params=_params(("arbitrary",)),
        name="modulation",
    )(cond, w_ada, b_ada.reshape(1, n_out))


INPROJ_TN = 2304
LN_ROWS = 256


def _add_grid_pos(x, pr_ref, pc_ref, tok0):
    half = D_MODEL // 2
    out = []
    for k in range(x.shape[0] // GRID_W):
        blk = x[k * GRID_W:(k + 1) * GRID_W, :]
        pr = pr_ref[pl.ds(tok0 // GRID_W + k, 1), :]
        out.append(jnp.concatenate([blk[:, :half] + pr, blk[:, half:] + pc_ref[...]], axis=1))
    return jnp.concatenate(out, axis=0)


def _inproj_kernel(*refs, tm, tiles_per_seq, has_pos):
    if has_pos:
        x_ref, pr_ref, pc_ref, mod_ref, w_ref, wg_ref, bg_ref, p_ref, g_ref, h_scr = refs
    else:
        x_ref, mod_ref, w_ref, wg_ref, bg_ref, p_ref, g_ref, h_scr = refs

    @pl.when(pl.program_id(1) == 0)
    def _():
        sh = mod_ref[0, 0:1, :]
        sc = mod_ref[0, 1:2, :]

        def ln_rows(r, carry):
            rows = pl.ds(pl.multiple_of(r * LN_ROWS, LN_ROWS), LN_ROWS)
            x = x_ref[rows, :]
            if has_pos:
                x = _add_grid_pos(x, pr_ref, pc_ref,
                                  (pl.program_id(0) % tiles_per_seq) * tm + r * LN_ROWS)
            h = (_norm(x) * (1.0 + sc) + sh).astype(BF16)
            h_scr[rows, :] = h
            g_ref[rows, :] = _dot(h, wg_ref[...]) + bg_ref[...]
            return carry

        lax.fori_loop(0, tm // LN_ROWS, ln_rows, 0)

    p_ref[...] = _dot(h_scr[...], w_ref[...]).astype(BF16)


def _inproj(x2d, pos, mod6, w_main, w_gate, b_gate, seq_len, per_batch_mod):
    n_tok = x2d.shape[0]
    tm, tn = 1024, INPROJ_TN
    tiles_per_seq = max(seq_len // tm, 1)
    has_pos = pos is not None
    mod_idx = (lambda i, j: (i // tiles_per_seq, 0, 0)) if per_batch_mod else (lambda i, j: (0, 0, 0))
    in_specs = [pl.BlockSpec((tm, D_MODEL), lambda i, j: (i, 0))]
    args = [x2d]
    if has_pos:
        in_specs += [_resident(p.shape, lambda i, j: (0, 0)) for p in pos]
        args += list(pos)
    in_specs += [
        pl.BlockSpec((1, 6, D_MODEL), mod_idx),
        pl.BlockSpec((D_MODEL, tn), lambda i, j: (0, j)),
        _resident((D_MODEL, GATE_LANES), lambda i, j: (0, 0)),
        _resident((1, GATE_LANES), lambda i, j: (0, 0)),
    ]
    args += [mod6, w_main, w_gate, b_gate]
    return pl.pallas_call(
        functools.partial(_inproj_kernel, tm=tm, tiles_per_seq=tiles_per_seq, has_pos=has_pos),
        grid=(n_tok // tm, D_MAIN // tn),
        in_specs=in_specs,
        out_specs=[
            pl.BlockSpec((tm, tn), lambda i, j: (i, j)),
            pl.BlockSpec((tm, GATE_LANES), lambda i, j: (i, 0)),
        ],
        out_shape=[
            jax.ShapeDtypeStruct((n_tok, D_MAIN), BF16),
            jax.ShapeDtypeStruct((n_tok, GATE_LANES), F32),
        ],
        scratch_shapes=[pltpu.VMEM((tm, D_MODEL), BF16)],
        compiler_params=_params(("parallel", "arbitrary")),
        name="inproj",
    )(*args)


def _log_sigmoid(x):
    return jnp.minimum(x, 0.0) - jnp.log1p(jnp.exp(-jnp.abs(x)))


def _head_out(h, o, wn):
    return (jax.nn.sigmoid(o.astype(F32)) * (_norm(h) * wn)).astype(BF16)


EXT = HEAD_DIM + GATE_LANES
PASS1_GROUP = 4


def _lane_tile(x, n):
    return jnp.concatenate([x] * n, axis=1)


def _mlstm_kernel(*refs, seq_len, has_state, emit_state):
    refs = list(refs)
    q_ref, k_ref, v_ref, o_ref, g_ref, wn_ref = refs[:6]
    refs = refs[6:]
    if has_state:
        m0_ref, c0_ref, n0_ref = refs[:3]
        refs = refs[3:]
    hm_ref = refs[0]
    refs = refs[1:]
    if emit_state:
        cout_ref, nout_ref, mout_ref = refs[:3]
        refs = refs[3:]
    num_scr, row_scr, kv_scr, sc_scr, c_scr, h_scr = refs
    L = CHUNK
    nc = seq_len // L
    scale = HEAD_DIM ** -0.5
    head = pl.program_id(1)
    gate_shift = jnp.where(head == 0, 0, GATE_LANES - 2 * N_DIR * head)

    def chunk_rows(c):
        return pl.ds(pl.multiple_of(c * L, L), L)

    def load_q(rows):
        return (q_ref[rows, :].astype(F32) * scale).astype(BF16)

    t_idx = lax.broadcasted_iota(jnp.int32, (L, L), 0)
    s_idx = lax.broadcasted_iota(jnp.int32, (L, L), 1)
    hi_rows = lax.broadcasted_iota(jnp.int32, (16, L), 0) < 8

    masks = [s_idx <= t_idx, s_idx >= t_idx]
    masks_b = [jnp.where(mk, 1.0, 0.0).astype(BF16) for mk in masks]
    group = min(nc, PASS1_GROUP)

    def pass1(grp, carry):
        chunks = [grp * group + i for i in range(group)]
        base, items = [], []
        for c in chunks:
            rows = chunk_rows(c)
            qb, kb, vb = load_q(rows), k_ref[rows, :], v_ref[rows, :]
            v_ext = jnp.concatenate([vb, jnp.ones((L, GATE_LANES), BF16)], axis=1)
            s0 = _dot_nt(qb, kb)
            g_row = jnp.transpose(pltpu.roll(g_ref[rows, :], gate_shift, 1))
            h_scr[rows, :] = jnp.zeros((L, HEAD_DIM), F32)
            base.append((c, rows, kb, v_ext, s0))
            for d in range(N_DIR):
                ig_row = g_row[2 * d:2 * d + 1, :]
                lf = _log_sigmoid(g_row[2 * d + 1:2 * d + 2, :])
                lf_hi = lf.astype(BF16)
                lf_lo = (lf - lf_hi.astype(F32)).astype(BF16)
                lhs = jnp.where(hi_rows, lf_hi.astype(F32), lf_lo.astype(F32)).astype(BF16)
                r16 = _dot_nt(lhs, masks_b[d])
                lf_rep = jnp.concatenate([jnp.broadcast_to(lf_hi, (GATE_LANES, L)),
                                          jnp.broadcast_to(lf_lo, (GATE_LANES, L))], axis=0)
                bb = _dot_nt(masks_b[d], lf_rep)
                items.append((len(base) - 1, d, ig_row, lf, r16, bb))
        states = []
        for bi, d, ig_row, lf, r16, bb in items:
            c, rows, kb, v_ext, s0 = base[bi]
            b_row = r16[0:1, :] + r16[8:9, :]
            b_rep = bb[:, :GATE_LANES] + bb[:, GATE_LANES:]
            c_row = ig_row - b_row
            c_max = jnp.broadcast_to(
                jnp.max(jnp.where(masks[d], c_row, -jnp.inf), axis=1, keepdims=True), (L, L))
            w = jnp.exp(jnp.where(masks[d], c_row - c_max, -jnp.inf))
            pv = _dot((s0 * w).astype(BF16), v_ext)
            num_scr[d, rows, :] = pv[:, :HEAD_DIM]
            row_scr[d, 0, rows, :] = pv[:, HEAD_DIM:]
            row_scr[d, 1, rows, :] = b_rep + c_max[:, :GATE_LANES]
            row_scr[d, 2, rows, :] = b_rep
            states.append((bi, d, ig_row, lf, b_row))
        k_ts = [jnp.transpose(kb.astype(F32)) for (_, _, kb, _, _) in base]
        for bi, d, ig_row, lf, b_row in states:
            c, rows, kb, v_ext, s0 = base[bi]
            b_last = jnp.sum(lf, axis=1, keepdims=True)
            g = b_last - b_row + ig_row
            g_max = jnp.max(g, axis=1, keepdims=True)
            wk = jnp.exp(g - g_max)
            kv_scr[d, c] = _dot((k_ts[bi] * wk).astype(BF16), v_ext)
            sc_scr[d, c, 0] = jnp.broadcast_to(b_last, (8, GATE_LANES))
            sc_scr[d, c, 1] = jnp.broadcast_to(g_max, (8, GATE_LANES))
        return carry

    lax.fori_loop(0, nc // group, pass1, 0)

    ms = []
    for d in range(N_DIR):
        if has_state:
            bh = pl.program_id(0) * (N_DIR * N_HEADS) + d * N_HEADS + head
            n_rep = jnp.transpose(jnp.broadcast_to(n0_ref[0, d, 0], (GATE_LANES, HEAD_DIM)))
            c_scr[d] = jnp.concatenate([c0_ref[0, d, 0], n_rep], axis=1)
            ms.append(jnp.full((1, GATE_LANES), m0_ref[bh], F32))
        else:
            c_scr[d] = jnp.zeros((HEAD_DIM, EXT), F32)
            ms.append(jnp.zeros((1, GATE_LANES), F32))

    def pass2(j, carry):
        new = []
        for d, m in enumerate(carry):
            c = j if d == 0 else nc - 1 - j
            rows = chunk_rows(c)
            c_ext = c_scr[d]
            qcn = _dot(load_q(rows), c_ext.astype(BF16))
            den_i, a_rep, b_rep = row_scr[d, 0, rows, :], row_scr[d, 1, rows, :], row_scr[d, 2, rows, :]
            m_rep = jnp.maximum(b_rep + m, a_rep)
            r_intra = jnp.exp(a_rep - m_rep)
            r_state = jnp.exp(b_rep + m - m_rep)
            den = r_intra * den_i + r_state * qcn[:, HEAD_DIM:]
            inv = 1.0 / jnp.maximum(jnp.abs(den), jnp.exp(-m_rep))
            h = (_lane_tile(r_intra * inv, 2) * num_scr[d, rows, :]
                 + _lane_tile(r_state * inv, 2) * qcn[:, :HEAD_DIM])
            h_scr[rows, :] += h
            b_last, g_max = sc_scr[d, c, 0][0:1, :], sc_scr[d, c, 1][0:1, :]
            m_new = jnp.maximum(b_last + m, g_max)
            decay = jnp.exp(b_last + m - m_new)
            gain = jnp.exp(g_max - m_new)
            c_scr[d] = _lane_tile(decay, 3) * c_ext + _lane_tile(gain, 3) * kv_scr[d, c]
            new.append(m_new)
        return tuple(new)

    ms = lax.fori_loop(0, nc, pass2, tuple(ms))

    wn = wn_ref[0]

    def finish(c, carry):
        rows = chunk_rows(c)
        hm_ref[rows, :] = _head_out(h_scr[rows, :], o_ref[rows, :], wn)
        return carry

    lax.fori_loop(0, nc, finish, 0)
    if emit_state:
        for d in range(N_DIR):
            c_ext = c_scr[d]
            cout_ref[0, 0, d, 0] = c_ext[:, :HEAD_DIM]
            nout_ref[0, 0, d, 0] = jnp.transpose(c_ext[:, HEAD_DIM:])[0:1, :]
            mout_ref[0, d, 0] = ms[d]


def _mlstm(p_main, gates, w_hnorm, batch, seq_len, state=None, emit_state=False):
    n_tok = p_main.shape[0]
    T = seq_len
    blk = lambda col: pl.BlockSpec((T, HEAD_DIM), lambda b, h: (b, col // HEAD_DIM + h))
    in_specs = [blk(COL_Q), blk(COL_K), blk(COL_V), blk(COL_O),
                pl.BlockSpec((T, GATE_LANES), lambda b, h: (b, 0)),
                pl.BlockSpec((1, 1, HEAD_DIM), lambda b, h: (h, 0, 0))]
    args = [p_main, p_main, p_main, p_main, gates, w_hnorm.reshape(N_HEADS, 1, HEAD_DIM)]
    has_state = state is not None
    if has_state:
        C0, n0, m0 = state
        in_specs += [
            pl.BlockSpec(memory_space=pltpu.SMEM),
            pl.BlockSpec((1, N_DIR, 1, HEAD_DIM, HEAD_DIM), lambda b, h: (b, 0, h, 0, 0)),
            pl.BlockSpec((1, N_DIR, 1, 1, HEAD_DIM), lambda b, h: (b, 0, h, 0, 0)),
        ]
        args += [m0.reshape(-1), C0, n0.reshape(batch, N_DIR, N_HEADS, 1, HEAD_DIM)]
    out_specs = [pl.BlockSpec((T, HEAD_DIM), lambda b, h: (b, h))]
    out_shape = [jax.ShapeDtypeStruct((n_tok, D_MLSTM), BF16)]
    if emit_state:
        out_specs += [
            pl.BlockSpec((1, 1, N_DIR, 1, HEAD_DIM, HEAD_DIM), lambda b, h: (b, 0, 0, h, 0, 0)),
            pl.BlockSpec((1, 1, N_DIR, 1, 1, HEAD_DIM), lambda b, h: (b, 0, 0, h, 0, 0)),
            pl.BlockSpec((1, N_DIR, 1, 1, GATE_LANES), lambda b, h: (b, 0, h, 0, 0)),
        ]
        out_shape += [
            jax.ShapeDtypeStruct((batch, DEPTH, N_DIR, N_HEADS, HEAD_DIM, HEAD_DIM), F32),
            jax.ShapeDtypeStruct((batch, DEPTH, N_DIR, N_HEADS, 1, HEAD_DIM), F32),
            jax.ShapeDtypeStruct((batch, N_DIR, N_HEADS, 1, GATE_LANES), F32),
        ]
    nc = T // CHUNK
    scratch = [pltpu.VMEM((N_DIR, T, HEAD_DIM), F32),
               pltpu.VMEM((N_DIR, 3, T, GATE_LANES), F32),
               pltpu.VMEM((N_DIR, nc, HEAD_DIM, EXT), F32),
               pltpu.VMEM((N_DIR, nc, 2, 8, GATE_LANES), F32),
               pltpu.VMEM((N_DIR, HEAD_DIM, EXT), F32),
               pltpu.VMEM((T, HEAD_DIM), F32)]
    return pl.pallas_call(
        functools.partial(_mlstm_kernel, seq_len=T, has_state=has_state, emit_state=emit_state),
        grid=(batch, N_HEADS),
        in_specs=in_specs,
        out_specs=out_specs,
        out_shape=out_shape,
        scratch_shapes=scratch,
        compiler_params=_params(("parallel", "parallel")),
        name="mlstm",
    )(*args)


def _fourier_kernel(x_ref, cs_ref, ct_ref, st_ref, o_ref, *, scale):
    for g in range(N_FGROUPS):
        cols = slice(g * FGROUP_DIM, (g + 1) * FGROUP_DIM)
        z = _dot(x_ref[:, cols], cs_ref[...])
        zc = z[:, :FGROUP_DIM].astype(BF16)
        zs = z[:, FGROUP_DIM:].astype(BF16)
        y = _dot(ct_ref[...], zc) - _dot(st_ref[...], zs)
        o_ref[:, cols] = (y * scale).astype(BF16)


def _dft_tables(n):
    k = jnp.arange(n, dtype=jnp.int32)
    ang = ((k[:, None] * k[None, :]) % n).astype(F32) * (2.0 * jnp.pi / n)
    return jnp.cos(ang), jnp.sin(ang)


DFT_BLOCK = 256
DFT_RADIX = 8
HALF_LANES = 128


def _cadd(a, b):
    return a[0] + b[0], a[1] + b[1]


def _csub(a, b):
    return a[0] - b[0], a[1] - b[1]


def _mul_neg_i(a):
    return a[1], -a[0]


def _dft4(y):
    c0, c1 = _cadd(y[0], y[2]), _cadd(y[1], y[3])
    d0, d1 = _csub(y[0], y[2]), _mul_neg_i(_csub(y[1], y[3]))
    return [_cadd(c0, c1), _cadd(d0, d1), _csub(c0, c1), _csub(d0, d1)]


def _dft8(x):
    r = 0.5 ** 0.5
    a = [_cadd(x[n], x[n + 4]) for n in range(4)]
    b = [_csub(x[n], x[n + 4]) for n in range(4)]
    b[1] = ((b[1][0] + b[1][1]) * r, (b[1][1] - b[1][0]) * r)
    b[2] = _mul_neg_i(b[2])
    b[3] = ((b[3][1] - b[3][0]) * r, -(b[3][0] + b[3][1]) * r)
    even, odd = _dft4(a), _dft4(b)
    return [even[k // 2] if k % 2 == 0 else odd[k // 2] for k in range(8)]


def _fourier_long_kernel(x_ref, cs_ref, twc_ref, tws_ref, o_ref, w_scr, *, scale):
    nb = DFT_BLOCK
    cs = cs_ref[...]
    c_tab, s_tab = cs[:, :nb], cs[:, nb:]
    z = []
    for t1 in range(DFT_RADIX):
        zz = _dot(x_ref[t1 * nb:(t1 + 1) * nb, :], cs)
        z.append((zz[:, :nb], -zz[:, nb:]))
    a = _dft8(z)
    for u1 in range(DFT_RADIX):
        ar, ai = a[u1]
        if u1 > 0:
            twc = twc_ref[u1 * nb:(u1 + 1) * nb, :]
            tws = tws_ref[u1 * nb:(u1 + 1) * nb, :]
            ar, ai = ar * twc + ai * tws, ai * twc - ar * tws
        y = (_dot(c_tab, ar.astype(BF16)) + _dot(s_tab, ai.astype(BF16))) * scale
        for half in range(FGROUP_DIM // HALF_LANES):
            w_scr[half, pl.ds(u1, nb, stride=DFT_RADIX), :] = y[:, half * HALF_LANES:(half + 1) * HALF_LANES]
    for half in range(FGROUP_DIM // HALF_LANES):
        o_ref[:, half * HALF_LANES:(half + 1) * HALF_LANES] = w_scr[half].astype(BF16)


def _fourier_long(p_main, batch, seq_len):
    n_tok = p_main.shape[0]
    T = seq_len
    assert T == DFT_RADIX * DFT_BLOCK and FGROUP_DIM == DFT_BLOCK
    cc, sc = _dft_tables(FGROUP_DIM)
    cs = jnp.concatenate([cc, sc], axis=1).astype(BF16)
    u1 = jnp.repeat(jnp.arange(DFT_RADIX, dtype=jnp.int32), DFT_BLOCK)
    t2 = jnp.tile(jnp.arange(DFT_BLOCK, dtype=jnp.int32), DFT_RADIX)
    ang = ((u1 * t2) % T).astype(F32) * (2.0 * jnp.pi / T)
    twc = jnp.broadcast_to(jnp.cos(ang)[:, None], (T, FGROUP_DIM))
    tws = jnp.broadcast_to(jnp.sin(ang)[:, None], (T, FGROUP_DIM))
    return pl.pallas_call(
        functools.partial(_fourier_long_kernel, scale=float((T * FGROUP_DIM) ** -0.5)),
        grid=(batch, N_FGROUPS),
        in_specs=[
            pl.BlockSpec((T, FGROUP_DIM), lambda b, g: (b, COL_FR // FGROUP_DIM + g)),
            _resident((FGROUP_DIM, 2 * FGROUP_DIM), lambda b, g: (0, 0)),
            _resident((T, FGROUP_DIM), lambda b, g: (0, 0)),
            _resident((T, FGROUP_DIM), lambda b, g: (0, 0)),
        ],
        out_specs=pl.BlockSpec((T, FGROUP_DIM), lambda b, g: (b, g)),
        out_shape=jax.ShapeDtypeStruct((n_tok, D_FOURIER), BF16),
        scratch_shapes=[pltpu.VMEM((FGROUP_DIM // HALF_LANES, T, HALF_LANES), F32)],
        compiler_params=_params(("parallel", "parallel")),
        name="fourier_long",
    )(p_main, cs, twc, tws)


def _fourier(p_main, batch, seq_len):
    if seq_len == DFT_RADIX * DFT_BLOCK:
        return _fourier_long(p_main, batch, seq_len)
    n_tok = p_main.shape[0]
    T = seq_len
    cc, sc = _dft_tables(FGROUP_DIM)
    cs = jnp.concatenate([cc, sc], axis=1).astype(BF16)
    ct, st = _dft_tables(T)
    return pl.pallas_call(
        functools.partial(_fourier_kernel, scale=float((T * FGROUP_DIM) ** -0.5)),
        grid=(batch,),
        in_specs=[
            pl.BlockSpec((T, D_FOURIER), lambda b: (b, COL_FR // D_FOURIER)),
            _resident((FGROUP_DIM, 2 * FGROUP_DIM), lambda b: (0, 0)),
            _resident((T, T), lambda b: (0, 0)),
            _resident((T, T), lambda b: (0, 0)),
        ],
        out_specs=pl.BlockSpec((T, D_FOURIER), lambda b: (b, 0)),
        out_shape=jax.ShapeDtypeStruct((n_tok, D_FOURIER), BF16),
        compiler_params=_params(("parallel",)),
        name="fourier",
    )(p_main, cs, ct.astype(BF16), st.astype(BF16))


def _mix_kernel(*refs, tm, tiles_per_seq, has_pos):
    if has_pos:
        (hm_ref, fr_ref, ga0_ref, ga1_ref, gb0_ref, gb1_ref, x_ref, pr_ref, pc_ref, mod_ref,
         wm_ref, wf_ref, wo_ref, lg_ref, lb_ref, o_ref) = refs
    else:
        (hm_ref, fr_ref, ga0_ref, ga1_ref, gb0_ref, gb1_ref, x_ref, mod_ref,
         wm_ref, wf_ref, wo_ref, lg_ref, lb_ref, o_ref) = refs
    a = _dot(hm_ref[...], wm_ref[...])
    b = _dot(fr_ref[...], wf_ref[...])
    gate = lambda g0, g1: jax.nn.sigmoid(jnp.concatenate([g0[...], g1[...]], axis=1).astype(F32))
    mixed = gate(ga0_ref, ga1_ref) * a + gate(gb0_ref, gb1_ref) * b
    z = _dot(mixed.astype(BF16), wo_ref[...])
    x = x_ref[...]
    if has_pos:
        x = _add_grid_pos(x, pr_ref, pc_ref, (pl.program_id(0) % tiles_per_seq) * tm)
    g1 = mod_ref[0, 2:3, :]
    o_ref[...] = _norm(ALPHA * x + g1 * z) * lg_ref[...] + lb_ref[...]


def _mix(hm, fr, p_main, x2d, pos, mod6, w_br_m, w_br_f, w_out, ln_g, ln_b, seq_len, per_batch_mod):
    n_tok = x2d.shape[0]
    tm = 256
    tiles_per_seq = seq_len // tm
    has_pos = pos is not None
    mod_idx = (lambda i: (i // tiles_per_seq, 0, 0)) if per_batch_mod else (lambda i: (0, 0, 0))
    in_specs = [
        pl.BlockSpec((tm, D_MLSTM), lambda i: (i, 0)),
        pl.BlockSpec((tm, D_FOURIER), lambda i: (i, 0)),
        pl.BlockSpec((tm, GATE_BLOCK), lambda i: (i, COL_GA // GATE_BLOCK)),
        pl.BlockSpec((tm, GATE_BLOCK), lambda i: (i, COL_GA // GATE_BLOCK + 1)),
        pl.BlockSpec((tm, GATE_BLOCK), lambda i: (i, COL_GB // GATE_BLOCK)),
        pl.BlockSpec((tm, GATE_BLOCK), lambda i: (i, COL_GB // GATE_BLOCK + 1)),
        pl.BlockSpec((tm, D_MODEL), lambda i: (i, 0)),
    ]
    args = [hm, fr, p_main, p_main, p_main, p_main, x2d]
    if has_pos:
        in_specs += [_resident(p.shape, lambda i: (0, 0)) for p in pos]
        args += list(pos)
    in_specs += [
        pl.BlockSpec((1, 6, D_MODEL), mod_idx),
        _resident((D_MLSTM, D_MODEL), lambda i: (0, 0)),
        _resident((D_FOURIER, D_MODEL), lambda i: (0, 0)),
        _resident((D_MODEL, D_MODEL), lambda i: (0, 0)),
        _resident((1, D_MODEL), lambda i: (0, 0)),
        _resident((1, D_MODEL), lambda i: (0, 0)),
    ]
    args += [mod6, w_br_m, w_br_f, w_out, ln_g.reshape(1, D_MODEL), ln_b.reshape(1, D_MODEL)]
    return pl.pallas_call(
        functools.partial(_mix_kernel, tm=tm, tiles_per_seq=tiles_per_seq, has_pos=has_pos),
        grid=(n_tok // tm,),
        in_specs=in_specs,
        out_specs=pl.BlockSpec((tm, D_MODEL), lambda i: (i, 0)),
        out_shape=jax.ShapeDtypeStruct((n_tok, D_MODEL), F32),
        compiler_params=_params(("parallel",)),
        name="mix",
    )(*args)


def _ffn_kernel(*refs, tm, seg_len, has_halo, tiles_per_seq):
    if has_halo:
        (x_ref, xp_ref, xn_ref, mod_ref, wv_ref, wg_ref, cwv_ref, cwg_ref, cbv_ref, cbg_ref,
         wd_ref, lg_ref, lb_ref, o_ref, h_scr) = refs
        acc_scr = o_ref
    else:
        (x_ref, mod_ref, wv_ref, wg_ref, cwv_ref, cwg_ref, cbv_ref, cbg_ref,
         wd_ref, lg_ref, lb_ref, o_ref, h_scr, acc_scr) = refs
    f = pl.program_id(1)
    n_seg = tm // seg_len
    stride = seg_len + 2 * HALO
    rows = n_seg * stride
    zeros = jnp.zeros((HALO, D_MODEL), BF16)

    @pl.when(f == 0)
    def _():
        sh = mod_ref[0, 3:4, :]
        sc = mod_ref[0, 4:5, :]
        modulate = lambda x: _norm(x) * (1.0 + sc) + sh
        for s in range(n_seg):
            h_scr[s * stride + HALO:(s + 1) * stride - HALO, :] = (
                modulate(x_ref[s * seg_len:(s + 1) * seg_len, :]).astype(BF16))
            h_scr[s * stride:s * stride + HALO, :] = zeros
            h_scr[(s + 1) * stride - HALO:(s + 1) * stride, :] = zeros
        if has_halo:
            t = pl.program_id(0) % tiles_per_seq
            hp = jnp.where(t == 0, 0.0, modulate(xp_ref[...]))
            hn = jnp.where(t == tiles_per_seq - 1, 0.0, modulate(xn_ref[...]))
            h_scr[0:HALO, :] = hp.astype(BF16)
            h_scr[rows - HALO:rows, :] = hn.astype(BF16)
        acc_scr[...] = jnp.zeros_like(acc_scr)

    def conv(u, cw_ref, cb_ref):
        prev = pltpu.roll(u, 1, 0)
        nxt = pltpu.roll(u, rows - 1, 0)
        y = prev * cw_ref[0:1, :] + u * cw_ref[1:2, :] + nxt * cw_ref[2:3, :] + cb_ref[...]
        return y[HALO:rows - HALO, :]

    h = h_scr[...]
    val = conv(_dot(h, wv_ref[...]), cwv_ref, cbv_ref)
    gate = conv(_dot(h, wg_ref[...]), cwg_ref, cbg_ref)
    act = (gate * jax.nn.sigmoid(gate) * val).astype(BF16)
    acc_scr[...] += _dot(act, wd_ref[...])

    @pl.when(f == pl.num_programs(1) - 1)
    def _():
        g2 = mod_ref[0, 5:6, :]
        for s in range(n_seg):
            x = x_ref[s * seg_len:(s + 1) * seg_len, :]
            y = acc_scr[s * stride:s * stride + seg_len, :]
            o_ref[s * seg_len:(s + 1) * seg_len, :] = _norm(ALPHA * x + g2 * y) * lg_ref[...] + lb_ref[...]


def _ffn(x1, mod6, w_up, w_conv, b_conv, w_down, ln_g, ln_b, seq_len, per_batch_mod):
    n_tok = x1.shape[0]
    tf = 512
    nf = D_FF // tf
    tm = 1024 if seq_len >= 1024 else 512
    seg_len = min(seq_len, tm)
    tiles_per_seq = max(seq_len // tm, 1)
    has_halo = seq_len > tm
    h_rows = (tm // seg_len) * (seg_len + 2 * HALO)
    mod_idx = (lambda i, f: (i // tiles_per_seq, 0, 0)) if per_batch_mod else (lambda i, f: (0, 0, 0))
    hb = tm // HALO
    n_hblk = n_tok // HALO
    in_specs = [pl.BlockSpec((tm, D_MODEL), lambda i, f: (i, 0))]
    args = [x1]
    scratch = [pltpu.VMEM((h_rows, D_MODEL), BF16)]
    if has_halo:
        in_specs = [_resident((tm, D_MODEL), lambda i, f: (i, 0))]
        in_specs += [
            pl.BlockSpec((HALO, D_MODEL), lambda i, f: (jnp.maximum(i * hb - 1, 0), 0)),
            pl.BlockSpec((HALO, D_MODEL), lambda i, f: (jnp.minimum((i + 1) * hb, n_hblk - 1), 0)),
        ]
        args += [x1, x1]
    else:
        scratch.append(pltpu.VMEM((h_rows - 2 * HALO, D_MODEL), F32))
    in_specs += [
        pl.BlockSpec((1, 6, D_MODEL), mod_idx),
        pl.BlockSpec((D_MODEL, tf), lambda i, f: (0, f)),
        pl.BlockSpec((D_MODEL, tf), lambda i, f: (0, nf + f)),
        pl.BlockSpec((3, tf), lambda i, f: (0, f)),
        pl.BlockSpec((3, tf), lambda i, f: (0, nf + f)),
        pl.BlockSpec((1, tf), lambda i, f: (0, f)),
        pl.BlockSpec((1, tf), lambda i, f: (0, nf + f)),
        pl.BlockSpec((tf, D_MODEL), lambda i, f: (f, 0)),
        _resident((1, D_MODEL), lambda i, f: (0, 0)),
        _resident((1, D_MODEL), lambda i, f: (0, 0)),
    ]
    b_conv2 = b_conv.reshape(1, 2 * D_FF)
    args += [mod6, w_up, w_up, w_conv, w_conv, b_conv2, b_conv2, w_down,
             ln_g.reshape(1, D_MODEL), ln_b.reshape(1, D_MODEL)]
    return pl.pallas_call(
        functools.partial(_ffn_kernel, tm=tm, seg_len=seg_len, has_halo=has_halo,
                          tiles_per_seq=tiles_per_seq),
        grid=(n_tok // tm, nf),
        in_specs=in_specs,
        out_specs=pl.BlockSpec((tm, D_MODEL), lambda i, f: (i, 0)),
        out_shape=jax.ShapeDtypeStruct((n_tok, D_MODEL), F32),
        scratch_shapes=scratch,
        compiler_params=_params(("parallel", "arbitrary")),
        name="ffn",
    )(*args)


def _grid_pos_tables(n_tokens):
    quarter = D_MODEL // 4
    freq = 1.0 / (10000.0 ** (jnp.arange(quarter, dtype=F32) / quarter))
    er = jnp.arange(n_tokens // GRID_W, dtype=F32)[:, None] * freq
    ec = jnp.arange(GRID_W, dtype=F32)[:, None] * freq
    return (jnp.concatenate([jnp.sin(er), jnp.cos(er)], -1),
            jnp.concatenate([jnp.sin(ec), jnp.cos(ec)], -1))


def _split_w_in(w_in, b_gate):
    n_gate = 2 * N_DIR * N_HEADS
    w_main = jnp.concatenate([w_in[:, :4 * D_MLSTM], w_in[:, 4 * D_MLSTM + n_gate:]], axis=1).astype(BF16)
    gw = w_in[:, 4 * D_MLSTM:4 * D_MLSTM + n_gate].reshape(D_MODEL, N_DIR, 2, N_HEADS)
    gw = gw.transpose(0, 3, 1, 2).reshape(D_MODEL, n_gate)
    gw = jnp.pad(gw, ((0, 0), (0, GATE_LANES - n_gate))).astype(BF16)
    gb = b_gate.astype(F32).reshape(N_DIR, 2, N_HEADS).transpose(2, 0, 1).reshape(1, n_gate)
    gb = jnp.pad(gb, ((0, 0), (0, GATE_LANES - n_gate)))
    return w_main, gw, gb


def _layer(x2d, pos, mod6, weights, batch, seq_len, per_batch_mod, state, emit_state):
    (w_main, w_gate, b_gate, w_hnorm, w_br_m, w_br_f, w_out, ln1_g, ln1_b,
     w_up, w_conv, b_conv, w_down, ln2_g, ln2_b) = weights
    p_main, gates = _inproj(x2d, pos, mod6, w_main, w_gate, b_gate, seq_len, per_batch_mod)
    ml = _mlstm(p_main, gates, w_hnorm, batch, seq_len, state=state, emit_state=emit_state)
    fr = _fourier(p_main, batch, seq_len)
    x1 = _mix(ml[0], fr, p_main, x2d, pos, mod6, w_br_m, w_br_f, w_out, ln1_g, ln1_b,
              seq_len, per_batch_mod)
    x2 = _ffn(x1, mod6, w_up, w_conv, b_conv, w_down, ln2_g, ln2_b, seq_len, per_batch_mod)
    return x2, ml[1:]


def kernel(x_prompt, x_sample, c, state_C, state_n, state_m, c_ctx, w_ada, b_ada, w_in, b_gate,
           w_hnorm, w_br_m, w_br_f, w_out, ln1_g, ln1_b, w_up, w_conv, b_conv, w_down, ln2_g, ln2_b):
    assert w_ada.shape[0] == DEPTH
    B, S, _ = x_prompt.shape
    DB, DS, _ = x_sample.shape
    l = 0
    n_cond = 16
    cond = jnp.zeros((n_cond, D_MODEL), F32).at[0].set(c_ctx).at[1:1 + DB].set(c)
    mod6 = _modulation(cond, w_ada[l], b_ada[l]).reshape(n_cond, 6, D_MODEL)
    w_main, w_gate, b_gate_l = _split_w_in(w_in[l], b_gate[l])
    weights = (w_main, w_gate, b_gate_l, w_hnorm[l], w_br_m[l].astype(BF16), w_br_f[l].astype(BF16),
               w_out[l].astype(BF16), ln1_g[l], ln1_b[l], w_up[l].astype(BF16), w_conv[l], b_conv[l],
               w_down[l].astype(BF16), ln2_g[l], ln2_b[l])
    pos = _grid_pos_tables(DS)

    yp, states = _layer(x_prompt.reshape(B * S, D_MODEL), None, mod6[0:1], weights, B, S,
                        per_batch_mod=False, state=None, emit_state=True)
    ys, _ = _layer(x_sample.reshape(DB * DS, D_MODEL), pos, mod6[1:1 + DB], weights, DB, DS,
                   per_batch_mod=True, state=(state_C[:, l], state_n[:, l], state_m[:, l]),
                   emit_state=False)
    new_C, new_n, new_m = states
    new_n = new_n.reshape(B, DEPTH, N_DIR, N_HEADS, HEAD_DIM)
    new_m = new_m[:, :, :, 0, 0].reshape(B, DEPTH, N_DIR, N_HEADS)
    return (yp.reshape(B, S, D_MODEL), ys.reshape(DB, DS, D_MODEL), new_C, new_n, new_m)
```

```python
import functools

import jax
import jax.numpy as jnp
from jax import lax
from jax.experimental import pallas as pl
from jax.experimental.pallas import tpu as pltpu

D_MODEL = 2048
N_HEADS = 4
HEAD_DIM = 256
D_MLSTM = N_HEADS * HEAD_DIM
N_FGROUPS = 4
FGROUP_DIM = 256
D_FOURIER = N_FGROUPS * FGROUP_DIM
D_FF = 5632
GRID_W = 64
N_DIR = 2
DEPTH = 1
ALPHA = (2.0 * DEPTH) ** 0.25
LN_EPS = 1e-5

F32 = jnp.float32
BF16 = jnp.bfloat16

CHUNK = 256
COL_Q = 0
COL_K = COL_Q + D_MLSTM
COL_V = COL_K + D_MLSTM
COL_O = COL_V + D_MLSTM
COL_FR = COL_O + D_MLSTM
COL_GA = COL_FR + D_FOURIER
COL_GB = COL_GA + D_MODEL
D_MAIN = COL_GB + D_MODEL
GATE_BLOCK = 1024
GATE_LANES = 128
HALO = 8

VMEM_LIMIT = 56 * 1024 * 1024

INPROJ_TM = 1024
INPROJ_TN = 2304
MIX_TM = 256
FFN_TM_LONG = 1024
FFN_TM_SHORT = 512
FFN_TF = 512


def _params(sem, flags=None):
    return pltpu.CompilerParams(dimension_semantics=sem, vmem_limit_bytes=VMEM_LIMIT, flags=flags)


def _resident(shape, index_map):
    return pl.BlockSpec(shape, index_map, pipeline_mode=pl.Buffered(1))


def _norm(x):
    mu = jnp.mean(x, axis=-1, keepdims=True)
    xc = x - mu
    var = jnp.mean(xc * xc, axis=-1, keepdims=True)
    return xc * lax.rsqrt(var + LN_EPS)


def _dot(a, b):
    return jnp.dot(a, b, preferred_element_type=F32)


def _dot_nt(a, b):
    return lax.dot_general(a, b, (((1,), (1,)), ((), ())), preferred_element_type=F32)


def _mod_kernel(c_ref, w_ref, b_ref, o_ref):
    c = c_ref[...]
    s = c * jax.nn.sigmoid(c)
    o_ref[...] = _dot(s.astype(BF16), w_ref[...].astype(BF16)) + b_ref[...]


def _modulation(cond, w_ada, b_ada):
    rows, tn = cond.shape[0], 1024
    n_out = w_ada.shape[1]
    return pl.pallas_call(
        _mod_kernel,
        grid=(n_out // tn,),
        in_specs=[
            _resident((rows, D_MODEL), lambda j: (0, 0)),
            pl.BlockSpec((D_MODEL, tn), lambda j: (0, j)),
            pl.BlockSpec((1, tn), lambda j: (0, j)),
        ],
        out_specs=pl.BlockSpec((rows, tn), lambda j: (0, j)),
        out_shape=jax.ShapeDtypeStruct((rows, n_out), F32),
        compiler_params=_params(("arbitrary",)),
        name="modulation",
    )(cond, w_ada, b_ada.reshape(1, n_out))


LN_ROWS = 256


def _add_grid_pos(x, pr_ref, pc_ref, tok0):
    half = D_MODEL // 2
    out = []
    for k in range(x.shape[0] // GRID_W):
        blk = x[k * GRID_W:(k + 1) * GRID_W, :]
        pr = pr_ref[pl.ds(tok0 // GRID_W + k, 1), :]
        out.append(jnp.concatenate([blk[:, :half] + pr, blk[:, half:] + pc_ref[...]], axis=1))
    return jnp.concatenate(out, axis=0)


def _inproj_kernel(*refs, tm, tiles_per_seq, has_pos):
    if has_pos:
        x_ref, pr_ref, pc_ref, mod_ref, w_ref, wg_ref, bg_ref, p_ref, g_ref, h_scr = refs
    else:
        x_ref, mod_ref, w_ref, wg_ref, bg_ref, p_ref, g_ref, h_scr = refs

    @pl.when(pl.program_id(1) == 0)
    def _():
        sh = mod_ref[0, 0:1, :]
        sc = mod_ref[0, 1:2, :]

        def ln_rows(r, carry):
            rows = pl.ds(pl.multiple_of(r * LN_ROWS, LN_ROWS), LN_ROWS)
            x = x_ref[rows, :]
            if has_pos:
                x = _add_grid_pos(x, pr_ref, pc_ref,
                                  (pl.program_id(0) % tiles_per_seq) * tm + r * LN_ROWS)
            h = (_norm(x) * (1.0 + sc) + sh).astype(BF16)
            h_scr[rows, :] = h
            g_ref[rows, :] = _dot(h, wg_ref[...]) + bg_ref[...]
            return carry

        lax.fori_loop(0, tm // LN_ROWS, ln_rows, 0)

    p_ref[...] = _dot(h_scr[...], w_ref[...]).astype(BF16)


def _inproj(x2d, pos, mod6, w_main, w_gate, b_gate, seq_len, per_batch_mod):
    n_tok = x2d.shape[0]
    tm, tn = INPROJ_TM, INPROJ_TN
    tiles_per_seq = max(seq_len // tm, 1)
    has_pos = pos is not None
    mod_idx = (lambda i, j: (i // tiles_per_seq, 0, 0)) if per_batch_mod else (lambda i, j: (0, 0, 0))
    in_specs = [pl.BlockSpec((tm, D_MODEL), lambda i, j: (i, 0))]
    args = [x2d]
    if has_pos:
        in_specs += [_resident(p.shape, lambda i, j: (0, 0)) for p in pos]
        args += list(pos)
    in_specs += [
        pl.BlockSpec((1, 6, D_MODEL), mod_idx),
        pl.BlockSpec((D_MODEL, tn), lambda i, j: (0, j)),
        _resident((D_MODEL, GATE_LANES), lambda i, j: (0, 0)),
        _resident((1, GATE_LANES), lambda i, j: (0, 0)),
    ]
    args += [mod6, w_main, w_gate, b_gate]
    return pl.pallas_call(
        functools.partial(_inproj_kernel, tm=tm, tiles_per_seq=tiles_per_seq, has_pos=has_pos),
        grid=(n_tok // tm, D_MAIN // tn),
        in_specs=in_specs,
        out_specs=[
            pl.BlockSpec((tm, tn), lambda i, j: (i, j)),
            pl.BlockSpec((tm, GATE_LANES), lambda i, j: (i, 0)),
        ],
        out_shape=[
            jax.ShapeDtypeStruct((n_tok, D_MAIN), BF16),
            jax.ShapeDtypeStruct((n_tok, GATE_LANES), F32),
        ],
        scratch_shapes=[pltpu.VMEM((tm, D_MODEL), BF16)],
        compiler_params=_params(("parallel", "arbitrary")),
        name="inproj",
    )(*args)


def _log_sigmoid(x):
    return jnp.minimum(x, 0.0) - jnp.log1p(jnp.exp(-jnp.abs(x)))


def _head_out(h, o, wn):
    return (jax.nn.sigmoid(o.astype(F32)) * (_norm(h) * wn)).astype(BF16)


EXT = HEAD_DIM + GATE_LANES
PASS1_GROUP = 4


def _lane_tile(x, n):
    return jnp.concatenate([x] * n, axis=1)


def _mlstm_kernel(*refs, seq_len, has_state, emit_state):
    refs = list(refs)
    q_ref, k_ref, v_ref, o_ref, g_ref, wn_ref = refs[:6]
    refs = refs[6:]
    if has_state:
        m0_ref, c0_ref, n0_ref = refs[:3]
        refs = refs[3:]
    hm_ref = refs[0]
    refs = refs[1:]
    if emit_state:
        cout_ref, nout_ref, mout_ref = refs[:3]
        refs = refs[3:]
    num_scr, row_scr, kv_scr, sc_scr, c_scr, h_scr = refs
    L = CHUNK
    nc = seq_len // L
    scale = HEAD_DIM ** -0.5
    head = pl.program_id(1)
    gate_shift = jnp.where(head == 0, 0, GATE_LANES - 2 * N_DIR * head)

    def chunk_rows(c):
        return pl.ds(pl.multiple_of(c * L, L), L)

    def load_q(rows):
        return (q_ref[rows, :].astype(F32) * scale).astype(BF16)

    t_idx = lax.broadcasted_iota(jnp.int32, (L, L), 0)
    s_idx = lax.broadcasted_iota(jnp.int32, (L, L), 1)
    hi_rows = lax.broadcasted_iota(jnp.int32, (16, L), 0) < 8

    masks = [s_idx <= t_idx, s_idx >= t_idx]
    masks_b = [jnp.where(mk, 1.0, 0.0).astype(BF16) for mk in masks]
    group = min(nc, PASS1_GROUP)

    def pass1(grp, carry):
        chunks = [grp * group + i for i in range(group)]
        base, items = [], []
        for c in chunks:
            rows = chunk_rows(c)
            qb, kb, vb = load_q(rows), k_ref[rows, :], v_ref[rows, :]
            v_ext = jnp.concatenate([vb, jnp.ones((L, GATE_LANES), BF16)], axis=1)
            s0 = _dot_nt(qb, kb)
            g_row = jnp.transpose(pltpu.roll(g_ref[rows, :], gate_shift, 1))
            h_scr[rows, :] = jnp.zeros((L, HEAD_DIM), F32)
            base.append((c, rows, kb, v_ext, s0))
            for d in range(N_DIR):
                ig_row = g_row[2 * d:2 * d + 1, :]
                lf = _log_sigmoid(g_row[2 * d + 1:2 * d + 2, :])
                lf_hi = lf.astype(BF16)
                lf_lo = (lf - lf_hi.astype(F32)).astype(BF16)
                lhs = jnp.where(hi_rows, lf_hi.astype(F32), lf_lo.astype(F32)).astype(BF16)
                r16 = _dot_nt(lhs, masks_b[d])
                lf_rep = jnp.concatenate([jnp.broadcast_to(lf_hi, (GATE_LANES, L)),
                                          jnp.broadcast_to(lf_lo, (GATE_LANES, L))], axis=0)
                bb = _dot_nt(masks_b[d], lf_rep)
                items.append((len(base) - 1, d, ig_row, lf, r16, bb))
        states = []
        for bi, d, ig_row, lf, r16, bb in items:
            c, rows, kb, v_ext, s0 = base[bi]
            b_row = r16[0:1, :] + r16[8:9, :]
            b_rep = bb[:, :GATE_LANES] + bb[:, GATE_LANES:]
            c_row = ig_row - b_row
            c_max = jnp.broadcast_to(
                jnp.max(jnp.where(masks[d], c_row, -jnp.inf), axis=1, keepdims=True), (L, L))
            w = jnp.exp(jnp.where(masks[d], c_row - c_max, -jnp.inf))
            pv = _dot((s0 * w).astype(BF16), v_ext)
            num_scr[d, rows, :] = pv[:, :HEAD_DIM]
            row_scr[d, 0, rows, :] = pv[:, HEAD_DIM:]
            row_scr[d, 1, rows, :] = b_rep + c_max[:, :GATE_LANES]
            row_scr[d, 2, rows, :] = b_rep
            states.append((bi, d, ig_row, lf, b_row))
        k_ts = [jnp.transpose(kb.astype(F32)) for (_, _, kb, _, _) in base]
        for bi, d, ig_row, lf, b_row in states:
            c, rows, kb, v_ext, s0 = base[bi]
            b_last = jnp.sum(lf, axis=1, keepdims=True)
            g = b_last - b_row + ig_row
            g_max = jnp.max(g, axis=1, keepdims=True)
            wk = jnp.exp(g - g_max)
            kv_scr[d, c] = _dot((k_ts[bi] * wk).astype(BF16), v_ext)
            sc_scr[d, c, 0] = jnp.broadcast_to(b_last, (8, GATE_LANES))
            sc_scr[d, c, 1] = jnp.broadcast_to(g_max, (8, GATE_LANES))
        return carry

    lax.fori_loop(0, nc // group, pass1, 0)

    ms = []
    for d in range(N_DIR):
        if has_state:
            bh = pl.program_id(0) * (N_DIR * N_HEADS) + d * N_HEADS + head
            n_rep = jnp.transpose(jnp.broadcast_to(n0_ref[0, d, 0], (GATE_LANES, HEAD_DIM)))
            c_scr[d] = jnp.concatenate([c0_ref[0, d, 0], n_rep], axis=1)
            ms.append(jnp.full((1, GATE_LANES), m0_ref[bh], F32))
        else:
            c_scr[d] = jnp.zeros((HEAD_DIM, EXT), F32)
            ms.append(jnp.zeros((1, GATE_LANES), F32))

    def pass2(j, carry):
        new = []
        for d, m in enumerate(carry):
            c = j if d == 0 else nc - 1 - j
            rows = chunk_rows(c)
            c_ext = c_scr[d]
            qcn = _dot(load_q(rows), c_ext.astype(BF16))
            den_i, a_rep, b_rep = row_scr[d, 0, rows, :], row_scr[d, 1, rows, :], row_scr[d, 2, rows, :]
            m_rep = jnp.maximum(b_rep + m, a_rep)
            r_intra = jnp.exp(a_rep - m_rep)
            r_state = jnp.exp(b_rep + m - m_rep)
            den = r_intra * den_i + r_state * qcn[:, HEAD_DIM:]
            inv = 1.0 / jnp.maximum(jnp.abs(den), jnp.exp(-m_rep))
            h = (_lane_tile(r_intra * inv, 2) * num_scr[d, rows, :]
                 + _lane_tile(r_state * inv, 2) * qcn[:, :HEAD_DIM])
            h_scr[rows, :] += h
            b_last, g_max = sc_scr[d, c, 0][0:1, :], sc_scr[d, c, 1][0:1, :]
            m_new = jnp.maximum(b_last + m, g_max)
            decay = jnp.exp(b_last + m - m_new)
            gain = jnp.exp(g_max - m_new)
            c_scr[d] = _lane_tile(decay, 3) * c_ext + _lane_tile(gain, 3) * kv_scr[d, c]
            new.append(m_new)
        return tuple(new)

    ms = lax.fori_loop(0, nc, pass2, tuple(ms))

    wn = wn_ref[0]

    def finish(c, carry):
        rows = chunk_rows(c)
        hm_ref[rows, :] = _head_out(h_scr[rows, :], o_ref[rows, :], wn)
        return carry

    lax.fori_loop(0, nc, finish, 0)
    if emit_state:
        for d in range(N_DIR):
            c_ext = c_scr[d]
            cout_ref[0, 0, d, 0] = c_ext[:, :HEAD_DIM]
            nout_ref[0, 0, d, 0] = jnp.transpose(c_ext[:, HEAD_DIM:])[0:1, :]
            mout_ref[0, d, 0] = ms[d]


def _mlstm(p_main, gates, w_hnorm, batch, seq_len, state=None, emit_state=False):
    n_tok = p_main.shape[0]
    T = seq_len
    blk = lambda col: pl.BlockSpec((T, HEAD_DIM), lambda b, h: (b, col // HEAD_DIM + h))
    in_specs = [blk(COL_Q), blk(COL_K), blk(COL_V), blk(COL_O),
                pl.BlockSpec((T, GATE_LANES), lambda b, h: (b, 0)),
                pl.BlockSpec((1, 1, HEAD_DIM), lambda b, h: (h, 0, 0))]
    args = [p_main, p_main, p_main, p_main, gates, w_hnorm.reshape(N_HEADS, 1, HEAD_DIM)]
    has_state = state is not None
    if has_state:
        C0, n0, m0 = state
        in_specs += [
            pl.BlockSpec(memory_space=pltpu.SMEM),
            pl.BlockSpec((1, N_DIR, 1, HEAD_DIM, HEAD_DIM), lambda b, h: (b, 0, h, 0, 0)),
            pl.BlockSpec((1, N_DIR, 1, 1, HEAD_DIM), lambda b, h: (b, 0, h, 0, 0)),
        ]
        args += [m0.reshape(-1), C0, n0.reshape(batch, N_DIR, N_HEADS, 1, HEAD_DIM)]
    out_specs = [pl.BlockSpec((T, HEAD_DIM), lambda b, h: (b, h))]
    out_shape = [jax.ShapeDtypeStruct((n_tok, D_MLSTM), BF16)]
    if emit_state:
        out_specs += [
            pl.BlockSpec((1, 1, N_DIR, 1, HEAD_DIM, HEAD_DIM), lambda b, h: (b, 0, 0, h, 0, 0)),
            pl.BlockSpec((1, 1, N_DIR, 1, 1, HEAD_DIM), lambda b, h: (b, 0, 0, h, 0, 0)),
            pl.BlockSpec((1, N_DIR, 1, 1, GATE_LANES), lambda b, h: (b, 0, h, 0, 0)),
        ]
        out_shape += [
            jax.ShapeDtypeStruct((batch, DEPTH, N_DIR, N_HEADS, HEAD_DIM, HEAD_DIM), F32),
            jax.ShapeDtypeStruct((batch, DEPTH, N_DIR, N_HEADS, 1, HEAD_DIM), F32),
            jax.ShapeDtypeStruct((batch, N_DIR, N_HEADS, 1, GATE_LANES), F32),
        ]
    nc = T // CHUNK
    scratch = [pltpu.VMEM((N_DIR, T, HEAD_DIM), F32),
               pltpu.VMEM((N_DIR, 3, T, GATE_LANES), F32),
               pltpu.VMEM((N_DIR, nc, HEAD_DIM, EXT), F32),
               pltpu.VMEM((N_DIR, nc, 2, 8, GATE_LANES), F32),
               pltpu.VMEM((N_DIR, HEAD_DIM, EXT), F32),
               pltpu.VMEM((T, HEAD_DIM), F32)]
    return pl.pallas_call(
        functools.partial(_mlstm_kernel, seq_len=T, has_state=has_state, emit_state=emit_state),
        grid=(batch, N_HEADS),
        in_specs=in_specs,
        out_specs=out_specs,
        out_shape=out_shape,
        scratch_shapes=scratch,
        compiler_params=_params(("parallel", "parallel")),
        name="mlstm",
    )(*args)


def _fourier_kernel(x_ref, cs_ref, ct_ref, st_ref, o_ref, *, scale):
    for g in range(N_FGROUPS):
        cols = slice(g * FGROUP_DIM, (g + 1) * FGROUP_DIM)
        z = _dot(x_ref[:, cols], cs_ref[...])
        zc = z[:, :FGROUP_DIM].astype(BF16)
        zs = z[:, FGROUP_DIM:].astype(BF16)
        y = _dot(ct_ref[...], zc) - _dot(st_ref[...], zs)
        o_ref[:, cols] = (y * scale).astype(BF16)


def _dft_tables(n):
    k = jnp.arange(n, dtype=jnp.int32)
    ang = ((k[:, None] * k[None, :]) % n).astype(F32) * (2.0 * jnp.pi / n)
    return jnp.cos(ang), jnp.sin(ang)


DFT_BLOCK = 256
DFT_RADIX = 8
HALF_LANES = 128


def _cadd(a, b):
    return a[0] + b[0], a[1] + b[1]


def _csub(a, b):
    return a[0] - b[0], a[1] - b[1]


def _mul_neg_i(a):
    return a[1], -a[0]


def _dft4(y):
    c0, c1 = _cadd(y[0], y[2]), _cadd(y[1], y[3])
    d0, d1 = _csub(y[0], y[2]), _mul_neg_i(_csub(y[1], y[3]))
    return [_cadd(c0, c1), _cadd(d0, d1), _csub(c0, c1), _csub(d0, d1)]


def _dft8(x):
    r = 0.5 ** 0.5
    a = [_cadd(x[n], x[n + 4]) for n in range(4)]
    b = [_csub(x[n], x[n + 4]) for n in range(4)]
    b[1] = ((b[1][0] + b[1][1]) * r, (b[1][1] - b[1][0]) * r)
    b[2] = _mul_neg_i(b[2])
    b[3] = ((b[3][1] - b[3][0]) * r, -(b[3][0] + b[3][1]) * r)
    even, odd = _dft4(a), _dft4(b)
    return [even[k // 2] if k % 2 == 0 else odd[k // 2] for k in range(8)]


def _fourier_long_kernel(x_ref, cs_ref, twc_ref, tws_ref, o_ref, w_scr, *, scale):
    nb = DFT_BLOCK
    cs = cs_ref[...]
    c_tab, s_tab = cs[:, :nb], cs[:, nb:]
    z = []
    for t1 in range(DFT_RADIX):
        zz = _dot(x_ref[t1 * nb:(t1 + 1) * nb, :], cs)
        z.append((zz[:, :nb], -zz[:, nb:]))
    a = _dft8(z)
    for u1 in range(DFT_RADIX):
        ar, ai = a[u1]
        if u1 > 0:
            twc = twc_ref[u1 * nb:(u1 + 1) * nb, :]
            tws = tws_ref[u1 * nb:(u1 + 1) * nb, :]
            ar, ai = ar * twc + ai * tws, ai * twc - ar * tws
        y = (_dot(c_tab, ar.astype(BF16)) + _dot(s_tab, ai.astype(BF16))) * scale
        for half in range(FGROUP_DIM // HALF_LANES):
            w_scr[half, pl.ds(u1, nb, stride=DFT_RADIX), :] = y[:, half * HALF_LANES:(half + 1) * HALF_LANES]
    for half in range(FGROUP_DIM // HALF_LANES):
        o_ref[:, half * HALF_LANES:(half + 1) * HALF_LANES] = w_scr[half].astype(BF16)


def _fourier_long(p_main, batch, seq_len):
    n_tok = p_main.shape[0]
    T = seq_len
    assert T == DFT_RADIX * DFT_BLOCK and FGROUP_DIM == DFT_BLOCK
    cc, sc = _dft_tables(FGROUP_DIM)
    cs = jnp.concatenate([cc, sc], axis=1).astype(BF16)
    u1 = jnp.repeat(jnp.arange(DFT_RADIX, dtype=jnp.int32), DFT_BLOCK)
    t2 = jnp.tile(jnp.arange(DFT_BLOCK, dtype=jnp.int32), DFT_RADIX)
    ang = ((u1 * t2) % T).astype(F32) * (2.0 * jnp.pi / T)
    twc = jnp.broadcast_to(jnp.cos(ang)[:, None], (T, FGROUP_DIM))
    tws = jnp.broadcast_to(jnp.sin(ang)[:, None], (T, FGROUP_DIM))
    return pl.pallas_call(
        functools.partial(_fourier_long_kernel, scale=float((T * FGROUP_DIM) ** -0.5)),
        grid=(batch, N_FGROUPS),
        in_specs=[
            pl.BlockSpec((T, FGROUP_DIM), lambda b, g: (b, COL_FR // FGROUP_DIM + g)),
            _resident((FGROUP_DIM, 2 * FGROUP_DIM), lambda b, g: (0, 0)),
            _resident((T, FGROUP_DIM), lambda b, g: (0, 0)),
            _resident((T, FGROUP_DIM), lambda b, g: (0, 0)),
        ],
        out_specs=pl.BlockSpec((T, FGROUP_DIM), lambda b, g: (b, g)),
        out_shape=jax.ShapeDtypeStruct((n_tok, D_FOURIER), BF16),
        scratch_shapes=[pltpu.VMEM((FGROUP_DIM // HALF_LANES, T, HALF_LANES), F32)],
        compiler_params=_params(("parallel", "parallel")),
        name="fourier_long",
    )(p_main, cs, twc, tws)


def _fourier(p_main, batch, seq_len):
    if seq_len == DFT_RADIX * DFT_BLOCK:
        return _fourier_long(p_main, batch, seq_len)
    n_tok = p_main.shape[0]
    T = seq_len
    cc, sc = _dft_tables(FGROUP_DIM)
    cs = jnp.concatenate([cc, sc], axis=1).astype(BF16)
    ct, st = _dft_tables(T)
    return pl.pallas_call(
        functools.partial(_fourier_kernel, scale=float((T * FGROUP_DIM) ** -0.5)),
        grid=(batch,),
        in_specs=[
            pl.BlockSpec((T, D_FOURIER), lambda b: (b, COL_FR // D_FOURIER)),
            _resident((FGROUP_DIM, 2 * FGROUP_DIM), lambda b: (0, 0)),
            _resident((T, T), lambda b: (0, 0)),
            _resident((T, T), lambda b: (0, 0)),
        ],
        out_specs=pl.BlockSpec((T, D_FOURIER), lambda b: (b, 0)),
        out_shape=jax.ShapeDtypeStruct((n_tok, D_FOURIER), BF16),
        compiler_params=_params(("parallel",)),
        name="fourier",
    )(p_main, cs, ct.astype(BF16), st.astype(BF16))


def _mix_kernel(*refs, tm, tiles_per_seq, has_pos):
    if has_pos:
        (hm_ref, fr_ref, ga0_ref, ga1_ref, gb0_ref, gb1_ref, x_ref, pr_ref, pc_ref, mod_ref,
         wm_ref, wf_ref, wo_ref, lg_ref, lb_ref, o_ref) = refs
    else:
        (hm_ref, fr_ref, ga0_ref, ga1_ref, gb0_ref, gb1_ref, x_ref, mod_ref,
         wm_ref, wf_ref, wo_ref, lg_ref, lb_ref, o_ref) = refs
    a = _dot(hm_ref[...], wm_ref[...])
    b = _dot(fr_ref[...], wf_ref[...])
    gate = lambda g0, g1: jax.nn.sigmoid(jnp.concatenate([g0[...], g1[...]], axis=1).astype(F32))
    mixed = gate(ga0_ref, ga1_ref) * a + gate(gb0_ref, gb1_ref) * b
    z = _dot(mixed.astype(BF16), wo_ref[...])
    x = x_ref[...]
    if has_pos:
        x = _add_grid_pos(x, pr_ref, pc_ref, (pl.program_id(0) % tiles_per_seq) * tm)
    g1 = mod_ref[0, 2:3, :]
    o_ref[...] = _norm(ALPHA * x + g1 * z) * lg_ref[...] + lb_ref[...]


def _mix(hm, fr, p_main, x2d, pos, mod6, w_br_m, w_br_f, w_out, ln_g, ln_b, seq_len, per_batch_mod):
    n_tok = x2d.shape[0]
    tm = MIX_TM
    tiles_per_seq = seq_len // tm
    has_pos = pos is not None
    mod_idx = (lambda i: (i // tiles_per_seq, 0, 0)) if per_batch_mod else (lambda i: (0, 0, 0))
    in_specs = [
        pl.BlockSpec((tm, D_MLSTM), lambda i: (i, 0)),
        pl.BlockSpec((tm, D_FOURIER), lambda i: (i, 0)),
        pl.BlockSpec((tm, GATE_BLOCK), lambda i: (i, COL_GA // GATE_BLOCK)),
        pl.BlockSpec((tm, GATE_BLOCK), lambda i: (i, COL_GA // GATE_BLOCK + 1)),
        pl.BlockSpec((tm, GATE_BLOCK), lambda i: (i, COL_GB // GATE_BLOCK)),
        pl.BlockSpec((tm, GATE_BLOCK), lambda i: (i, COL_GB // GATE_BLOCK + 1)),
        pl.BlockSpec((tm, D_MODEL), lambda i: (i, 0)),
    ]
    args = [hm, fr, p_main, p_main, p_main, p_main, x2d]
    if has_pos:
        in_specs += [_resident(p.shape, lambda i: (0, 0)) for p in pos]
        args += list(pos)
    in_specs += [
        pl.BlockSpec((1, 6, D_MODEL), mod_idx),
        _resident((D_MLSTM, D_MODEL), lambda i: (0, 0)),
        _resident((D_FOURIER, D_MODEL), lambda i: (0, 0)),
        _resident((D_MODEL, D_MODEL), lambda i: (0, 0)),
        _resident((1, D_MODEL), lambda i: (0, 0)),
        _resident((1, D_MODEL), lambda i: (0, 0)),
    ]
    args += [mod6, w_br_m, w_br_f, w_out, ln_g.reshape(1, D_MODEL), ln_b.reshape(1, D_MODEL)]
    return pl.pallas_call(
        functools.partial(_mix_kernel, tm=tm, tiles_per_seq=tiles_per_seq, has_pos=has_pos),
        grid=(n_tok // tm,),
        in_specs=in_specs,
        out_specs=pl.BlockSpec((tm, D_MODEL), lambda i: (i, 0)),
        out_shape=jax.ShapeDtypeStruct((n_tok, D_MODEL), F32),
        compiler_params=_params(("parallel",)),
        name="mix",
    )(*args)


def _ffn_kernel(*refs, tm, seg_len, has_halo, tiles_per_seq):
    if has_halo:
        (x_ref, xp_ref, xn_ref, mod_ref, wv_ref, wg_ref, cwv_ref, cwg_ref, cbv_ref, cbg_ref,
         wd_ref, lg_ref, lb_ref, o_ref, h_scr) = refs
        acc_scr = o_ref
    else:
        (x_ref, mod_ref, wv_ref, wg_ref, cwv_ref, cwg_ref, cbv_ref, cbg_ref,
         wd_ref, lg_ref, lb_ref, o_ref, h_scr, acc_scr) = refs
    f = pl.program_id(1)
    n_seg = tm // seg_len
    stride = seg_len + 2 * HALO
    rows = n_seg * stride
    zeros = jnp.zeros((HALO, D_MODEL), BF16)

    @pl.when(f == 0)
    def _():
        sh = mod_ref[0, 3:4, :]
        sc = mod_ref[0, 4:5, :]
        modulate = lambda x: _norm(x) * (1.0 + sc) + sh
        for s in range(n_seg):
            h_scr[s * stride + HALO:(s + 1) * stride - HALO, :] = (
                modulate(x_ref[s * seg_len:(s + 1) * seg_len, :]).astype(BF16))
            h_scr[s * stride:s * stride + HALO, :] = zeros
            h_scr[(s + 1) * stride - HALO:(s + 1) * stride, :] = zeros
        if has_halo:
            t = pl.program_id(0) % tiles_per_seq
            hp = jnp.where(t == 0, 0.0, modulate(xp_ref[...]))
            hn = jnp.where(t == tiles_per_seq - 1, 0.0, modulate(xn_ref[...]))
            h_scr[0:HALO, :] = hp.astype(BF16)
            h_scr[rows - HALO:rows, :] = hn.astype(BF16)
        acc_scr[...] = jnp.zeros_like(acc_scr)

    def conv(u, cw_ref, cb_ref):
        prev = pltpu.roll(u, 1, 0)
        nxt = pltpu.roll(u, rows - 1, 0)
        y = prev * cw_ref[0:1, :] + u * cw_ref[1:2, :] + nxt * cw_ref[2:3, :] + cb_ref[...]
        return y[HALO:rows - HALO, :]

    h = h_scr[...]
    val = conv(_dot(h, wv_ref[...]), cwv_ref, cbv_ref)
    gate = conv(_dot(h, wg_ref[...]), cwg_ref, cbg_ref)
    act = (gate * jax.nn.sigmoid(gate) * val).astype(BF16)
    acc_scr[...] += _dot(act, wd_ref[...])

    @pl.when(f == pl.num_programs(1) - 1)
    def _():
        g2 = mod_ref[0, 5:6, :]
        for s in range(n_seg):
            x = x_ref[s * seg_len:(s + 1) * seg_len, :]
            y = acc_scr[s * stride:s * stride + seg_len, :]
            o_ref[s * seg_len:(s + 1) * seg_len, :] = _norm(ALPHA * x + g2 * y) * lg_ref[...] + lb_ref[...]


def _ffn(x1, mod6, w_up, w_conv, b_conv, w_down, ln_g, ln_b, seq_len, per_batch_mod):
    n_tok = x1.shape[0]
    tf = FFN_TF
    nf = D_FF // tf
    tm = FFN_TM_LONG if seq_len >= FFN_TM_LONG else FFN_TM_SHORT
    seg_len = min(seq_len, tm)
    tiles_per_seq = max(seq_len // tm, 1)
    has_halo = seq_len > tm
    h_rows = (tm // seg_len) * (seg_len + 2 * HALO)
    mod_idx = (lambda i, f: (i // tiles_per_seq, 0, 0)) if per_batch_mod else (lambda i, f: (0, 0, 0))
    hb = tm // HALO
    n_hblk = n_tok // HALO
    in_specs = [pl.BlockSpec((tm, D_MODEL), lambda i, f: (i, 0))]
    args = [x1]
    scratch = [pltpu.VMEM((h_rows, D_MODEL), BF16)]
    if has_halo:
        in_specs = [_resident((tm, D_MODEL), lambda i, f: (i, 0))]
        in_specs += [
            pl.BlockSpec((HALO, D_MODEL), lambda i, f: (jnp.maximum(i * hb - 1, 0), 0)),
            pl.BlockSpec((HALO, D_MODEL), lambda i, f: (jnp.minimum((i + 1) * hb, n_hblk - 1), 0)),
        ]
        args += [x1, x1]
    else:
        scratch.append(pltpu.VMEM((h_rows - 2 * HALO, D_MODEL), F32))
    in_specs += [
        pl.BlockSpec((1, 6, D_MODEL), mod_idx),
        pl.BlockSpec((D_MODEL, tf), lambda i, f: (0, f)),
        pl.BlockSpec((D_MODEL, tf), lambda i, f: (0, nf + f)),
        pl.BlockSpec((3, tf), lambda i, f: (0, f)),
        pl.BlockSpec((3, tf), lambda i, f: (0, nf + f)),
        pl.BlockSpec((1, tf), lambda i, f: (0, f)),
        pl.BlockSpec((1, tf), lambda i, f: (0, nf + f)),
        pl.BlockSpec((tf, D_MODEL), lambda i, f: (f, 0)),
        _resident((1, D_MODEL), lambda i, f: (0, 0)),
        _resident((1, D_MODEL), lambda i, f: (0, 0)),
    ]
    b_conv2 = b_conv.reshape(1, 2 * D_FF)
    args += [mod6, w_up, w_up, w_conv, w_conv, b_conv2, b_conv2, w_down,
             ln_g.reshape(1, D_MODEL), ln_b.reshape(1, D_MODEL)]
    return pl.pallas_call(
        functools.partial(_ffn_kernel, tm=tm, seg_len=seg_len, has_halo=has_halo,
                          tiles_per_seq=tiles_per_seq),
        grid=(n_tok // tm, nf),
        in_specs=in_specs,
        out_specs=pl.BlockSpec((tm, D_MODEL), lambda i, f: (i, 0)),
        out_shape=jax.ShapeDtypeStruct((n_tok, D_MODEL), F32),
        scratch_shapes=scratch,
        compiler_params=_params(("parallel", "arbitrary")),
        name="ffn",
    )(*args)


def _grid_pos_tables(n_tokens):
    quarter = D_MODEL // 4
    freq = 1.0 / (10000.0 ** (jnp.arange(quarter, dtype=F32) / quarter))
    er = jnp.arange(n_tokens // GRID_W, dtype=F32)[:, None] * freq
    ec = jnp.arange(GRID_W, dtype=F32)[:, None] * freq
    return (jnp.concatenate([jnp.sin(er), jnp.cos(er)], -1),
            jnp.concatenate([jnp.sin(ec), jnp.cos(ec)], -1))


def _split_w_in(w_in, b_gate):
    n_gate = 2 * N_DIR * N_HEADS
    w_main = jnp.concatenate([w_in[:, :4 * D_MLSTM], w_in[:, 4 * D_MLSTM + n_gate:]], axis=1).astype(BF16)
    gw = w_in[:, 4 * D_MLSTM:4 * D_MLSTM + n_gate].reshape(D_MODEL, N_DIR, 2, N_HEADS)
    gw = gw.transpose(0, 3, 1, 2).reshape(D_MODEL, n_gate)
    gw = jnp.pad(gw, ((0, 0), (0, GATE_LANES - n_gate))).astype(BF16)
    gb = b_gate.astype(F32).reshape(N_DIR, 2, N_HEADS).transpose(2, 0, 1).reshape(1, n_gate)
    gb = jnp.pad(gb, ((0, 0), (0, GATE_LANES - n_gate)))
    return w_main, gw, gb


def _layer(x2d, pos, mod6, weights, batch, seq_len, per_batch_mod, state, emit_state):
    (w_main, w_gate, b_gate, w_hnorm, w_br_m, w_br_f, w_out, ln1_g, ln1_b,
     w_up, w_conv, b_conv, w_down, ln2_g, ln2_b) = weights
    p_main, gates = _inproj(x2d, pos, mod6, w_main, w_gate, b_gate, seq_len, per_batch_mod)
    ml = _mlstm(p_main, gates, w_hnorm, batch, seq_len, state=state, emit_state=emit_state)
    fr = _fourier(p_main, batch, seq_len)
    x1 = _mix(ml[0], fr, p_main, x2d, pos, mod6, w_br_m, w_br_f, w_out, ln1_g, ln1_b,
              seq_len, per_batch_mod)
    x2 = _ffn(x1, mod6, w_up, w_conv, b_conv, w_down, ln2_g, ln2_b, seq_len, per_batch_mod)
    return x2, ml[1:]


def kernel(x_prompt, x_sample, c, state_C, state_n, state_m, c_ctx, w_ada, b_ada, w_in, b_gate,
           w_hnorm, w_br_m, w_br_f, w_out, ln1_g, ln1_b, w_up, w_conv, b_conv, w_down, ln2_g, ln2_b):
    assert w_ada.shape[0] == DEPTH
    B, S, _ = x_prompt.shape
    DB, DS, _ = x_sample.shape
    l = 0
    n_cond = 16
    cond = jnp.zeros((n_cond, D_MODEL), F32).at[0].set(c_ctx).at[1:1 + DB].set(c)
    mod6 = _modulation(cond, w_ada[l], b_ada[l]).reshape(n_cond, 6, D_MODEL)
    w_main, w_gate, b_gate_l = _split_w_in(w_in[l], b_gate[l])
    weights = (w_main, w_gate, b_gate_l, w_hnorm[l], w_br_m[l].astype(BF16), w_br_f[l].astype(BF16),
               w_out[l].astype(BF16), ln1_g[l], ln1_b[l], w_up[l].astype(BF16), w_conv[l], b_conv[l],
               w_down[l].astype(BF16), ln2_g[l], ln2_b[l])
    pos = _grid_pos_tables(DS)

    yp, states = _layer(x_prompt.reshape(B * S, D_MODEL), None, mod6[0:1], weights, B, S,
                        per_batch_mod=False, state=None, emit_state=True)
    ys, _ = _layer(x_sample.reshape(DB * DS, D_MODEL), pos, mod6[1:1 + DB], weights, DB, DS,
                   per_batch_mod=True, state=(state_C[:, l], state_n[:, l], state_m[:, l]),
                   emit_state=False)
    new_C, new_n, new_m = states
    new_n = new_n.reshape(B, DEPTH, N_DIR, N_HEADS, HEAD_DIM)
    new_m = new_m[:, :, :, 0, 0].reshape(B, DEPTH, N_DIR, N_HEADS)
    return (yp.reshape(B, S, D_MODEL), ys.reshape(DB, DS, D_MODEL), new_C, new_n, new_m)
```

```python
import functools

import jax
import jax.numpy as jnp
from jax import lax
from jax.experimental import pallas as pl
from jax.experimental.pallas import tpu as pltpu

D_MODEL = 2048
N_HEADS = 4
HEAD_DIM = 256
D_MLSTM = N_HEADS * HEAD_DIM
N_FGROUPS = 4
FGROUP_DIM = 256
D_FOURIER = N_FGROUPS * FGROUP_DIM
D_FF = 5632
GRID_W = 64
N_DIR = 2
DEPTH = 1
ALPHA = (2.0 * DEPTH) ** 0.25
LN_EPS = 1e-5

F32 = jnp.float32
BF16 = jnp.bfloat16

CHUNK = 256
COL_Q = 0
COL_K = COL_Q + D_MLSTM
COL_V = COL_K + D_MLSTM
COL_O = COL_V + D_MLSTM
COL_FR = COL_O + D_MLSTM
COL_GA = COL_FR + D_FOURIER
COL_GB = COL_GA + D_MODEL
D_MAIN = COL_GB + D_MODEL
GATE_BLOCK = 1024
GATE_LANES = 128
HALO = 8

VMEM_LIMIT = 56 * 1024 * 1024

INPROJ_TM = 1024
INPROJ_TN = 2304
MIX_TM = 256
FFN_TM_LONG = 1024
FFN_TM_SHORT = 512
FFN_TF = 512


def _params(sem, flags=None):
    return pltpu.CompilerParams(dimension_semantics=sem, vmem_limit_bytes=VMEM_LIMIT, flags=flags)


def _resident(shape, index_map):
    return pl.BlockSpec(shape, index_map, pipeline_mode=pl.Buffered(1))


def _norm(x):
    mu = jnp.mean(x, axis=-1, keepdims=True)
    xc = x - mu
    var = jnp.mean(xc * xc, axis=-1, keepdims=True)
    return xc * lax.rsqrt(var + LN_EPS)


def _dot(a, b):
    return jnp.dot(a, b, preferred_element_type=F32)


def _dot_nt(a, b):
    return lax.dot_general(a, b, (((1,), (1,)), ((), ())), preferred_element_type=F32)


def _mod_kernel(c_ref, w_ref, b_ref, o_ref):
    c = c_ref[...]
    s = c * jax.nn.sigmoid(c)
    o_ref[...] = _dot(s.astype(BF16), w_ref[...].astype(BF16)) + b_ref[...]


def _modulation(cond, w_ada, b_ada):
    rows, tn = cond.shape[0], 1024
    n_out = w_ada.shape[1]
    return pl.pallas_call(
        _mod_kernel,
        grid=(n_out // tn,),
        in_specs=[
            _resident((rows, D_MODEL), lambda j: (0, 0)),
            pl.BlockSpec((D_MODEL, tn), lambda j: (0, j)),
            pl.BlockSpec((1, tn), lambda j: (0, j)),
        ],
        out_specs=pl.BlockSpec((rows, tn), lambda j: (0, j)),
        out_shape=jax.ShapeDtypeStruct((rows, n_out), F32),
        compiler_params=_params(("arbitrary",)),
        name="modulation",
    )(cond, w_ada, b_ada.reshape(1, n_out))


LN_ROWS = 256


def _add_grid_pos(x, pr_ref, pc_ref, tok0):
    half = D_MODEL // 2
    out = []
    for k in range(x.shape[0] // GRID_W):
        blk = x[k * GRID_W:(k + 1) * GRID_W, :]
        pr = pr_ref[pl.ds(tok0 // GRID_W + k, 1), :]
        out.append(jnp.concatenate([blk[:, :half] + pr, blk[:, half:] + pc_ref[...]], axis=1))
    return jnp.concatenate(out, axis=0)


def _inproj_kernel(*refs, tm, tiles_per_seq, has_pos):
    if has_pos:
        x_ref, pr_ref, pc_ref, mod_ref, w_ref, wg_ref, bg_ref, p_ref, g_ref, h_scr = refs
    else:
        x_ref, mod_ref, w_ref, wg_ref, bg_ref, p_ref, g_ref, h_scr = refs

    @pl.when(pl.program_id(1) == 0)
    def _():
        sh = mod_ref[0, 0:1, :]
        sc = mod_ref[0, 1:2, :]

        def ln_rows(r, carry):
            rows = pl.ds(pl.multiple_of(r * LN_ROWS, LN_ROWS), LN_ROWS)
            x = x_ref[rows, :]
            if has_pos:
                x = _add_grid_pos(x, pr_ref, pc_ref,
                                  (pl.program_id(0) % tiles_per_seq) * tm + r * LN_ROWS)
            h = (_norm(x) * (1.0 + sc) + sh).astype(BF16)
            h_scr[rows, :] = h
            g_ref[rows, :] = _dot(h, wg_ref[...]) + bg_ref[...]
            return carry

        lax.fori_loop(0, tm // LN_ROWS, ln_rows, 0)

    p_ref[...] = _dot(h_scr[...], w_ref[...]).astype(BF16)


def _inproj(x2d, pos, mod6, w_main, w_gate, b_gate, seq_len, per_batch_mod):
    n_tok = x2d.shape[0]
    tm, tn = INPROJ_TM, INPROJ_TN
    tiles_per_seq = max(seq_len // tm, 1)
    has_pos = pos is not None
    mod_idx = (lambda i, j: (i // tiles_per_seq, 0, 0)) if per_batch_mod else (lambda i, j: (0, 0, 0))
    in_specs = [pl.BlockSpec((tm, D_MODEL), lambda i, j: (i, 0))]
    args = [x2d]
    if has_pos:
        in_specs += [_resident(p.shape, lambda i, j: (0, 0)) for p in pos]
        args += list(pos)
    in_specs += [
        pl.BlockSpec((1, 6, D_MODEL), mod_idx),
        pl.BlockSpec((D_MODEL, tn), lambda i, j: (0, j)),
        _resident((D_MODEL, GATE_LANES), lambda i, j: (0, 0)),
        _resident((1, GATE_LANES), lambda i, j: (0, 0)),
    ]
    args += [mod6, w_main, w_gate, b_gate]
    return pl.pallas_call(
        functools.partial(_inproj_kernel, tm=tm, tiles_per_seq=tiles_per_seq, has_pos=has_pos),
        grid=(n_tok // tm, D_MAIN // tn),
        in_specs=in_specs,
        out_specs=[
            pl.BlockSpec((tm, tn), lambda i, j: (i, j)),
            pl.BlockSpec((tm, GATE_LANES), lambda i, j: (i, 0)),
        ],
        out_shape=[
            jax.ShapeDtypeStruct((n_tok, D_MAIN), BF16),
            jax.ShapeDtypeStruct((n_tok, GATE_LANES), F32),
        ],
        scratch_shapes=[pltpu.VMEM((tm, D_MODEL), BF16)],
        compiler_params=_params(("parallel", "arbitrary")),
        name="inproj",
    )(*args)


def _log_sigmoid(x):
    return jnp.minimum(x, 0.0) - jnp.log1p(jnp.exp(-jnp.abs(x)))


def _head_out(h, o, wn):
    return (jax.nn.sigmoid(o.astype(F32)) * (_norm(h) * wn)).astype(BF16)


EXT = HEAD_DIM + GATE_LANES
PASS1_GROUP = 4


def _lane_tile(x, n):
    return jnp.concatenate([x] * n, axis=1)


def _mlstm_kernel(*refs, seq_len, has_state, emit_state):
    refs = list(refs)
    q_ref, k_ref, v_ref, o_ref, g_ref, wn_ref = refs[:6]
    refs = refs[6:]
    if has_state:
        m0_ref, c0_ref, n0_ref = refs[:3]
        refs = refs[3:]
    hm_ref = refs[0]
    refs = refs[1:]
    if emit_state:
        cout_ref, nout_ref, mout_ref = refs[:3]
        refs = refs[3:]
    num_scr, row_scr, kv_scr, sc_scr, c_scr, h_scr = refs
    L = CHUNK
    nc = seq_len // L
    scale = HEAD_DIM ** -0.5
    head = pl.program_id(1)
    gate_shift = jnp.where(head == 0, 0, GATE_LANES - 2 * N_DIR * head)

    def chunk_rows(c):
        return pl.ds(pl.multiple_of(c * L, L), L)

    def load_q(rows):
        return (q_ref[rows, :].astype(F32) * scale).astype(BF16)

    t_idx = lax.broadcasted_iota(jnp.int32, (L, L), 0)
    s_idx = lax.broadcasted_iota(jnp.int32, (L, L), 1)
    hi_rows = lax.broadcasted_iota(jnp.int32, (16, L), 0) < 8

    masks = [s_idx <= t_idx, s_idx >= t_idx]
    masks_b = [jnp.where(mk, 1.0, 0.0).astype(BF16) for mk in masks]
    group = min(nc, PASS1_GROUP)

    def pass1(grp, carry):
        chunks = [grp * group + i for i in range(group)]
        base, items = [], []
        for c in chunks:
            rows = chunk_rows(c)
            qb, kb, vb = load_q(rows), k_ref[rows, :], v_ref[rows, :]
            v_ext = jnp.concatenate([vb, jnp.ones((L, GATE_LANES), BF16)], axis=1)
            s0 = _dot_nt(qb, kb)
            g_row = jnp.transpose(pltpu.roll(g_ref[rows, :], gate_shift, 1))
            h_scr[rows, :] = jnp.zeros((L, HEAD_DIM), F32)
            base.append((c, rows, kb, v_ext, s0))
            for d in range(N_DIR):
                ig_row = g_row[2 * d:2 * d + 1, :]
                lf = _log_sigmoid(g_row[2 * d + 1:2 * d + 2, :])
                lf_hi = lf.astype(BF16)
                lf_lo = (lf - lf_hi.astype(F32)).astype(BF16)
                lhs = jnp.where(hi_rows, lf_hi.astype(F32), lf_lo.astype(F32)).astype(BF16)
                r16 = _dot_nt(lhs, masks_b[d])
                lf_rep = jnp.concatenate([jnp.broadcast_to(lf_hi, (GATE_LANES, L)),
                                          jnp.broadcast_to(lf_lo, (GATE_LANES, L))], axis=0)
                bb = _dot_nt(masks_b[d], lf_rep)
                items.append((len(base) - 1, d, ig_row, lf, r16, bb))
        states = []
        for bi, d, ig_row, lf, r16, bb in items:
            c, rows, kb, v_ext, s0 = base[bi]
            b_row = r16[0:1, :] + r16[8:9, :]
            b_rep = bb[:, :GATE_LANES] + bb[:, GATE_LANES:]
            c_row = ig_row - b_row
            c_max = jnp.broadcast_to(
                jnp.max(jnp.where(masks[d], c_row, -jnp.inf), axis=1, keepdims=True), (L, L))
            w = jnp.exp(jnp.where(masks[d], c_row - c_max, -jnp.inf))
            pv = _dot((s0 * w).astype(BF16), v_ext)
            num_scr[d, rows, :] = pv[:, :HEAD_DIM]
            row_scr[d, 0, rows, :] = pv[:, HEAD_DIM:]
            row_scr[d, 1, rows, :] = b_rep + c_max[:, :GATE_LANES]
            row_scr[d, 2, rows, :] = b_rep
            states.append((bi, d, ig_row, lf, b_row))
        k_ts = [jnp.transpose(kb.astype(F32)) for (_, _, kb, _, _) in base]
        for bi, d, ig_row, lf, b_row in states:
            c, rows, kb, v_ext, s0 = base[bi]
            b_last = jnp.sum(lf, axis=1, keepdims=True)
            g = b_last - b_row + ig_row
            g_max = jnp.max(g, axis=1, keepdims=True)
            wk = jnp.exp(g - g_max)
            kv_scr[d, c] = _dot((k_ts[bi] * wk).astype(BF16), v_ext)
            sc_scr[d, c, 0] = jnp.broadcast_to(b_last, (8, GATE_LANES))
            sc_scr[d, c, 1] = jnp.broadcast_to(g_max, (8, GATE_LANES))
        return carry

    lax.fori_loop(0, nc // group, pass1, 0)

    ms = []
    for d in range(N_DIR):
        if has_state:
            bh = pl.program_id(0) * (N_DIR * N_HEADS) + d * N_HEADS + head
            n_rep = jnp.transpose(jnp.broadcast_to(n0_ref[0, d, 0], (GATE_LANES, HEAD_DIM)))
            c_scr[d] = jnp.concatenate([c0_ref[0, d, 0], n_rep], axis=1)
            ms.append(jnp.full((1, GATE_LANES), m0_ref[bh], F32))
        else:
            c_scr[d] = jnp.zeros((HEAD_DIM, EXT), F32)
            ms.append(jnp.zeros((1, GATE_LANES), F32))

    def pass2(j, carry):
        new = []
        for d, m in enumerate(carry):
            c = j if d == 0 else nc - 1 - j
            rows = chunk_rows(c)
            c_ext = c_scr[d]
            qcn = _dot(load_q(rows), c_ext.astype(BF16))
            den_i, a_rep, b_rep = row_scr[d, 0, rows, :], row_scr[d, 1, rows, :], row_scr[d, 2, rows, :]
            m_rep = jnp.maximum(b_rep + m, a_rep)
            r_intra = jnp.exp(a_rep - m_rep)
            r_state = jnp.exp(b_rep + m - m_rep)
            den = r_intra * den_i + r_state * qcn[:, HEAD_DIM:]
            inv = 1.0 / jnp.maximum(jnp.abs(den), jnp.exp(-m_rep))
            h = (_lane_tile(r_intra * inv, 2) * num_scr[d, rows, :]
                 + _lane_tile(r_state * inv, 2) * qcn[:, :HEAD_DIM])
            h_scr[rows, :] += h
            b_last, g_max = sc_scr[d, c, 0][0:1, :], sc_scr[d, c, 1][0:1, :]
            m_new = jnp.maximum(b_last + m, g_max)
            decay = jnp.exp(b_last + m - m_new)
            gain = jnp.exp(g_max - m_new)
            c_scr[d] = _lane_tile(decay, 3) * c_ext + _lane_tile(gain, 3) * kv_scr[d, c]
            new.append(m_new)
        return tuple(new)

    ms = lax.fori_loop(0, nc, pass2, tuple(ms))

    wn = wn_ref[0]

    def finish(c, carry):
        rows = chunk_rows(c)
        hm_ref[rows, :] = _head_out(h_scr[rows, :], o_ref[rows, :], wn)
        return carry

    lax.fori_loop(0, nc, finish, 0)
    if emit_state:
        for d in range(N_DIR):
            c_ext = c_scr[d]
            cout_ref[0, 0, d, 0] = c_ext[:, :HEAD_DIM]
            nout_ref[0, 0, d, 0] = jnp.transpose(c_ext[:, HEAD_DIM:])[0:1, :]
            mout_ref[0, d, 0] = ms[d]


def _mlstm(p_main, gates, w_hnorm, batch, seq_len, state=None, emit_state=False):
    n_tok = p_main.shape[0]
    T = seq_len
    blk = lambda col: pl.BlockSpec((T, HEAD_DIM), lambda b, h: (b, col // HEAD_DIM + h))
    in_specs = [blk(COL_Q), blk(COL_K), blk(COL_V), blk(COL_O),
                pl.BlockSpec((T, GATE_LANES), lambda b, h: (b, 0)),
                pl.BlockSpec((1, 1, HEAD_DIM), lambda b, h: (h, 0, 0))]
    args = [p_main, p_main, p_main, p_main, gates, w_hnorm.reshape(N_HEADS, 1, HEAD_DIM)]
    has_state = state is not None
    if has_state:
        C0, n0, m0 = state
        in_specs += [
            pl.BlockSpec(memory_space=pltpu.SMEM),
            pl.BlockSpec((1, N_DIR, 1, HEAD_DIM, HEAD_DIM), lambda b, h: (b, 0, h, 0, 0)),
            pl.BlockSpec((1, N_DIR, 1, 1, HEAD_DIM), lambda b, h: (b, 0, h, 0, 0)),
        ]
        args += [m0.reshape(-1), C0, n0.reshape(batch, N_DIR, N_HEADS, 1, HEAD_DIM)]
    out_specs = [pl.BlockSpec((T, HEAD_DIM), lambda b, h: (b, h))]
    out_shape = [jax.ShapeDtypeStruct((n_tok, D_MLSTM), BF16)]
    if emit_state:
        out_specs += [
            pl.BlockSpec((1, 1, N_DIR, 1, HEAD_DIM, HEAD_DIM), lambda b, h: (b, 0, 0, h, 0, 0)),
            pl.BlockSpec((1, 1, N_DIR, 1, 1, HEAD_DIM), lambda b, h: (b, 0, 0, h, 0, 0)),
            pl.BlockSpec((1, N_DIR, 1, 1, GATE_LANES), lambda b, h: (b, 0, h, 0, 0)),
        ]
        out_shape += [
            jax.ShapeDtypeStruct((batch, DEPTH, N_DIR, N_HEADS, HEAD_DIM, HEAD_DIM), F32),
            jax.ShapeDtypeStruct((batch, DEPTH, N_DIR, N_HEADS, 1, HEAD_DIM), F32),
            jax.ShapeDtypeStruct((batch, N_DIR, N_HEADS, 1, GATE_LANES), F32),
        ]
    nc = T // CHUNK
    scratch = [pltpu.VMEM((N_DIR, T, HEAD_DIM), F32),
               pltpu.VMEM((N_DIR, 3, T, GATE_LANES), F32),
               pltpu.VMEM((N_DIR, nc, HEAD_DIM, EXT), F32),
               pltpu.VMEM((N_DIR, nc, 2, 8, GATE_LANES), F32),
               pltpu.VMEM((N_DIR, HEAD_DIM, EXT), F32),
               pltpu.VMEM((T, HEAD_DIM), F32)]
    return pl.pallas_call(
        functools.partial(_mlstm_kernel, seq_len=T, has_state=has_state, emit_state=emit_state),
        grid=(batch, N_HEADS),
        in_specs=in_specs,
        out_specs=out_specs,
        out_shape=out_shape,
        scratch_shapes=scratch,
        compiler_params=_params(("parallel", "parallel")),
        name="mlstm",
    )(*args)


def _fourier_kernel(x_ref, cs_ref, ct_ref, st_ref, o_ref, *, scale):
    for g in range(N_FGROUPS):
        cols = slice(g * FGROUP_DIM, (g + 1) * FGROUP_DIM)
        z = _dot(x_ref[:, cols], cs_ref[...])
        zc = z[:, :FGROUP_DIM].astype(BF16)
        zs = z[:, FGROUP_DIM:].astype(BF16)
        y = _dot(ct_ref[...], zc) - _dot(st_ref[...], zs)
        o_ref[:, cols] = (y * scale).astype(BF16)


def _dft_tables(n):
    k = jnp.arange(n, dtype=jnp.int32)
    ang = ((k[:, None] * k[None, :]) % n).astype(F32) * (2.0 * jnp.pi / n)
    return jnp.cos(ang), jnp.sin(ang)


DFT_BLOCK = 256
DFT_RADIX = 8
HALF_LANES = 128


def _cadd(a, b):
    return a[0] + b[0], a[1] + b[1]


def _csub(a, b):
    return a[0] - b[0], a[1] - b[1]


def _mul_neg_i(a):
    return a[1], -a[0]


def _dft4(y):
    c0, c1 = _cadd(y[0], y[2]), _cadd(y[1], y[3])
    d0, d1 = _csub(y[0], y[2]), _mul_neg_i(_csub(y[1], y[3]))
    return [_cadd(c0, c1), _cadd(d0, d1), _csub(c0, c1), _csub(d0, d1)]


def _dft8(x):
    r = 0.5 ** 0.5
    a = [_cadd(x[n], x[n + 4]) for n in range(4)]
    b = [_csub(x[n], x[n + 4]) for n in range(4)]
    b[1] = ((b[1][0] + b[1][1]) * r, (b[1][1] - b[1][0]) * r)
    b[2] = _mul_neg_i(b[2])
    b[3] = ((b[3][1] - b[3][0]) * r, -(b[3][0] + b[3][1]) * r)
    even, odd = _dft4(a), _dft4(b)
    return [even[k // 2] if k % 2 == 0 else odd[k // 2] for k in range(8)]


def _fourier_long_kernel(x_ref, cs_ref, twc_ref, tws_ref, o_ref, w_scr, *, scale):
    nb = DFT_BLOCK
    cs = cs_ref[...]
    c_tab, s_tab = cs[:, :nb], cs[:, nb:]
    z = []
    for t1 in range(DFT_RADIX):
        zz = _dot(x_ref[t1 * nb:(t1 + 1) * nb, :], cs)
        z.append((zz[:, :nb], -zz[:, nb:]))
    a = _dft8(z)
    for u1 in range(DFT_RADIX):
        ar, ai = a[u1]
        if u1 > 0:
            twc = twc_ref[u1 * nb:(u1 + 1) * nb, :]
            tws = tws_ref[u1 * nb:(u1 + 1) * nb, :]
            ar, ai = ar * twc + ai * tws, ai * twc - ar * tws
        y = (_dot(c_tab, ar.astype(BF16)) + _dot(s_tab, ai.astype(BF16))) * scale
        for half in range(FGROUP_DIM // HALF_LANES):
            w_scr[half, pl.ds(u1, nb, stride=DFT_RADIX), :] = y[:, half * HALF_LANES:(half + 1) * HALF_LANES]
    for half in range(FGROUP_DIM // HALF_LANES):
        o_ref[:, half * HALF_LANES:(half + 1) * HALF_LANES] = w_scr[half].astype(BF16)


def _fourier_long(p_main, batch, seq_len):
    n_tok = p_main.shape[0]
    T = seq_len
    assert T == DFT_RADIX * DFT_BLOCK and FGROUP_DIM == DFT_BLOCK
    cc, sc = _dft_tables(FGROUP_DIM)
    cs = jnp.concatenate([cc, sc], axis=1).astype(BF16)
    u1 = jnp.repeat(jnp.arange(DFT_RADIX, dtype=jnp.int32), DFT_BLOCK)
    t2 = jnp.tile(jnp.arange(DFT_BLOCK, dtype=jnp.int32), DFT_RADIX)
    ang = ((u1 * t2) % T).astype(F32) * (2.0 * jnp.pi / T)
    twc = jnp.broadcast_to(jnp.cos(ang)[:, None], (T, FGROUP_DIM))
    tws = jnp.broadcast_to(jnp.sin(ang)[:, None], (T, FGROUP_DIM))
    return pl.pallas_call(
        functools.partial(_fourier_long_kernel, scale=float((T * FGROUP_DIM) ** -0.5)),
        grid=(batch, N_FGROUPS),
        in_specs=[
            pl.BlockSpec((T, FGROUP_DIM), lambda b, g: (b, COL_FR // FGROUP_DIM + g)),
            _resident((FGROUP_DIM, 2 * FGROUP_DIM), lambda b, g: (0, 0)),
            _resident((T, FGROUP_DIM), lambda b, g: (0, 0)),
            _resident((T, FGROUP_DIM), lambda b, g: (0, 0)),
        ],
        out_specs=pl.BlockSpec((T, FGROUP_DIM), lambda b, g: (b, g)),
        out_shape=jax.ShapeDtypeStruct((n_tok, D_FOURIER), BF16),
        scratch_shapes=[pltpu.VMEM((FGROUP_DIM // HALF_LANES, T, HALF_LANES), F32)],
        compiler_params=_params(("parallel", "parallel")),
        name="fourier_long",
    )(p_main, cs, twc, tws)


def _fourier(p_main, batch, seq_len):
    if seq_len == DFT_RADIX * DFT_BLOCK:
        return _fourier_long(p_main, batch, seq_len)
    n_tok = p_main.shape[0]
    T = seq_len
    cc, sc = _dft_tables(FGROUP_DIM)
    cs = jnp.concatenate([cc, sc], axis=1).astype(BF16)
    ct, st = _dft_tables(T)
    return pl.pallas_call(
        functools.partial(_fourier_kernel, scale=float((T * FGROUP_DIM) ** -0.5)),
        grid=(batch,),
        in_specs=[
            pl.BlockSpec((T, D_FOURIER), lambda b: (b, COL_FR // D_FOURIER)),
            _resident((FGROUP_DIM, 2 * FGROUP_DIM), lambda b: (0, 0)),
            _resident((T, T), lambda b: (0, 0)),
            _resident((T, T), lambda b: (0, 0)),
        ],
        out_specs=pl.BlockSpec((T, D_FOURIER), lambda b: (b, 0)),
        out_shape=jax.ShapeDtypeStruct((n_tok, D_FOURIER), BF16),
        compiler_params=_params(("parallel",)),
        name="fourier",
    )(p_main, cs, ct.astype(BF16), st.astype(BF16))


def _mix_kernel(*refs, tm, tiles_per_seq, has_pos):
    if has_pos:
        (hm_ref, fr_ref, ga0_ref, ga1_ref, gb0_ref, gb1_ref, x_ref, pr_ref, pc_ref, mod_ref,
         wm_ref, wf_ref, wo_ref, lg_ref, lb_ref, o_ref) = refs
    else:
        (hm_ref, fr_ref, ga0_ref, ga1_ref, gb0_ref, gb1_ref, x_ref, mod_ref,
         wm_ref, wf_ref, wo_ref, lg_ref, lb_ref, o_ref) = refs
    a = _dot(hm_ref[...], wm_ref[...])
    b = _dot(fr_ref[...], wf_ref[...])
    gate = lambda g0, g1: jax.nn.sigmoid(jnp.concatenate([g0[...], g1[...]], axis=1).astype(F32))
    mixed = gate(ga0_ref, ga1_ref) * a + gate(gb0_ref, gb1_ref) * b
    z = _dot(mixed.astype(BF16), wo_ref[...])
    x = x_ref[...]
    if has_pos:
        x = _add_grid_pos(x, pr_ref, pc_ref, (pl.program_id(0) % tiles_per_seq) * tm)
    g1 = mod_ref[0, 2:3, :]
    o_ref[...] = _norm(ALPHA * x + g1 * z) * lg_ref[...] + lb_ref[...]


def _mix(hm, fr, p_main, x2d, pos, mod6, w_br_m, w_br_f, w_out, ln_g, ln_b, seq_len, per_batch_mod):
    n_tok = x2d.shape[0]
    tm = MIX_TM
    tiles_per_seq = seq_len // tm
    has_pos = pos is not None
    mod_idx = (lambda i: (i // tiles_per_seq, 0, 0)) if per_batch_mod else (lambda i: (0, 0, 0))
    in_specs = [
        pl.BlockSpec((tm, D_MLSTM), lambda i: (i, 0)),
        pl.BlockSpec((tm, D_FOURIER), lambda i: (i, 0)),
        pl.BlockSpec((tm, GATE_BLOCK), lambda i: (i, COL_GA // GATE_BLOCK)),
        pl.BlockSpec((tm, GATE_BLOCK), lambda i: (i, COL_GA // GATE_BLOCK + 1)),
        pl.BlockSpec((tm, GATE_BLOCK), lambda i: (i, COL_GB // GATE_BLOCK)),
        pl.BlockSpec((tm, GATE_BLOCK), lambda i: (i, COL_GB // GATE_BLOCK + 1)),
        pl.BlockSpec((tm, D_MODEL), lambda i: (i, 0)),
    ]
    args = [hm, fr, p_main, p_main, p_main, p_main, x2d]
    if has_pos:
        in_specs += [_resident(p.shape, lambda i: (0, 0)) for p in pos]
        args += list(pos)
    in_specs += [
        pl.BlockSpec((1, 6, D_MODEL), mod_idx),
        _resident((D_MLSTM, D_MODEL), lambda i: (0, 0)),
        _resident((D_FOURIER, D_MODEL), lambda i: (0, 0)),
        _resident((D_MODEL, D_MODEL), lambda i: (0, 0)),
        _resident((1, D_MODEL), lambda i: (0, 0)),
        _resident((1, D_MODEL), lambda i: (0, 0)),
    ]
    args += [mod6, w_br_m, w_br_f, w_out, ln_g.reshape(1, D_MODEL), ln_b.reshape(1, D_MODEL)]
    return pl.pallas_call(
        functools.partial(_mix_kernel, tm=tm, tiles_per_seq=tiles_per_seq, has_pos=has_pos),
        grid=(n_tok // tm,),
        in_specs=in_specs,
        out_specs=pl.BlockSpec((tm, D_MODEL), lambda i: (i, 0)),
        out_shape=jax.ShapeDtypeStruct((n_tok, D_MODEL), F32),
        compiler_params=_params(("parallel",)),
        name="mix",
    )(*args)


def _ffn_kernel(*refs, tm, seg_len, has_halo, tiles_per_seq):
    if has_halo:
        (x_ref, xp_ref, xn_ref, mod_ref, wv_ref, wg_ref, cwv_ref, cwg_ref, cbv_ref, cbg_ref,
         wd_ref, lg_ref, lb_ref, o_ref, h_scr) = refs
        acc_scr = o_ref
    else:
        (x_ref, mod_ref, wv_ref, wg_ref, cwv_ref, cwg_ref, cbv_ref, cbg_ref,
         wd_ref, lg_ref, lb_ref, o_ref, h_scr, acc_scr) = refs
    f = pl.program_id(1)
    n_seg = tm // seg_len
    stride = seg_len + 2 * HALO
    rows = n_seg * stride
    zeros = jnp.zeros((HALO, D_MODEL), BF16)

    @pl.when(f == 0)
    def _():
        sh = mod_ref[0, 3:4, :]
        sc = mod_ref[0, 4:5, :]
        modulate = lambda x: _norm(x) * (1.0 + sc) + sh
        for s in range(n_seg):
            h_scr[s * stride + HALO:(s + 1) * stride - HALO, :] = (
                modulate(x_ref[s * seg_len:(s + 1) * seg_len, :]).astype(BF16))
            h_scr[s * stride:s * stride + HALO, :] = zeros
            h_scr[(s + 1) * stride - HALO:(s + 1) * stride, :] = zeros
        if has_halo:
            t = pl.program_id(0) % tiles_per_seq
            hp = jnp.where(t == 0, 0.0, modulate(xp_ref[...]))
            hn = jnp.where(t == tiles_per_seq - 1, 0.0, modulate(xn_ref[...]))
            h_scr[0:HALO, :] = hp.astype(BF16)
            h_scr[rows - HALO:rows, :] = hn.astype(BF16)
        acc_scr[...] = jnp.zeros_like(acc_scr)

    def conv(u, cw_ref, cb_ref):
        prev = pltpu.roll(u, 1, 0)
        nxt = pltpu.roll(u, rows - 1, 0)
        y = prev * cw_ref[0:1, :] + u * cw_ref[1:2, :] + nxt * cw_ref[2:3, :] + cb_ref[...]
        return y[HALO:rows - HALO, :]

    h = h_scr[...]
    gate = conv(_dot(h, wg_ref[...]), cwg_ref, cbg_ref)
    gate = gate * jax.nn.sigmoid(gate)
    val = conv(_dot(h, wv_ref[...]), cwv_ref, cbv_ref)
    act = (gate * val).astype(BF16)
    acc_scr[...] += _dot(act, wd_ref[...])

    @pl.when(f == pl.num_programs(1) - 1)
    def _():
        g2 = mod_ref[0, 5:6, :]
        for s in range(n_seg):
            x = x_ref[s * seg_len:(s + 1) * seg_len, :]
            y = acc_scr[s * stride:s * stride + seg_len, :]
            o_ref[s * seg_len:(s + 1) * seg_len, :] = _norm(ALPHA * x + g2 * y) * lg_ref[...] + lb_ref[...]


def _ffn(x1, mod6, w_up, w_conv, b_conv, w_down, ln_g, ln_b, seq_len, per_batch_mod):
    n_tok = x1.shape[0]
    tf = FFN_TF
    nf = D_FF // tf
    tm = FFN_TM_LONG if seq_len >= FFN_TM_LONG else FFN_TM_SHORT
    seg_len = min(seq_len, tm)
    tiles_per_seq = max(seq_len // tm, 1)
    has_halo = seq_len > tm
    h_rows = (tm // seg_len) * (seg_len + 2 * HALO)
    mod_idx = (lambda i, f: (i // tiles_per_seq, 0, 0)) if per_batch_mod else (lambda i, f: (0, 0, 0))
    hb = tm // HALO
    n_hblk = n_tok // HALO
    in_specs = [pl.BlockSpec((tm, D_MODEL), lambda i, f: (i, 0))]
    args = [x1]
    scratch = [pltpu.VMEM((h_rows, D_MODEL), BF16)]
    if has_halo:
        in_specs = [_resident((tm, D_MODEL), lambda i, f: (i, 0))]
        in_specs += [
            pl.BlockSpec((HALO, D_MODEL), lambda i, f: (jnp.maximum(i * hb - 1, 0), 0)),
            pl.BlockSpec((HALO, D_MODEL), lambda i, f: (jnp.minimum((i + 1) * hb, n_hblk - 1), 0)),
        ]
        args += [x1, x1]
    else:
        scratch.append(pltpu.VMEM((h_rows - 2 * HALO, D_MODEL), F32))
    in_specs += [
        pl.BlockSpec((1, 6, D_MODEL), mod_idx),
        pl.BlockSpec((D_MODEL, tf), lambda i, f: (0, f)),
        pl.BlockSpec((D_MODEL, tf), lambda i, f: (0, nf + f)),
        pl.BlockSpec((3, tf), lambda i, f: (0, f)),
        pl.BlockSpec((3, tf), lambda i, f: (0, nf + f)),
        pl.BlockSpec((1, tf), lambda i, f: (0, f)),
        pl.BlockSpec((1, tf), lambda i, f: (0, nf + f)),
        pl.BlockSpec((tf, D_MODEL), lambda i, f: (f, 0)),
        _resident((1, D_MODEL), lambda i, f: (0, 0)),
        _resident((1, D_MODEL), lambda i, f: (0, 0)),
    ]
    b_conv2 = b_conv.reshape(1, 2 * D_FF)
    args += [mod6, w_up, w_up, w_conv, w_conv, b_conv2, b_conv2, w_down,
             ln_g.reshape(1, D_MODEL), ln_b.reshape(1, D_MODEL)]
    return pl.pallas_call(
        functools.partial(_ffn_kernel, tm=tm, seg_len=seg_len, has_halo=has_halo,
                          tiles_per_seq=tiles_per_seq),
        grid=(n_tok // tm, nf),
        in_specs=in_specs,
        out_specs=pl.BlockSpec((tm, D_MODEL), lambda i, f: (i, 0)),
        out_shape=jax.ShapeDtypeStruct((n_tok, D_MODEL), F32),
        scratch_shapes=scratch,
        compiler_params=_params(("parallel", "arbitrary")),
        name="ffn",
    )(*args)


def _grid_pos_tables(n_tokens):
    quarter = D_MODEL // 4
    freq = 1.0 / (10000.0 ** (jnp.arange(quarter, dtype=F32) / quarter))
    er = jnp.arange(n_tokens // GRID_W, dtype=F32)[:, None] * freq
    ec = jnp.arange(GRID_W, dtype=F32)[:, None] * freq
    return (jnp.concatenate([jnp.sin(er), jnp.cos(er)], -1),
            jnp.concatenate([jnp.sin(ec), jnp.cos(ec)], -1))


def _split_w_in(w_in, b_gate):
    n_gate = 2 * N_DIR * N_HEADS
    w_main = jnp.concatenate([w_in[:, :4 * D_MLSTM], w_in[:, 4 * D_MLSTM + n_gate:]], axis=1).astype(BF16)
    gw = w_in[:, 4 * D_MLSTM:4 * D_MLSTM + n_gate].reshape(D_MODEL, N_DIR, 2, N_HEADS)
    gw = gw.transpose(0, 3, 1, 2).reshape(D_MODEL, n_gate)
    gw = jnp.pad(gw, ((0, 0), (0, GATE_LANES - n_gate))).astype(BF16)
    gb = b_gate.astype(F32).reshape(N_DIR, 2, N_HEADS).transpose(2, 0, 1).reshape(1, n_gate)
    gb = jnp.pad(gb, ((0, 0), (0, GATE_LANES - n_gate)))
    return w_main, gw, gb


def _layer(x2d, pos, mod6, weights, batch, seq_len, per_batch_mod, state, emit_state):
    (w_main, w_gate, b_gate, w_hnorm, w_br_m, w_br_f, w_out, ln1_g, ln1_b,
     w_up, w_conv, b_conv, w_down, ln2_g, ln2_b) = weights
    p_main, gates = _inproj(x2d, pos, mod6, w_main, w_gate, b_gate, seq_len, per_batch_mod)
    ml = _mlstm(p_main, gates, w_hnorm, batch, seq_len, state=state, emit_state=emit_state)
    fr = _fourier(p_main, batch, seq_len)
    x1 = _mix(ml[0], fr, p_main, x2d, pos, mod6, w_br_m, w_br_f, w_out, ln1_g, ln1_b,
              seq_len, per_batch_mod)
    x2 = _ffn(x1, mod6, w_up, w_conv, b_conv, w_down, ln2_g, ln2_b, seq_len, per_batch_mod)
    return x2, ml[1:]


def kernel(x_prompt, x_sample, c, state_C, state_n, state_m, c_ctx, w_ada, b_ada, w_in, b_gate,
           w_hnorm, w_br_m, w_br_f, w_out, ln1_g, ln1_b, w_up, w_conv, b_conv, w_down, ln2_g, ln2_b):
    assert w_ada.shape[0] == DEPTH
    B, S, _ = x_prompt.shape
    DB, DS, _ = x_sample.shape
    l = 0
    n_cond = 16
    cond = jnp.zeros((n_cond, D_MODEL), F32).at[0].set(c_ctx).at[1:1 + DB].set(c)
    mod6 = _modulation(cond, w_ada[l], b_ada[l]).reshape(n_cond, 6, D_MODEL)
    w_main, w_gate, b_gate_l = _split_w_in(w_in[l], b_gate[l])
    weights = (w_main, w_gate, b_gate_l, w_hnorm[l], w_br_m[l].astype(BF16), w_br_f[l].astype(BF16),
               w_out[l].astype(BF16), ln1_g[l], ln1_b[l], w_up[l].astype(BF16), w_conv[l], b_conv[l],
               w_down[l].astype(BF16), ln2_g[l], ln2_b[l])
    pos = _grid_pos_tables(DS)

    yp, states = _layer(x_prompt.reshape(B * S, D_MODEL), None, mod6[0:1], weights, B, S,
                        per_batch_mod=False, state=None, emit_state=True)
    ys, _ = _layer(x_sample.reshape(DB * DS, D_MODEL), pos, mod6[1:1 + DB], weights, DB, DS,
                   per_batch_mod=True, state=(state_C[:, l], state_n[:, l], state_m[:, l]),
                   emit_state=False)
    new_C, new_n, new_m = states
    new_n = new_n.reshape(B, DEPTH, N_DIR, N_HEADS, HEAD_DIM)
    new_m = new_m[:, :, :, 0, 0].reshape(B, DEPTH, N_DIR, N_HEADS)
    return (yp.reshape(B, S, D_MODEL), ys.reshape(DB, DS, D_MODEL), new_C, new_n, new_m)
```

```python
import functools

import jax
import jax.numpy as jnp
from jax import lax
from jax.experimental import pallas as pl
from jax.experimental.pallas import tpu as pltpu

D_MODEL = 2048
N_HEADS = 4
HEAD_DIM = 256
D_MLSTM = N_HEADS * HEAD_DIM
N_FGROUPS = 4
FGROUP_DIM = 256
D_FOURIER = N_FGROUPS * FGROUP_DIM
D_FF = 5632
GRID_W = 64
N_DIR = 2
DEPTH = 1
ALPHA = (2.0 * DEPTH) ** 0.25
LN_EPS = 1e-5

F32 = jnp.float32
BF16 = jnp.bfloat16

CHUNK = 256
COL_Q = 0
COL_K = COL_Q + D_MLSTM
COL_V = COL_K + D_MLSTM
COL_O = COL_V + D_MLSTM
COL_FR = COL_O + D_MLSTM
COL_GA = COL_FR + D_FOURIER
COL_GB = COL_GA + D_MODEL
D_MAIN = COL_GB + D_MODEL
GATE_BLOCK = 1024
GATE_LANES = 128
HALO = 8

VMEM_LIMIT = 56 * 1024 * 1024

INPROJ_TM = 1024
INPROJ_TN = 2304
MIX_TM = 512
MIX_SUB = 256
FFN_TM_LONG = 1024
FFN_TM_SHORT = 512
FFN_TF = 512


def _params(sem, flags=None):
    return pltpu.CompilerParams(dimension_semantics=sem, vmem_limit_bytes=VMEM_LIMIT, flags=flags)


def _resident(shape, index_map):
    return pl.BlockSpec(shape, index_map, pipeline_mode=pl.Buffered(1))


def _norm(x):
    mu = jnp.mean(x, axis=-1, keepdims=True)
    xc = x - mu
    var = jnp.mean(xc * xc, axis=-1, keepdims=True)
    return xc * lax.rsqrt(var + LN_EPS)


def _dot(a, b):
    return jnp.dot(a, b, preferred_element_type=F32)


def _dot_nt(a, b):
    return lax.dot_general(a, b, (((1,), (1,)), ((), ())), preferred_element_type=F32)


def _mod_kernel(c_ref, w_ref, b_ref, o_ref):
    c = c_ref[...]
    s = c * jax.nn.sigmoid(c)
    o_ref[...] = _dot(s.astype(BF16), w_ref[...].astype(BF16)) + b_ref[...]


def _modulation(cond, w_ada, b_ada):
    rows, tn = cond.shape[0], 1024
    n_out = w_ada.shape[1]
    return pl.pallas_call(
        _mod_kernel,
        grid=(n_out // tn,),
        in_specs=[
            _resident((rows, D_MODEL), lambda j: (0, 0)),
            pl.BlockSpec((D_MODEL, tn), lambda j: (0, j)),
            pl.BlockSpec((1, tn), lambda j: (0, j)),
        ],
        out_specs=pl.BlockSpec((rows, tn), lambda j: (0, j)),
        out_shape=jax.ShapeDtypeStruct((rows, n_out), F32),
        compiler_params=_params(("arbitrary",)),
        name="modulation",
    )(cond, w_ada, b_ada.reshape(1, n_out))


LN_ROWS = 256


def _add_grid_pos(x, pr_ref, pc_ref, tok0):
    half = D_MODEL // 2
    out = []
    for k in range(x.shape[0] // GRID_W):
        blk = x[k * GRID_W:(k + 1) * GRID_W, :]
        pr = pr_ref[pl.ds(tok0 // GRID_W + k, 1), :]
        out.append(jnp.concatenate([blk[:, :half] + pr, blk[:, half:] + pc_ref[...]], axis=1))
    return jnp.concatenate(out, axis=0)


def _inproj_kernel(*refs, tm, tiles_per_seq, has_pos):
    if has_pos:
        x_ref, pr_ref, pc_ref, mod_ref, w_ref, wg_ref, bg_ref, p_ref, g_ref, h_scr = refs
    else:
        x_ref, mod_ref, w_ref, wg_ref, bg_ref, p_ref, g_ref, h_scr = refs

    @pl.when(pl.program_id(1) == 0)
    def _():
        sh = mod_ref[0, 0:1, :]
        sc = mod_ref[0, 1:2, :]

        def ln_rows(r, carry):
            rows = pl.ds(pl.multiple_of(r * LN_ROWS, LN_ROWS), LN_ROWS)
            x = x_ref[rows, :]
            if has_pos:
                x = _add_grid_pos(x, pr_ref, pc_ref,
                                  (pl.program_id(0) % tiles_per_seq) * tm + r * LN_ROWS)
            h = (_norm(x) * (1.0 + sc) + sh).astype(BF16)
            h_scr[rows, :] = h
            g_ref[rows, :] = _dot(h, wg_ref[...]) + bg_ref[...]
            return carry

        lax.fori_loop(0, tm // LN_ROWS, ln_rows, 0)

    p_ref[...] = _dot(h_scr[...], w_ref[...]).astype(BF16)


def _inproj(x2d, pos, mod6, w_main, w_gate, b_gate, seq_len, per_batch_mod):
    n_tok = x2d.shape[0]
    tm, tn = INPROJ_TM, INPROJ_TN
    tiles_per_seq = max(seq_len // tm, 1)
    has_pos = pos is not None
    mod_idx = (lambda i, j: (i // tiles_per_seq, 0, 0)) if per_batch_mod else (lambda i, j: (0, 0, 0))
    in_specs = [pl.BlockSpec((tm, D_MODEL), lambda i, j: (i, 0))]
    args = [x2d]
    if has_pos:
        in_specs += [_resident(p.shape, lambda i, j: (0, 0)) for p in pos]
        args += list(pos)
    in_specs += [
        pl.BlockSpec((1, 6, D_MODEL), mod_idx),
        pl.BlockSpec((D_MODEL, tn), lambda i, j: (0, j)),
        _resident((D_MODEL, GATE_LANES), lambda i, j: (0, 0)),
        _resident((1, GATE_LANES), lambda i, j: (0, 0)),
    ]
    args += [mod6, w_main, w_gate, b_gate]
    return pl.pallas_call(
        functools.partial(_inproj_kernel, tm=tm, tiles_per_seq=tiles_per_seq, has_pos=has_pos),
        grid=(n_tok // tm, D_MAIN // tn),
        in_specs=in_specs,
        out_specs=[
            pl.BlockSpec((tm, tn), lambda i, j: (i, j)),
            pl.BlockSpec((tm, GATE_LANES), lambda i, j: (i, 0)),
        ],
        out_shape=[
            jax.ShapeDtypeStruct((n_tok, D_MAIN), BF16),
            jax.ShapeDtypeStruct((n_tok, GATE_LANES), F32),
        ],
        scratch_shapes=[pltpu.VMEM((tm, D_MODEL), BF16)],
        compiler_params=_params(("parallel", "arbitrary")),
        name="inproj",
    )(*args)


def _log_sigmoid(x):
    return jnp.minimum(x, 0.0) - jnp.log1p(jnp.exp(-jnp.abs(x)))


def _head_out(h, o, wn):
    return (jax.nn.sigmoid(o.astype(F32)) * (_norm(h) * wn)).astype(BF16)


EXT = HEAD_DIM + GATE_LANES
PASS1_GROUP = 4


def _lane_tile(x, n):
    return jnp.concatenate([x] * n, axis=1)


def _mlstm_kernel(*refs, seq_len, has_state, emit_state):
    refs = list(refs)
    q_ref, k_ref, v_ref, o_ref, g_ref, wn_ref = refs[:6]
    refs = refs[6:]
    if has_state:
        m0_ref, c0_ref, n0_ref = refs[:3]
        refs = refs[3:]
    hm_ref = refs[0]
    refs = refs[1:]
    if emit_state:
        cout_ref, nout_ref, mout_ref = refs[:3]
        refs = refs[3:]
    num_scr, row_scr, kv_scr, sc_scr, c_scr, h_scr = refs
    L = CHUNK
    nc = seq_len // L
    scale = HEAD_DIM ** -0.5
    head = pl.program_id(1)
    gate_shift = jnp.where(head == 0, 0, GATE_LANES - 2 * N_DIR * head)

    def chunk_rows(c):
        return pl.ds(pl.multiple_of(c * L, L), L)

    def load_q(rows):
        return (q_ref[rows, :].astype(F32) * scale).astype(BF16)

    t_idx = lax.broadcasted_iota(jnp.int32, (L, L), 0)
    s_idx = lax.broadcasted_iota(jnp.int32, (L, L), 1)
    hi_rows = lax.broadcasted_iota(jnp.int32, (16, L), 0) < 8

    masks = [s_idx <= t_idx, s_idx >= t_idx]
    masks_b = [jnp.where(mk, 1.0, 0.0).astype(BF16) for mk in masks]
    group = min(nc, PASS1_GROUP)

    def pass1(grp, carry):
        chunks = [grp * group + i for i in range(group)]
        base, items = [], []
        for c in chunks:
            rows = chunk_rows(c)
            qb, kb, vb = load_q(rows), k_ref[rows, :], v_ref[rows, :]
            v_ext = jnp.concatenate([vb, jnp.ones((L, GATE_LANES), BF16)], axis=1)
            s0 = _dot_nt(qb, kb)
            g_row = jnp.transpose(pltpu.roll(g_ref[rows, :], gate_shift, 1))
            h_scr[rows, :] = jnp.zeros((L, HEAD_DIM), F32)
            base.append((c, rows, kb, v_ext, s0))
            for d in range(N_DIR):
                ig_row = g_row[2 * d:2 * d + 1, :]
                lf = _log_sigmoid(g_row[2 * d + 1:2 * d + 2, :])
                lf_hi = lf.astype(BF16)
                lf_lo = (lf - lf_hi.astype(F32)).astype(BF16)
                lhs = jnp.where(hi_rows, lf_hi.astype(F32), lf_lo.astype(F32)).astype(BF16)
                r16 = _dot_nt(lhs, masks_b[d])
                lf_rep = jnp.concatenate([jnp.broadcast_to(lf_hi, (GATE_LANES, L)),
                                          jnp.broadcast_to(lf_lo, (GATE_LANES, L))], axis=0)
                bb = _dot_nt(masks_b[d], lf_rep)
                items.append((len(base) - 1, d, ig_row, lf, r16, bb))
        states = []
        for bi, d, ig_row, lf, r16, bb in items:
            c, rows, kb, v_ext, s0 = base[bi]
            b_row = r16[0:1, :] + r16[8:9, :]
            b_rep = bb[:, :GATE_LANES] + bb[:, GATE_LANES:]
            c_row = ig_row - b_row
            c_max = jnp.broadcast_to(
                jnp.max(jnp.where(masks[d], c_row, -jnp.inf), axis=1, keepdims=True), (L, L))
            w = jnp.exp(jnp.where(masks[d], c_row - c_max, -jnp.inf))
            pv = _dot((s0 * w).astype(BF16), v_ext)
            num_scr[d, rows, :] = pv[:, :HEAD_DIM]
            row_scr[d, 0, rows, :] = pv[:, HEAD_DIM:]
            row_scr[d, 1, rows, :] = b_rep + c_max[:, :GATE_LANES]
            row_scr[d, 2, rows, :] = b_rep
            states.append((bi, d, ig_row, lf, b_row))
        k_ts = [jnp.transpose(kb.astype(F32)) for (_, _, kb, _, _) in base]
        for bi, d, ig_row, lf, b_row in states:
            c, rows, kb, v_ext, s0 = base[bi]
            b_last = jnp.sum(lf, axis=1, keepdims=True)
            g = b_last - b_row + ig_row
            g_max = jnp.max(g, axis=1, keepdims=True)
            wk = jnp.exp(g - g_max)
            kv_scr[d, c] = _dot((k_ts[bi] * wk).astype(BF16), v_ext)
            sc_scr[d, c, 0] = jnp.broadcast_to(b_last, (8, GATE_LANES))
            sc_scr[d, c, 1] = jnp.broadcast_to(g_max, (8, GATE_LANES))
        return carry

    lax.fori_loop(0, nc // group, pass1, 0)

    ms = []
    for d in range(N_DIR):
        if has_state:
            bh = pl.program_id(0) * (N_DIR * N_HEADS) + d * N_HEADS + head
            n_rep = jnp.transpose(jnp.broadcast_to(n0_ref[0, d, 0], (GATE_LANES, HEAD_DIM)))
            c_scr[d] = jnp.concatenate([c0_ref[0, d, 0], n_rep], axis=1)
            ms.append(jnp.full((1, GATE_LANES), m0_ref[bh], F32))
        else:
            c_scr[d] = jnp.zeros((HEAD_DIM, EXT), F32)
            ms.append(jnp.zeros((1, GATE_LANES), F32))

    def pass2(j, carry):
        new = []
        for d, m in enumerate(carry):
            c = j if d == 0 else nc - 1 - j
            rows = chunk_rows(c)
            c_ext = c_scr[d]
            qcn = _dot(load_q(rows), c_ext.astype(BF16))
            den_i, a_rep, b_rep = row_scr[d, 0, rows, :], row_scr[d, 1, rows, :], row_scr[d, 2, rows, :]
            m_rep = jnp.maximum(b_rep + m, a_rep)
            r_intra = jnp.exp(a_rep - m_rep)
            r_state = jnp.exp(b_rep + m - m_rep)
            den = r_intra * den_i + r_state * qcn[:, HEAD_DIM:]
            inv = 1.0 / jnp.maximum(jnp.abs(den), jnp.exp(-m_rep))
            h = (_lane_tile(r_intra * inv, 2) * num_scr[d, rows, :]
                 + _lane_tile(r_state * inv, 2) * qcn[:, :HEAD_DIM])
            h_scr[rows, :] += h
            b_last, g_max = sc_scr[d, c, 0][0:1, :], sc_scr[d, c, 1][0:1, :]
            m_new = jnp.maximum(b_last + m, g_max)
            decay = jnp.exp(b_last + m - m_new)
            gain = jnp.exp(g_max - m_new)
            c_scr[d] = _lane_tile(decay, 3) * c_ext + _lane_tile(gain, 3) * kv_scr[d, c]
            new.append(m_new)
        return tuple(new)

    ms = lax.fori_loop(0, nc, pass2, tuple(ms))

    wn = wn_ref[0]

    def finish(c, carry):
        rows = chunk_rows(c)
        hm_ref[rows, :] = _head_out(h_scr[rows, :], o_ref[rows, :], wn)
        return carry

    lax.fori_loop(0, nc, finish, 0)
    if emit_state:
        for d in range(N_DIR):
            c_ext = c_scr[d]
            cout_ref[0, 0, d, 0] = c_ext[:, :HEAD_DIM]
            nout_ref[0, 0, d, 0] = jnp.transpose(c_ext[:, HEAD_DIM:])[0:1, :]
            mout_ref[0, d, 0] = ms[d]


def _mlstm(p_main, gates, w_hnorm, batch, seq_len, state=None, emit_state=False):
    n_tok = p_main.shape[0]
    T = seq_len
    blk = lambda col: pl.BlockSpec((T, HEAD_DIM), lambda b, h: (b, col // HEAD_DIM + h))
    in_specs = [blk(COL_Q), blk(COL_K), blk(COL_V), blk(COL_O),
                pl.BlockSpec((T, GATE_LANES), lambda b, h: (b, 0)),
                pl.BlockSpec((1, 1, HEAD_DIM), lambda b, h: (h, 0, 0))]
    args = [p_main, p_main, p_main, p_main, gates, w_hnorm.reshape(N_HEADS, 1, HEAD_DIM)]
    has_state = state is not None
    if has_state:
        C0, n0, m0 = state
        in_specs += [
            pl.BlockSpec(memory_space=pltpu.SMEM),
            pl.BlockSpec((1, N_DIR, 1, HEAD_DIM, HEAD_DIM), lambda b, h: (b, 0, h, 0, 0)),
            pl.BlockSpec((1, N_DIR, 1, 1, HEAD_DIM), lambda b, h: (b, 0, h, 0, 0)),
        ]
        args += [m0.reshape(-1), C0, n0.reshape(batch, N_DIR, N_HEADS, 1, HEAD_DIM)]
    out_specs = [pl.BlockSpec((T, HEAD_DIM), lambda b, h: (b, h))]
    out_shape = [jax.ShapeDtypeStruct((n_tok, D_MLSTM), BF16)]
    if emit_state:
        out_specs += [
            pl.BlockSpec((1, 1, N_DIR, 1, HEAD_DIM, HEAD_DIM), lambda b, h: (b, 0, 0, h, 0, 0)),
            pl.BlockSpec((1, 1, N_DIR, 1, 1, HEAD_DIM), lambda b, h: (b, 0, 0, h, 0, 0)),
            pl.BlockSpec((1, N_DIR, 1, 1, GATE_LANES), lambda b, h: (b, 0, h, 0, 0)),
        ]
        out_shape += [
            jax.ShapeDtypeStruct((batch, DEPTH, N_DIR, N_HEADS, HEAD_DIM, HEAD_DIM), F32),
            jax.ShapeDtypeStruct((batch, DEPTH, N_DIR, N_HEADS, 1, HEAD_DIM), F32),
            jax.ShapeDtypeStruct((batch, N_DIR, N_HEADS, 1, GATE_LANES), F32),
        ]
    nc = T // CHUNK
    scratch = [pltpu.VMEM((N_DIR, T, HEAD_DIM), F32),
               pltpu.VMEM((N_DIR, 3, T, GATE_LANES), F32),
               pltpu.VMEM((N_DIR, nc, HEAD_DIM, EXT), F32),
               pltpu.VMEM((N_DIR, nc, 2, 8, GATE_LANES), F32),
               pltpu.VMEM((N_DIR, HEAD_DIM, EXT), F32),
               pltpu.VMEM((T, HEAD_DIM), F32)]
    return pl.pallas_call(
        functools.partial(_mlstm_kernel, seq_len=T, has_state=has_state, emit_state=emit_state),
        grid=(batch, N_HEADS),
        in_specs=in_specs,
        out_specs=out_specs,
        out_shape=out_shape,
        scratch_shapes=scratch,
        compiler_params=_params(("parallel", "parallel")),
        name="mlstm",
    )(*args)


def _fourier_kernel(x_ref, cs_ref, ct_ref, st_ref, o_ref, *, scale):
    for g in range(N_FGROUPS):
        cols = slice(g * FGROUP_DIM, (g + 1) * FGROUP_DIM)
        z = _dot(x_ref[:, cols], cs_ref[...])
        zc = z[:, :FGROUP_DIM].astype(BF16)
        zs = z[:, FGROUP_DIM:].astype(BF16)
        y = _dot(ct_ref[...], zc) - _dot(st_ref[...], zs)
        o_ref[:, cols] = (y * scale).astype(BF16)


def _dft_tables(n):
    k = jnp.arange(n, dtype=jnp.int32)
    ang = ((k[:, None] * k[None, :]) % n).astype(F32) * (2.0 * jnp.pi / n)
    return jnp.cos(ang), jnp.sin(ang)


DFT_BLOCK = 256
DFT_RADIX = 8
HALF_LANES = 128


def _cadd(a, b):
    return a[0] + b[0], a[1] + b[1]


def _csub(a, b):
    return a[0] - b[0], a[1] - b[1]


def _mul_neg_i(a):
    return a[1], -a[0]


def _dft4(y):
    c0, c1 = _cadd(y[0], y[2]), _cadd(y[1], y[3])
    d0, d1 = _csub(y[0], y[2]), _mul_neg_i(_csub(y[1], y[3]))
    return [_cadd(c0, c1), _cadd(d0, d1), _csub(c0, c1), _csub(d0, d1)]


def _dft8(x):
    r = 0.5 ** 0.5
    a = [_cadd(x[n], x[n + 4]) for n in range(4)]
    b = [_csub(x[n], x[n + 4]) for n in range(4)]
    b[1] = ((b[1][0] + b[1][1]) * r, (b[1][1] - b[1][0]) * r)
    b[2] = _mul_neg_i(b[2])
    b[3] = ((b[3][1] - b[3][0]) * r, -(b[3][0] + b[3][1]) * r)
    even, odd = _dft4(a), _dft4(b)
    return [even[k // 2] if k % 2 == 0 else odd[k // 2] for k in range(8)]


def _fourier_long_kernel(x_ref, cs_ref, twc_ref, tws_ref, o_ref, w_scr, *, scale):
    nb = DFT_BLOCK
    cs = cs_ref[...]
    c_tab, s_tab = cs[:, :nb], cs[:, nb:]
    z = []
    for t1 in range(DFT_RADIX):
        zz = _dot(x_ref[t1 * nb:(t1 + 1) * nb, :], cs)
        z.append((zz[:, :nb], -zz[:, nb:]))
    a = _dft8(z)
    for u1 in range(DFT_RADIX):
        ar, ai = a[u1]
        if u1 > 0:
            twc = twc_ref[u1 * nb:(u1 + 1) * nb, :]
            tws = tws_ref[u1 * nb:(u1 + 1) * nb, :]
            ar, ai = ar * twc + ai * tws, ai * twc - ar * tws
        y = (_dot(c_tab, ar.astype(BF16)) + _dot(s_tab, ai.astype(BF16))) * scale
        for half in range(FGROUP_DIM // HALF_LANES):
            w_scr[half, pl.ds(u1, nb, stride=DFT_RADIX), :] = y[:, half * HALF_LANES:(half + 1) * HALF_LANES]
    for half in range(FGROUP_DIM // HALF_LANES):
        o_ref[:, half * HALF_LANES:(half + 1) * HALF_LANES] = w_scr[half].astype(BF16)


def _fourier_long(p_main, batch, seq_len):
    n_tok = p_main.shape[0]
    T = seq_len
    assert T == DFT_RADIX * DFT_BLOCK and FGROUP_DIM == DFT_BLOCK
    cc, sc = _dft_tables(FGROUP_DIM)
    cs = jnp.concatenate([cc, sc], axis=1).astype(BF16)
    u1 = jnp.repeat(jnp.arange(DFT_RADIX, dtype=jnp.int32), DFT_BLOCK)
    t2 = jnp.tile(jnp.arange(DFT_BLOCK, dtype=jnp.int32), DFT_RADIX)
    ang = ((u1 * t2) % T).astype(F32) * (2.0 * jnp.pi / T)
    twc = jnp.broadcast_to(jnp.cos(ang)[:, None], (T, FGROUP_DIM))
    tws = jnp.broadcast_to(jnp.sin(ang)[:, None], (T, FGROUP_DIM))
    return pl.pallas_call(
        functools.partial(_fourier_long_kernel, scale=float((T * FGROUP_DIM) ** -0.5)),
        grid=(batch, N_FGROUPS),
        in_specs=[
            pl.BlockSpec((T, FGROUP_DIM), lambda b, g: (b, COL_FR // FGROUP_DIM + g)),
            _resident((FGROUP_DIM, 2 * FGROUP_DIM), lambda b, g: (0, 0)),
            _resident((T, FGROUP_DIM), lambda b, g: (0, 0)),
            _resident((T, FGROUP_DIM), lambda b, g: (0, 0)),
        ],
        out_specs=pl.BlockSpec((T, FGROUP_DIM), lambda b, g: (b, g)),
        out_shape=jax.ShapeDtypeStruct((n_tok, D_FOURIER), BF16),
        scratch_shapes=[pltpu.VMEM((FGROUP_DIM // HALF_LANES, T, HALF_LANES), F32)],
        compiler_params=_params(("parallel", "parallel")),
        name="fourier_long",
    )(p_main, cs, twc, tws)


def _fourier(p_main, batch, seq_len):
    if seq_len == DFT_RADIX * DFT_BLOCK:
        return _fourier_long(p_main, batch, seq_len)
    n_tok = p_main.shape[0]
    T = seq_len
    cc, sc = _dft_tables(FGROUP_DIM)
    cs = jnp.concatenate([cc, sc], axis=1).astype(BF16)
    ct, st = _dft_tables(T)
    return pl.pallas_call(
        functools.partial(_fourier_kernel, scale=float((T * FGROUP_DIM) ** -0.5)),
        grid=(batch,),
        in_specs=[
            pl.BlockSpec((T, D_FOURIER), lambda b: (b, COL_FR // D_FOURIER)),
            _resident((FGROUP_DIM, 2 * FGROUP_DIM), lambda b: (0, 0)),
            _resident((T, T), lambda b: (0, 0)),
            _resident((T, T), lambda b: (0, 0)),
        ],
        out_specs=pl.BlockSpec((T, D_FOURIER), lambda b: (b, 0)),
        out_shape=jax.ShapeDtypeStruct((n_tok, D_FOURIER), BF16),
        compiler_params=_params(("parallel",)),
        name="fourier",
    )(p_main, cs, ct.astype(BF16), st.astype(BF16))


def _mix_kernel(*refs, tm, tiles_per_seq, has_pos):
    if has_pos:
        (hm_ref, fr_ref, ga0_ref, ga1_ref, gb0_ref, gb1_ref, x_ref, pr_ref, pc_ref, mod_ref,
         wm_ref, wf_ref, wo_ref, lg_ref, lb_ref, o_ref) = refs
    else:
        (hm_ref, fr_ref, ga0_ref, ga1_ref, gb0_ref, gb1_ref, x_ref, mod_ref,
         wm_ref, wf_ref, wo_ref, lg_ref, lb_ref, o_ref) = refs
    g1 = mod_ref[0, 2:3, :]
    for r in range(tm // MIX_SUB):
        rows = slice(r * MIX_SUB, (r + 1) * MIX_SUB)
        a = _dot(hm_ref[rows, :], wm_ref[...])
        b = _dot(fr_ref[rows, :], wf_ref[...])
        gate = lambda g0, g1_: jax.nn.sigmoid(jnp.concatenate([g0[rows, :], g1_[rows, :]], axis=1).astype(F32))
        mixed = gate(ga0_ref, ga1_ref) * a + gate(gb0_ref, gb1_ref) * b
        z = _dot(mixed.astype(BF16), wo_ref[...])
        x = x_ref[rows, :]
        if has_pos:
            x = _add_grid_pos(x, pr_ref, pc_ref, (pl.program_id(0) % tiles_per_seq) * tm + r * MIX_SUB)
        o_ref[rows, :] = _norm(ALPHA * x + g1 * z) * lg_ref[...] + lb_ref[...]


def _mix(hm, fr, p_main, x2d, pos, mod6, w_br_m, w_br_f, w_out, ln_g, ln_b, seq_len, per_batch_mod):
    n_tok = x2d.shape[0]
    tm = MIX_TM
    tiles_per_seq = max(seq_len // tm, 1)
    has_pos = pos is not None
    mod_idx = (lambda i: (i // tiles_per_seq, 0, 0)) if per_batch_mod else (lambda i: (0, 0, 0))
    in_specs = [
        pl.BlockSpec((tm, D_MLSTM), lambda i: (i, 0)),
        pl.BlockSpec((tm, D_FOURIER), lambda i: (i, 0)),
        pl.BlockSpec((tm, GATE_BLOCK), lambda i: (i, COL_GA // GATE_BLOCK)),
        pl.BlockSpec((tm, GATE_BLOCK), lambda i: (i, COL_GA // GATE_BLOCK + 1)),
        pl.BlockSpec((tm, GATE_BLOCK), lambda i: (i, COL_GB // GATE_BLOCK)),
        pl.BlockSpec((tm, GATE_BLOCK), lambda i: (i, COL_GB // GATE_BLOCK + 1)),
        pl.BlockSpec((tm, D_MODEL), lambda i: (i, 0)),
    ]
    args = [hm, fr, p_main, p_main, p_main, p_main, x2d]
    if has_pos:
        in_specs += [_resident(p.shape, lambda i: (0, 0)) for p in pos]
        args += list(pos)
    in_specs += [
        pl.BlockSpec((1, 6, D_MODEL), mod_idx),
        _resident((D_MLSTM, D_MODEL), lambda i: (0, 0)),
        _resident((D_FOURIER, D_MODEL), lambda i: (0, 0)),
        _resident((D_MODEL, D_MODEL), lambda i: (0, 0)),
        _resident((1, D_MODEL), lambda i: (0, 0)),
        _resident((1, D_MODEL), lambda i: (0, 0)),
    ]
    args += [mod6, w_br_m, w_br_f, w_out, ln_g.reshape(1, D_MODEL), ln_b.reshape(1, D_MODEL)]
    return pl.pallas_call(
        functools.partial(_mix_kernel, tm=tm, tiles_per_seq=tiles_per_seq, has_pos=has_pos),
        grid=(n_tok // tm,),
        in_specs=in_specs,
        out_specs=pl.BlockSpec((tm, D_MODEL), lambda i: (i, 0)),
        out_shape=jax.ShapeDtypeStruct((n_tok, D_MODEL), F32),
        compiler_params=_params(("parallel",)),
        name="mix",
    )(*args)


def _ffn_kernel(*refs, tm, seg_len, has_halo, tiles_per_seq):
    if has_halo:
        (x_ref, xp_ref, xn_ref, mod_ref, wv_ref, wg_ref, cwv_ref, cwg_ref, cbv_ref, cbg_ref,
         wd_ref, lg_ref, lb_ref, o_ref, h_scr) = refs
        acc_scr = o_ref
    else:
        (x_ref, mod_ref, wv_ref, wg_ref, cwv_ref, cwg_ref, cbv_ref, cbg_ref,
         wd_ref, lg_ref, lb_ref, o_ref, h_scr, acc_scr) = refs
    f = pl.program_id(1)
    n_seg = tm // seg_len
    stride = seg_len + 2 * HALO
    rows = n_seg * stride
    zeros = jnp.zeros((HALO, D_MODEL), BF16)

    @pl.when(f == 0)
    def _():
        sh = mod_ref[0, 3:4, :]
        sc = mod_ref[0, 4:5, :]
        modulate = lambda x: _norm(x) * (1.0 + sc) + sh
        for s in range(n_seg):
            h_scr[s * stride + HALO:(s + 1) * stride - HALO, :] = (
                modulate(x_ref[s * seg_len:(s + 1) * seg_len, :]).astype(BF16))
            h_scr[s * stride:s * stride + HALO, :] = zeros
            h_scr[(s + 1) * stride - HALO:(s + 1) * stride, :] = zeros
        if has_halo:
            t = pl.program_id(0) % tiles_per_seq
            hp = jnp.where(t == 0, 0.0, modulate(xp_ref[...]))
            hn = jnp.where(t == tiles_per_seq - 1, 0.0, modulate(xn_ref[...]))
            h_scr[0:HALO, :] = hp.astype(BF16)
            h_scr[rows - HALO:rows, :] = hn.astype(BF16)
        acc_scr[...] = jnp.zeros_like(acc_scr)

    def conv(u, cw_ref, cb_ref):
        prev = pltpu.roll(u, 1, 0)
        nxt = pltpu.roll(u, rows - 1, 0)
        y = prev * cw_ref[0:1, :] + u * cw_ref[1:2, :] + nxt * cw_ref[2:3, :] + cb_ref[...]
        return y[HALO:rows - HALO, :]

    h = h_scr[...]
    gate = conv(_dot(h, wg_ref[...]), cwg_ref, cbg_ref)
    gate = gate * jax.nn.sigmoid(gate)
    val = conv(_dot(h, wv_ref[...]), cwv_ref, cbv_ref)
    act = (gate * val).astype(BF16)
    acc_scr[...] += _dot(act, wd_ref[...])

    @pl.when(f == pl.num_programs(1) - 1)
    def _():
        g2 = mod_ref[0, 5:6, :]
        for s in range(n_seg):
            x = x_ref[s * seg_len:(s + 1) * seg_len, :]
            y = acc_scr[s * stride:s * stride + seg_len, :]
            o_ref[s * seg_len:(s + 1) * seg_len, :] = _norm(ALPHA * x + g2 * y) * lg_ref[...] + lb_ref[...]


def _ffn(x1, mod6, w_up, w_conv, b_conv, w_down, ln_g, ln_b, seq_len, per_batch_mod):
    n_tok = x1.shape[0]
    tf = FFN_TF
    nf = D_FF // tf
    tm = FFN_TM_LONG if seq_len >= FFN_TM_LONG else FFN_TM_SHORT
    seg_len = min(seq_len, tm)
    tiles_per_seq = max(seq_len // tm, 1)
    has_halo = seq_len > tm
    h_rows = (tm // seg_len) * (seg_len + 2 * HALO)
    mod_idx = (lambda i, f: (i // tiles_per_seq, 0, 0)) if per_batch_mod else (lambda i, f: (0, 0, 0))
    hb = tm // HALO
    n_hblk = n_tok // HALO
    in_specs = [pl.BlockSpec((tm, D_MODEL), lambda i, f: (i, 0))]
    args = [x1]
    scratch = [pltpu.VMEM((h_rows, D_MODEL), BF16)]
    if has_halo:
        in_specs = [_resident((tm, D_MODEL), lambda i, f: (i, 0))]
        in_specs += [
            pl.BlockSpec((HALO, D_MODEL), lambda i, f: (jnp.maximum(i * hb - 1, 0), 0)),
            pl.BlockSpec((HALO, D_MODEL), lambda i, f: (jnp.minimum((i + 1) * hb, n_hblk - 1), 0)),
        ]
        args += [x1, x1]
    else:
        scratch.append(pltpu.VMEM((h_rows - 2 * HALO, D_MODEL), F32))
    in_specs += [
        pl.BlockSpec((1, 6, D_MODEL), mod_idx),
        pl.BlockSpec((D_MODEL, tf), lambda i, f: (0, f)),
        pl.BlockSpec((D_MODEL, tf), lambda i, f: (0, nf + f)),
        pl.BlockSpec((3, tf), lambda i, f: (0, f)),
        pl.BlockSpec((3, tf), lambda i, f: (0, nf + f)),
        pl.BlockSpec((1, tf), lambda i, f: (0, f)),
        pl.BlockSpec((1, tf), lambda i, f: (0, nf + f)),
        pl.BlockSpec((tf, D_MODEL), lambda i, f: (f, 0)),
        _resident((1, D_MODEL), lambda i, f: (0, 0)),
        _resident((1, D_MODEL), lambda i, f: (0, 0)),
    ]
    b_conv2 = b_conv.reshape(1, 2 * D_FF)
    args += [mod6, w_up, w_up, w_conv, w_conv, b_conv2, b_conv2, w_down,
             ln_g.reshape(1, D_MODEL), ln_b.reshape(1, D_MODEL)]
    return pl.pallas_call(
        functools.partial(_ffn_kernel, tm=tm, seg_len=seg_len, has_halo=has_halo,
                          tiles_per_seq=tiles_per_seq),
        grid=(n_tok // tm, nf),
        in_specs=in_specs,
        out_specs=pl.BlockSpec((tm, D_MODEL), lambda i, f: (i, 0)),
        out_shape=jax.ShapeDtypeStruct((n_tok, D_MODEL), F32),
        scratch_shapes=scratch,
        compiler_params=_params(("parallel", "arbitrary")),
        name="ffn",
    )(*args)


def _grid_pos_tables(n_tokens):
    quarter = D_MODEL // 4
    freq = 1.0 / (10000.0 ** (jnp.arange(quarter, dtype=F32) / quarter))
    er = jnp.arange(n_tokens // GRID_W, dtype=F32)[:, None] * freq
    ec = jnp.arange(GRID_W, dtype=F32)[:, None] * freq
    return (jnp.concatenate([jnp.sin(er), jnp.cos(er)], -1),
            jnp.concatenate([jnp.sin(ec), jnp.cos(ec)], -1))


def _split_w_in(w_in, b_gate):
    n_gate = 2 * N_DIR * N_HEADS
    w_main = jnp.concatenate([w_in[:, :4 * D_MLSTM], w_in[:, 4 * D_MLSTM + n_gate:]], axis=1).astype(BF16)
    gw = w_in[:, 4 * D_MLSTM:4 * D_MLSTM + n_gate].reshape(D_MODEL, N_DIR, 2, N_HEADS)
    gw = gw.transpose(0, 3, 1, 2).reshape(D_MODEL, n_gate)
    gw = jnp.pad(gw, ((0, 0), (0, GATE_LANES - n_gate))).astype(BF16)
    gb = b_gate.astype(F32).reshape(N_DIR, 2, N_HEADS).transpose(2, 0, 1).reshape(1, n_gate)
    gb = jnp.pad(gb, ((0, 0), (0, GATE_LANES - n_gate)))
    return w_main, gw, gb


def _layer(x2d, pos, mod6, weights, batch, seq_len, per_batch_mod, state, emit_state):
    (w_main, w_gate, b_gate, w_hnorm, w_br_m, w_br_f, w_out, ln1_g, ln1_b,
     w_up, w_conv, b_conv, w_down, ln2_g, ln2_b) = weights
    p_main, gates = _inproj(x2d, pos, mod6, w_main, w_gate, b_gate, seq_len, per_batch_mod)
    ml = _mlstm(p_main, gates, w_hnorm, batch, seq_len, state=state, emit_state=emit_state)
    fr = _fourier(p_main, batch, seq_len)
    x1 = _mix(ml[0], fr, p_main, x2d, pos, mod6, w_br_m, w_br_f, w_out, ln1_g, ln1_b,
              seq_len, per_batch_mod)
    x2 = _ffn(x1, mod6, w_up, w_conv, b_conv, w_down, ln2_g, ln2_b, seq_len, per_batch_mod)
    return x2, ml[1:]


def kernel(x_prompt, x_sample, c, state_C, state_n, state_m, c_ctx, w_ada, b_ada, w_in, b_gate,
           w_hnorm, w_br_m, w_br_f, w_out, ln1_g, ln1_b, w_up, w_conv, b_conv, w_down, ln2_g, ln2_b):
    assert w_ada.shape[0] == DEPTH
    B, S, _ = x_prompt.shape
    DB, DS, _ = x_sample.shape
    l = 0
    n_cond = 16
    cond = jnp.zeros((n_cond, D_MODEL), F32).at[0].set(c_ctx).at[1:1 + DB].set(c)
    mod6 = _modulation(cond, w_ada[l], b_ada[l]).reshape(n_cond, 6, D_MODEL)
    w_main, w_gate, b_gate_l = _split_w_in(w_in[l], b_gate[l])
    weights = (w_main, w_gate, b_gate_l, w_hnorm[l], w_br_m[l].astype(BF16), w_br_f[l].astype(BF16),
               w_out[l].astype(BF16), ln1_g[l], ln1_b[l], w_up[l].astype(BF16), w_conv[l], b_conv[l],
               w_down[l].astype(BF16), ln2_g[l], ln2_b[l])
    pos = _grid_pos_tables(DS)

    yp, states = _layer(x_prompt.reshape(B * S, D_MODEL), None, mod6[0:1], weights, B, S,
                        per_batch_mod=False, state=None, emit_state=True)
    ys, _ = _layer(x_sample.reshape(DB * DS, D_MODEL), pos, mod6[1:1 + DB], weights, DB, DS,
                   per_batch_mod=True, state=(state_C[:, l], state_n[:, l], state_m[:, l]),
                   emit_state=False)
    new_C, new_n, new_m = states
    new_n = new_n.reshape(B, DEPTH, N_DIR, N_HEADS, HEAD_DIM)
    new_m = new_m[:, :, :, 0, 0].reshape(B, DEPTH, N_DIR, N_HEADS)
    return (yp.reshape(B, S, D_MODEL), ys.reshape(DB, DS, D_MODEL), new_C, new_n, new_m)
```

```python
import functools

import jax
import jax.numpy as jnp
from jax import lax
from jax.experimental import pallas as pl
from jax.experimental.pallas import tpu as pltpu

D_MODEL = 2048
N_HEADS = 4
HEAD_DIM = 256
D_MLSTM = N_HEADS * HEAD_DIM
N_FGROUPS = 4
FGROUP_DIM = 256
D_FOURIER = N_FGROUPS * FGROUP_DIM
D_FF = 5632
GRID_W = 64
N_DIR = 2
DEPTH = 1
ALPHA = (2.0 * DEPTH) ** 0.25
LN_EPS = 1e-5

F32 = jnp.float32
BF16 = jnp.bfloat16

CHUNK = 256
COL_Q = 0
COL_K = COL_Q + D_MLSTM
COL_V = COL_K + D_MLSTM
COL_O = COL_V + D_MLSTM
COL_FR = COL_O + D_MLSTM
COL_GA = COL_FR + D_FOURIER
COL_GB = COL_GA + D_MODEL
D_MAIN = COL_GB + D_MODEL
GATE_BLOCK = 1024
GATE_LANES = 128
HALO = 8

VMEM_LIMIT = 56 * 1024 * 1024

INPROJ_TM = 1024
INPROJ_TN = 2304
MIX_TM = 512
MIX_SUB = 256
FFN_TM_LONG = 1024
FFN_TM_SHORT = 512
FFN_TF = 512


def _params(sem, flags=None):
    return pltpu.CompilerParams(dimension_semantics=sem, vmem_limit_bytes=VMEM_LIMIT, flags=flags)


def _resident(shape, index_map):
    return pl.BlockSpec(shape, index_map, pipeline_mode=pl.Buffered(1))


def _norm(x):
    mu = jnp.mean(x, axis=-1, keepdims=True)
    xc = x - mu
    var = jnp.mean(xc * xc, axis=-1, keepdims=True)
    return xc * lax.rsqrt(var + LN_EPS)


def _dot(a, b):
    return jnp.dot(a, b, preferred_element_type=F32)


def _dot_nt(a, b):
    return lax.dot_general(a, b, (((1,), (1,)), ((), ())), preferred_element_type=F32)


def _mod_kernel(c_ref, w_ref, b_ref, o_ref):
    c = c_ref[...]
    s = c * jax.nn.sigmoid(c)
    o_ref[...] = _dot(s.astype(BF16), w_ref[...].astype(BF16)) + b_ref[...]


def _modulation(cond, w_ada, b_ada):
    rows, tn = cond.shape[0], 1024
    n_out = w_ada.shape[1]
    return pl.pallas_call(
        _mod_kernel,
        grid=(n_out // tn,),
        in_specs=[
            _resident((rows, D_MODEL), lambda j: (0, 0)),
            pl.BlockSpec((D_MODEL, tn), lambda j: (0, j)),
            pl.BlockSpec((1, tn), lambda j: (0, j)),
        ],
        out_specs=pl.BlockSpec((rows, tn), lambda j: (0, j)),
        out_shape=jax.ShapeDtypeStruct((rows, n_out), F32),
        compiler_params=_params(("arbitrary",)),
        name="modulation",
    )(cond, w_ada, b_ada.reshape(1, n_out))


LN_ROWS = 256


def _add_grid_pos(x, pr_ref, pc_ref, tok0):
    half = D_MODEL // 2
    out = []
    for k in range(x.shape[0] // GRID_W):
        blk = x[k * GRID_W:(k + 1) * GRID_W, :]
        pr = pr_ref[pl.ds(tok0 // GRID_W + k, 1), :]
        out.append(jnp.concatenate([blk[:, :half] + pr, blk[:, half:] + pc_ref[...]], axis=1))
    return jnp.concatenate(out, axis=0)


def _inproj_kernel(*refs, tm, tiles_per_seq, has_pos):
    if has_pos:
        x_ref, pr_ref, pc_ref, mod_ref, w_ref, wg_ref, bg_ref, p_ref, g_ref, h_scr = refs
    else:
        x_ref, mod_ref, w_ref, wg_ref, bg_ref, p_ref, g_ref, h_scr = refs

    @pl.when(pl.program_id(1) == 0)
    def _():
        sh = mod_ref[0, 0:1, :]
        sc = mod_ref[0, 1:2, :]

        def ln_rows(r, carry):
            rows = pl.ds(pl.multiple_of(r * LN_ROWS, LN_ROWS), LN_ROWS)
            x = x_ref[rows, :]
            if has_pos:
                x = _add_grid_pos(x, pr_ref, pc_ref,
                                  (pl.program_id(0) % tiles_per_seq) * tm + r * LN_ROWS)
            h = (_norm(x) * (1.0 + sc) + sh).astype(BF16)
            h_scr[rows, :] = h
            g_ref[rows, :] = _dot(h, wg_ref[...]) + bg_ref[...]
            return carry

        lax.fori_loop(0, tm // LN_ROWS, ln_rows, 0)

    p_ref[...] = _dot(h_scr[...], w_ref[...]).astype(BF16)


def _inproj(x2d, pos, mod6, w_main, w_gate, b_gate, seq_len, per_batch_mod):
    n_tok = x2d.shape[0]
    tm, tn = INPROJ_TM, INPROJ_TN
    tiles_per_seq = max(seq_len // tm, 1)
    has_pos = pos is not None
    mod_idx = (lambda i, j: (i // tiles_per_seq, 0, 0)) if per_batch_mod else (lambda i, j: (0, 0, 0))
    in_specs = [pl.BlockSpec((tm, D_MODEL), lambda i, j: (i, 0))]
    args = [x2d]
    if has_pos:
        in_specs += [_resident(p.shape, lambda i, j: (0, 0)) for p in pos]
        args += list(pos)
    in_specs += [
        pl.BlockSpec((1, 6, D_MODEL), mod_idx),
        pl.BlockSpec((D_MODEL, tn), lambda i, j: (0, j)),
        _resident((D_MODEL, GATE_LANES), lambda i, j: (0, 0)),
        _resident((1, GATE_LANES), lambda i, j: (0, 0)),
    ]
    args += [mod6, w_main, w_gate, b_gate]
    return pl.pallas_call(
        functools.partial(_inproj_kernel, tm=tm, tiles_per_seq=tiles_per_seq, has_pos=has_pos),
        grid=(n_tok // tm, D_MAIN // tn),
        in_specs=in_specs,
        out_specs=[
            pl.BlockSpec((tm, tn), lambda i, j: (i, j)),
            pl.BlockSpec((tm, GATE_LANES), lambda i, j: (i, 0)),
        ],
        out_shape=[
            jax.ShapeDtypeStruct((n_tok, D_MAIN), BF16),
            jax.ShapeDtypeStruct((n_tok, GATE_LANES), F32),
        ],
        scratch_shapes=[pltpu.VMEM((tm, D_MODEL), BF16)],
        compiler_params=_params(("parallel", "arbitrary")),
        name="inproj",
    )(*args)


def _log_sigmoid(x):
    return jnp.minimum(x, 0.0) - jnp.log1p(jnp.exp(-jnp.abs(x)))


def _head_out(h, o, wn):
    return (jax.nn.sigmoid(o.astype(F32)) * (_norm(h) * wn)).astype(BF16)


EXT = HEAD_DIM + GATE_LANES
PASS1_GROUP = 4


def _lane_tile(x, n):
    return jnp.concatenate([x] * n, axis=1)


def _mlstm_kernel(*refs, seq_len, has_state, emit_state):
    refs = list(refs)
    q_ref, k_ref, v_ref, o_ref, g_ref, wn_ref = refs[:6]
    refs = refs[6:]
    if has_state:
        m0_ref, c0_ref, n0_ref = refs[:3]
        refs = refs[3:]
    hm_ref = refs[0]
    refs = refs[1:]
    if emit_state:
        cout_ref, nout_ref, mout_ref = refs[:3]
        refs = refs[3:]
    num_scr, row_scr, kv_scr, sc_scr, c_scr, h_scr = refs
    L = CHUNK
    nc = seq_len // L
    scale = HEAD_DIM ** -0.5
    head = pl.program_id(1)
    gate_shift = jnp.where(head == 0, 0, GATE_LANES - 2 * N_DIR * head)

    def chunk_rows(c):
        return pl.ds(pl.multiple_of(c * L, L), L)

    def load_q(rows):
        return (q_ref[rows, :].astype(F32) * scale).astype(BF16)

    t_idx = lax.broadcasted_iota(jnp.int32, (L, L), 0)
    s_idx = lax.broadcasted_iota(jnp.int32, (L, L), 1)
    hi_rows = lax.broadcasted_iota(jnp.int32, (16, L), 0) < 8

    masks = [s_idx <= t_idx, s_idx >= t_idx]
    masks_b = [jnp.where(mk, 1.0, 0.0).astype(BF16) for mk in masks]
    group = min(nc, PASS1_GROUP)

    def pass1(grp, carry):
        chunks = [grp * group + i for i in range(group)]
        base, items = [], []
        for c in chunks:
            rows = chunk_rows(c)
            qb, kb, vb = load_q(rows), k_ref[rows, :], v_ref[rows, :]
            v_ext = jnp.concatenate([vb, jnp.ones((L, GATE_LANES), BF16)], axis=1)
            s0 = _dot_nt(qb, kb)
            g_row = jnp.transpose(pltpu.roll(g_ref[rows, :], gate_shift, 1))
            h_scr[rows, :] = jnp.zeros((L, HEAD_DIM), F32)
            base.append((c, rows, kb, v_ext, s0))
            for d in range(N_DIR):
                ig_row = g_row[2 * d:2 * d + 1, :]
                lf = _log_sigmoid(g_row[2 * d + 1:2 * d + 2, :])
                lf_hi = lf.astype(BF16)
                lf_lo = (lf - lf_hi.astype(F32)).astype(BF16)
                lhs = jnp.where(hi_rows, lf_hi.astype(F32), lf_lo.astype(F32)).astype(BF16)
                r16 = _dot_nt(lhs, masks_b[d])
                lf_rep = jnp.concatenate([jnp.broadcast_to(lf_hi, (GATE_LANES, L)),
                                          jnp.broadcast_to(lf_lo, (GATE_LANES, L))], axis=0)
                bb = _dot_nt(masks_b[d], lf_rep)
                items.append((len(base) - 1, d, ig_row, lf, r16, bb))
        states = []
        for bi, d, ig_row, lf, r16, bb in items:
            c, rows, kb, v_ext, s0 = base[bi]
            b_row = r16[0:1, :] + r16[8:9, :]
            b_rep = bb[:, :GATE_LANES] + bb[:, GATE_LANES:]
            c_row = ig_row - b_row
            c_max = jnp.broadcast_to(
                jnp.max(jnp.where(masks[d], c_row, -jnp.inf), axis=1, keepdims=True), (L, L))
            w = jnp.exp(jnp.where(masks[d], c_row - c_max, -jnp.inf))
            pv = _dot((s0 * w).astype(BF16), v_ext)
            num_scr[d, rows, :] = pv[:, :HEAD_DIM]
            row_scr[d, 0, rows, :] = pv[:, HEAD_DIM:]
            row_scr[d, 1, rows, :] = b_rep + c_max[:, :GATE_LANES]
            row_scr[d, 2, rows, :] = b_rep
            states.append((bi, d, ig_row, lf, b_row))
        k_ts = [jnp.transpose(kb.astype(F32)) for (_, _, kb, _, _) in base]
        for bi, d, ig_row, lf, b_row in states:
            c, rows, kb, v_ext, s0 = base[bi]
            b_last = jnp.sum(lf, axis=1, keepdims=True)
            g = b_last - b_row + ig_row
            g_max = jnp.max(g, axis=1, keepdims=True)
            wk = jnp.exp(g - g_max)
            kv_scr[d, c] = _dot((k_ts[bi] * wk).astype(BF16), v_ext)
            sc_scr[d, c, 0] = jnp.broadcast_to(b_last, (8, GATE_LANES))
            sc_scr[d, c, 1] = jnp.broadcast_to(g_max, (8, GATE_LANES))
        return carry

    lax.fori_loop(0, nc // group, pass1, 0)

    ms = []
    for d in range(N_DIR):
        if has_state:
            bh = pl.program_id(0) * (N_DIR * N_HEADS) + d * N_HEADS + head
            n_rep = jnp.transpose(jnp.broadcast_to(n0_ref[0, d, 0], (GATE_LANES, HEAD_DIM)))
            c_scr[d] = jnp.concatenate([c0_ref[0, d, 0], n_rep], axis=1)
            ms.append(jnp.full((1, GATE_LANES), m0_ref[bh], F32))
        else:
            c_scr[d] = jnp.zeros((HEAD_DIM, EXT), F32)
            ms.append(jnp.zeros((1, GATE_LANES), F32))

    def pass2(j, carry):
        new = []
        for d, m in enumerate(carry):
            c = j if d == 0 else nc - 1 - j
            rows = chunk_rows(c)
            c_ext = c_scr[d]
            qcn = _dot(load_q(rows), c_ext.astype(BF16))
            den_i, a_rep, b_rep = row_scr[d, 0, rows, :], row_scr[d, 1, rows, :], row_scr[d, 2, rows, :]
            m_rep = jnp.maximum(b_rep + m, a_rep)
            r_intra = jnp.exp(a_rep - m_rep)
            r_state = jnp.exp(b_rep + m - m_rep)
            den = r_intra * den_i + r_state * qcn[:, HEAD_DIM:]
            inv = 1.0 / jnp.maximum(jnp.abs(den), jnp.exp(-m_rep))
            h = (_lane_tile(r_intra * inv, 2) * num_scr[d, rows, :]
                 + _lane_tile(r_state * inv, 2) * qcn[:, :HEAD_DIM])
            h_scr[rows, :] += h
            b_last, g_max = sc_scr[d, c, 0][0:1, :], sc_scr[d, c, 1][0:1, :]
            m_new = jnp.maximum(b_last + m, g_max)
            decay = jnp.exp(b_last + m - m_new)
            gain = jnp.exp(g_max - m_new)
            c_scr[d] = _lane_tile(decay, 3) * c_ext + _lane_tile(gain, 3) * kv_scr[d, c]
            new.append(m_new)
        return tuple(new)

    ms = lax.fori_loop(0, nc, pass2, tuple(ms))

    wn = wn_ref[0]

    def finish(c, carry):
        rows = chunk_rows(c)
        hm_ref[rows, :] = _head_out(h_scr[rows, :], o_ref[rows, :], wn)
        return carry

    lax.fori_loop(0, nc, finish, 0)
    if emit_state:
        for d in range(N_DIR):
            c_ext = c_scr[d]
            cout_ref[0, 0, d, 0] = c_ext[:, :HEAD_DIM]
            nout_ref[0, 0, d, 0] = jnp.transpose(c_ext[:, HEAD_DIM:])[0:1, :]
            mout_ref[0, d, 0] = ms[d]


def _mlstm(p_main, gates, w_hnorm, batch, seq_len, state=None, emit_state=False):
    n_tok = p_main.shape[0]
    T = seq_len
    blk = lambda col: pl.BlockSpec((T, HEAD_DIM), lambda b, h: (b, col // HEAD_DIM + h))
    in_specs = [blk(COL_Q), blk(COL_K), blk(COL_V), blk(COL_O),
                pl.BlockSpec((T, GATE_LANES), lambda b, h: (b, 0)),
                pl.BlockSpec((1, 1, HEAD_DIM), lambda b, h: (h, 0, 0))]
    args = [p_main, p_main, p_main, p_main, gates, w_hnorm.reshape(N_HEADS, 1, HEAD_DIM)]
    has_state = state is not None
    if has_state:
        C0, n0, m0 = state
        in_specs += [
            pl.BlockSpec(memory_space=pltpu.SMEM),
            pl.BlockSpec((1, N_DIR, 1, HEAD_DIM, HEAD_DIM), lambda b, h: (b, 0, h, 0, 0)),
            pl.BlockSpec((1, N_DIR, 1, 1, HEAD_DIM), lambda b, h: (b, 0, h, 0, 0)),
        ]
        args += [m0.reshape(-1), C0, n0.reshape(batch, N_DIR, N_HEADS, 1, HEAD_DIM)]
    out_specs = [pl.BlockSpec((T, HEAD_DIM), lambda b, h: (b, h))]
    out_shape = [jax.ShapeDtypeStruct((n_tok, D_MLSTM), BF16)]
    if emit_state:
        out_specs += [
            pl.BlockSpec((1, 1, N_DIR, 1, HEAD_DIM, HEAD_DIM), lambda b, h: (b, 0, 0, h, 0, 0)),
            pl.BlockSpec((1, 1, N_DIR, 1, 1, HEAD_DIM), lambda b, h: (b, 0, 0, h, 0, 0)),
            pl.BlockSpec((1, N_DIR, 1, 1, GATE_LANES), lambda b, h: (b, 0, h, 0, 0)),
        ]
        out_shape += [
            jax.ShapeDtypeStruct((batch, DEPTH, N_DIR, N_HEADS, HEAD_DIM, HEAD_DIM), F32),
            jax.ShapeDtypeStruct((batch, DEPTH, N_DIR, N_HEADS, 1, HEAD_DIM), F32),
            jax.ShapeDtypeStruct((batch, N_DIR, N_HEADS, 1, GATE_LANES), F32),
        ]
    nc = T // CHUNK
    scratch = [pltpu.VMEM((N_DIR, T, HEAD_DIM), F32),
               pltpu.VMEM((N_DIR, 3, T, GATE_LANES), F32),
               pltpu.VMEM((N_DIR, nc, HEAD_DIM, EXT), F32),
               pltpu.VMEM((N_DIR, nc, 2, 8, GATE_LANES), F32),
               pltpu.VMEM((N_DIR, HEAD_DIM, EXT), F32),
               pltpu.VMEM((T, HEAD_DIM), F32)]
    return pl.pallas_call(
        functools.partial(_mlstm_kernel, seq_len=T, has_state=has_state, emit_state=emit_state),
        grid=(batch, N_HEADS),
        in_specs=in_specs,
        out_specs=out_specs,
        out_shape=out_shape,
        scratch_shapes=scratch,
        compiler_params=_params(("parallel", "parallel")),
        name="mlstm",
    )(*args)


def _fourier_kernel(x_ref, cs_ref, ct_ref, st_ref, o_ref, *, scale):
    for g in range(N_FGROUPS):
        cols = slice(g * FGROUP_DIM, (g + 1) * FGROUP_DIM)
        z = _dot(x_ref[:, cols], cs_ref[...])
        zc = z[:, :FGROUP_DIM].astype(BF16)
        zs = z[:, FGROUP_DIM:].astype(BF16)
        y = _dot(ct_ref[...], zc) - _dot(st_ref[...], zs)
        o_ref[:, cols] = (y * scale).astype(BF16)


def _dft_tables(n):
    k = jnp.arange(n, dtype=jnp.int32)
    ang = ((k[:, None] * k[None, :]) % n).astype(F32) * (2.0 * jnp.pi / n)
    return jnp.cos(ang), jnp.sin(ang)


DFT_BLOCK = 256
DFT_RADIX = 8
DFT_GROUPS = 2
HALF_LANES = 128


def _cadd(a, b):
    return a[0] + b[0], a[1] + b[1]


def _csub(a, b):
    return a[0] - b[0], a[1] - b[1]


def _mul_neg_i(a):
    return a[1], -a[0]


def _dft4(y):
    c0, c1 = _cadd(y[0], y[2]), _cadd(y[1], y[3])
    d0, d1 = _csub(y[0], y[2]), _mul_neg_i(_csub(y[1], y[3]))
    return [_cadd(c0, c1), _cadd(d0, d1), _csub(c0, c1), _csub(d0, d1)]


def _dft8(x):
    r = 0.5 ** 0.5
    a = [_cadd(x[n], x[n + 4]) for n in range(4)]
    b = [_csub(x[n], x[n + 4]) for n in range(4)]
    b[1] = ((b[1][0] + b[1][1]) * r, (b[1][1] - b[1][0]) * r)
    b[2] = _mul_neg_i(b[2])
    b[3] = ((b[3][1] - b[3][0]) * r, -(b[3][0] + b[3][1]) * r)
    even, odd = _dft4(a), _dft4(b)
    return [even[k // 2] if k % 2 == 0 else odd[k // 2] for k in range(8)]


def _fourier_long_kernel(x_ref, cs_ref, twc_ref, tws_ref, o_ref, w_scr, *, scale):
    nb = DFT_BLOCK
    cs = cs_ref[...]
    c_tab, s_tab = cs[:, :nb], cs[:, nb:]
    n_groups = x_ref.shape[1] // FGROUP_DIM
    zs = []
    for g in range(n_groups):
        cols = slice(g * FGROUP_DIM, (g + 1) * FGROUP_DIM)
        z = []
        for t1 in range(DFT_RADIX):
            zz = _dot(x_ref[t1 * nb:(t1 + 1) * nb, cols], cs)
            z.append((zz[:, :nb], -zz[:, nb:]))
        zs.append(z)
    for g, z in enumerate(zs):
        a = _dft8(z)
        for u1 in range(DFT_RADIX):
            ar, ai = a[u1]
            if u1 > 0:
                twc = twc_ref[u1 * nb:(u1 + 1) * nb, :]
                tws = tws_ref[u1 * nb:(u1 + 1) * nb, :]
                ar, ai = ar * twc + ai * tws, ai * twc - ar * tws
            y = (_dot(c_tab, ar.astype(BF16)) + _dot(s_tab, ai.astype(BF16))) * scale
            for half in range(FGROUP_DIM // HALF_LANES):
                w_scr[g, half, pl.ds(u1, nb, stride=DFT_RADIX), :] = (
                    y[:, half * HALF_LANES:(half + 1) * HALF_LANES])
    for g in range(n_groups):
        for half in range(FGROUP_DIM // HALF_LANES):
            lanes = slice(g * FGROUP_DIM + half * HALF_LANES, g * FGROUP_DIM + (half + 1) * HALF_LANES)
            o_ref[:, lanes] = w_scr[g, half].astype(BF16)


def _fourier_long(p_main, batch, seq_len):
    n_tok = p_main.shape[0]
    T = seq_len
    assert T == DFT_RADIX * DFT_BLOCK and FGROUP_DIM == DFT_BLOCK
    cc, sc = _dft_tables(FGROUP_DIM)
    cs = jnp.concatenate([cc, sc], axis=1).astype(BF16)
    u1 = jnp.repeat(jnp.arange(DFT_RADIX, dtype=jnp.int32), DFT_BLOCK)
    t2 = jnp.tile(jnp.arange(DFT_BLOCK, dtype=jnp.int32), DFT_RADIX)
    ang = ((u1 * t2) % T).astype(F32) * (2.0 * jnp.pi / T)
    twc = jnp.broadcast_to(jnp.cos(ang)[:, None], (T, FGROUP_DIM))
    tws = jnp.broadcast_to(jnp.sin(ang)[:, None], (T, FGROUP_DIM))
    gw = DFT_GROUPS * FGROUP_DIM
    return pl.pallas_call(
        functools.partial(_fourier_long_kernel, scale=float((T * FGROUP_DIM) ** -0.5)),
        grid=(batch, N_FGROUPS // DFT_GROUPS),
        in_specs=[
            pl.BlockSpec((T, gw), lambda b, g: (b, COL_FR // gw + g)),
            _resident((FGROUP_DIM, 2 * FGROUP_DIM), lambda b, g: (0, 0)),
            _resident((T, FGROUP_DIM), lambda b, g: (0, 0)),
            _resident((T, FGROUP_DIM), lambda b, g: (0, 0)),
        ],
        out_specs=pl.BlockSpec((T, gw), lambda b, g: (b, g)),
        out_shape=jax.ShapeDtypeStruct((n_tok, D_FOURIER), BF16),
        scratch_shapes=[pltpu.VMEM((DFT_GROUPS, FGROUP_DIM // HALF_LANES, T, HALF_LANES), F32)],
        compiler_params=_params(("parallel", "parallel")),
        name="fourier_long",
    )(p_main, cs, twc, tws)


def _fourier(p_main, batch, seq_len):
    if seq_len == DFT_RADIX * DFT_BLOCK:
        return _fourier_long(p_main, batch, seq_len)
    n_tok = p_main.shape[0]
    T = seq_len
    cc, sc = _dft_tables(FGROUP_DIM)
    cs = jnp.concatenate([cc, sc], axis=1).astype(BF16)
    ct, st = _dft_tables(T)
    return pl.pallas_call(
        functools.partial(_fourier_kernel, scale=float((T * FGROUP_DIM) ** -0.5)),
        grid=(batch,),
        in_specs=[
            pl.BlockSpec((T, D_FOURIER), lambda b: (b, COL_FR // D_FOURIER)),
            _resident((FGROUP_DIM, 2 * FGROUP_DIM), lambda b: (0, 0)),
            _resident((T, T), lambda b: (0, 0)),
            _resident((T, T), lambda b: (0, 0)),
        ],
        out_specs=pl.BlockSpec((T, D_FOURIER), lambda b: (b, 0)),
        out_shape=jax.ShapeDtypeStruct((n_tok, D_FOURIER), BF16),
        compiler_params=_params(("parallel",)),
        name="fourier",
    )(p_main, cs, ct.astype(BF16), st.astype(BF16))


def _mix_kernel(*refs, tm, tiles_per_seq, has_pos):
    if has_pos:
        (hm_ref, fr_ref, ga0_ref, ga1_ref, gb0_ref, gb1_ref, x_ref, pr_ref, pc_ref, mod_ref,
         wm_ref, wf_ref, wo_ref, lg_ref, lb_ref, o_ref) = refs
    else:
        (hm_ref, fr_ref, ga0_ref, ga1_ref, gb0_ref, gb1_ref, x_ref, mod_ref,
         wm_ref, wf_ref, wo_ref, lg_ref, lb_ref, o_ref) = refs
    g1 = mod_ref[0, 2:3, :]
    for r in range(tm // MIX_SUB):
        rows = slice(r * MIX_SUB, (r + 1) * MIX_SUB)
        a = _dot(hm_ref[rows, :], wm_ref[...])
        b = _dot(fr_ref[rows, :], wf_ref[...])
        gate = lambda g0, g1_: jax.nn.sigmoid(jnp.concatenate([g0[rows, :], g1_[rows, :]], axis=1).astype(F32))
        mixed = gate(ga0_ref, ga1_ref) * a + gate(gb0_ref, gb1_ref) * b
        z = _dot(mixed.astype(BF16), wo_ref[...])
        x = x_ref[rows, :]
        if has_pos:
            x = _add_grid_pos(x, pr_ref, pc_ref, (pl.program_id(0) % tiles_per_seq) * tm + r * MIX_SUB)
        o_ref[rows, :] = _norm(ALPHA * x + g1 * z) * lg_ref[...] + lb_ref[...]


def _mix(hm, fr, p_main, x2d, pos, mod6, w_br_m, w_br_f, w_out, ln_g, ln_b, seq_len, per_batch_mod):
    n_tok = x2d.shape[0]
    tm = MIX_TM
    tiles_per_seq = max(seq_len // tm, 1)
    has_pos = pos is not None
    mod_idx = (lambda i: (i // tiles_per_seq, 0, 0)) if per_batch_mod else (lambda i: (0, 0, 0))
    in_specs = [
        pl.BlockSpec((tm, D_MLSTM), lambda i: (i, 0)),
        pl.BlockSpec((tm, D_FOURIER), lambda i: (i, 0)),
        pl.BlockSpec((tm, GATE_BLOCK), lambda i: (i, COL_GA // GATE_BLOCK)),
        pl.BlockSpec((tm, GATE_BLOCK), lambda i: (i, COL_GA // GATE_BLOCK + 1)),
        pl.BlockSpec((tm, GATE_BLOCK), lambda i: (i, COL_GB // GATE_BLOCK)),
        pl.BlockSpec((tm, GATE_BLOCK), lambda i: (i, COL_GB // GATE_BLOCK + 1)),
        pl.BlockSpec((tm, D_MODEL), lambda i: (i, 0)),
    ]
    args = [hm, fr, p_main, p_main, p_main, p_main, x2d]
    if has_pos:
        in_specs += [_resident(p.shape, lambda i: (0, 0)) for p in pos]
        args += list(pos)
    in_specs += [
        pl.BlockSpec((1, 6, D_MODEL), mod_idx),
        _resident((D_MLSTM, D_MODEL), lambda i: (0, 0)),
        _resident((D_FOURIER, D_MODEL), lambda i: (0, 0)),
        _resident((D_MODEL, D_MODEL), lambda i: (0, 0)),
        _resident((1, D_MODEL), lambda i: (0, 0)),
        _resident((1, D_MODEL), lambda i: (0, 0)),
    ]
    args += [mod6, w_br_m, w_br_f, w_out, ln_g.reshape(1, D_MODEL), ln_b.reshape(1, D_MODEL)]
    return pl.pallas_call(
        functools.partial(_mix_kernel, tm=tm, tiles_per_seq=tiles_per_seq, has_pos=has_pos),
        grid=(n_tok // tm,),
        in_specs=in_specs,
        out_specs=pl.BlockSpec((tm, D_MODEL), lambda i: (i, 0)),
        out_shape=jax.ShapeDtypeStruct((n_tok, D_MODEL), F32),
        compiler_params=_params(("parallel",)),
        name="mix",
    )(*args)


def _ffn_kernel(*refs, tm, seg_len, has_halo, tiles_per_seq):
    if has_halo:
        (x_ref, xp_ref, xn_ref, mod_ref, wv_ref, wg_ref, cwv_ref, cwg_ref, cbv_ref, cbg_ref,
         wd_ref, lg_ref, lb_ref, o_ref, h_scr) = refs
        acc_scr = o_ref
    else:
        (x_ref, mod_ref, wv_ref, wg_ref, cwv_ref, cwg_ref, cbv_ref, cbg_ref,
         wd_ref, lg_ref, lb_ref, o_ref, h_scr, acc_scr) = refs
    f = pl.program_id(1)
    n_seg = tm // seg_len
    stride = seg_len + 2 * HALO
    rows = n_seg * stride
    zeros = jnp.zeros((HALO, D_MODEL), BF16)

    @pl.when(f == 0)
    def _():
        sh = mod_ref[0, 3:4, :]
        sc = mod_ref[0, 4:5, :]
        modulate = lambda x: _norm(x) * (1.0 + sc) + sh
        for s in range(n_seg):
            h_scr[s * stride + HALO:(s + 1) * stride - HALO, :] = (
                modulate(x_ref[s * seg_len:(s + 1) * seg_len, :]).astype(BF16))
            h_scr[s * stride:s * stride + HALO, :] = zeros
            h_scr[(s + 1) * stride - HALO:(s + 1) * stride, :] = zeros
        if has_halo:
            t = pl.program_id(0) % tiles_per_seq
            hp = jnp.where(t == 0, 0.0, modulate(xp_ref[...]))
            hn = jnp.where(t == tiles_per_seq - 1, 0.0, modulate(xn_ref[...]))
            h_scr[0:HALO, :] = hp.astype(BF16)
            h_scr[rows - HALO:rows, :] = hn.astype(BF16)
        acc_scr[...] = jnp.zeros_like(acc_scr)

    def conv(u, cw_ref, cb_ref):
        prev = pltpu.roll(u, 1, 0)
        nxt = pltpu.roll(u, rows - 1, 0)
        y = prev * cw_ref[0:1, :] + u * cw_ref[1:2, :] + nxt * cw_ref[2:3, :] + cb_ref[...]
        return y[HALO:rows - HALO, :]

    h = h_scr[...]
    gate = conv(_dot(h, wg_ref[...]), cwg_ref, cbg_ref)
    gate = gate * jax.nn.sigmoid(gate)
    val = conv(_dot(h, wv_ref[...]), cwv_ref, cbv_ref)
    act = (gate * val).astype(BF16)
    acc_scr[...] += _dot(act, wd_ref[...])

    @pl.when(f == pl.num_programs(1) - 1)
    def _():
        g2 = mod_ref[0, 5:6, :]
        for s in range(n_seg):
            x = x_ref[s * seg_len:(s + 1) * seg_len, :]
            y = acc_scr[s * stride:s * stride + seg_len, :]
            o_ref[s * seg_len:(s + 1) * seg_len, :] = _norm(ALPHA * x + g2 * y) * lg_ref[...] + lb_ref[...]


def _ffn(x1, mod6, w_up, w_conv, b_conv, w_down, ln_g, ln_b, seq_len, per_batch_mod):
    n_tok = x1.shape[0]
    tf = FFN_TF
    nf = D_FF // tf
    tm = FFN_TM_LONG if seq_len >= FFN_TM_LONG else FFN_TM_SHORT
    seg_len = min(seq_len, tm)
    tiles_per_seq = max(seq_len // tm, 1)
    has_halo = seq_len > tm
    h_rows = (tm // seg_len) * (seg_len + 2 * HALO)
    mod_idx = (lambda i, f: (i // tiles_per_seq, 0, 0)) if per_batch_mod else (lambda i, f: (0, 0, 0))
    hb = tm // HALO
    n_hblk = n_tok // HALO
    in_specs = [pl.BlockSpec((tm, D_MODEL), lambda i, f: (i, 0))]
    args = [x1]
    scratch = [pltpu.VMEM((h_rows, D_MODEL), BF16)]
    if has_halo:
        in_specs = [_resident((tm, D_MODEL), lambda i, f: (i, 0))]
        in_specs += [
            pl.BlockSpec((HALO, D_MODEL), lambda i, f: (jnp.maximum(i * hb - 1, 0), 0)),
            pl.BlockSpec((HALO, D_MODEL), lambda i, f: (jnp.minimum((i + 1) * hb, n_hblk - 1), 0)),
        ]
        args += [x1, x1]
    else:
        scratch.append(pltpu.VMEM((h_rows - 2 * HALO, D_MODEL), F32))
    in_specs += [
        pl.BlockSpec((1, 6, D_MODEL), mod_idx),
        pl.BlockSpec((D_MODEL, tf), lambda i, f: (0, f)),
        pl.BlockSpec((D_MODEL, tf), lambda i, f: (0, nf + f)),
        pl.BlockSpec((3, tf), lambda i, f: (0, f)),
        pl.BlockSpec((3, tf), lambda i, f: (0, nf + f)),
        pl.BlockSpec((1, tf), lambda i, f: (0, f)),
        pl.BlockSpec((1, tf), lambda i, f: (0, nf + f)),
        pl.BlockSpec((tf, D_MODEL), lambda i, f: (f, 0)),
        _resident((1, D_MODEL), lambda i, f: (0, 0)),
        _resident((1, D_MODEL), lambda i, f: (0, 0)),
    ]
    b_conv2 = b_conv.reshape(1, 2 * D_FF)
    args += [mod6, w_up, w_up, w_conv, w_conv, b_conv2, b_conv2, w_down,
             ln_g.reshape(1, D_MODEL), ln_b.reshape(1, D_MODEL)]
    return pl.pallas_call(
        functools.partial(_ffn_kernel, tm=tm, seg_len=seg_len, has_halo=has_halo,
                          tiles_per_seq=tiles_per_seq),
        grid=(n_tok // tm, nf),
        in_specs=in_specs,
        out_specs=pl.BlockSpec((tm, D_MODEL), lambda i, f: (i, 0)),
        out_shape=jax.ShapeDtypeStruct((n_tok, D_MODEL), F32),
        scratch_shapes=scratch,
        compiler_params=_params(("parallel", "arbitrary")),
        name="ffn",
    )(*args)


def _grid_pos_tables(n_tokens):
    quarter = D_MODEL // 4
    freq = 1.0 / (10000.0 ** (jnp.arange(quarter, dtype=F32) / quarter))
    er = jnp.arange(n_tokens // GRID_W, dtype=F32)[:, None] * freq
    ec = jnp.arange(GRID_W, dtype=F32)[:, None] * freq
    return (jnp.concatenate([jnp.sin(er), jnp.cos(er)], -1),
            jnp.concatenate([jnp.sin(ec), jnp.cos(ec)], -1))


def _split_w_in(w_in, b_gate):
    n_gate = 2 * N_DIR * N_HEADS
    w_main = jnp.concatenate([w_in[:, :4 * D_MLSTM], w_in[:, 4 * D_MLSTM + n_gate:]], axis=1).astype(BF16)
    gw = w_in[:, 4 * D_MLSTM:4 * D_MLSTM + n_gate].reshape(D_MODEL, N_DIR, 2, N_HEADS)
    gw = gw.transpose(0, 3, 1, 2).reshape(D_MODEL, n_gate)
    gw = jnp.pad(gw, ((0, 0), (0, GATE_LANES - n_gate))).astype(BF16)
    gb = b_gate.astype(F32).reshape(N_DIR, 2, N_HEADS).transpose(2, 0, 1).reshape(1, n_gate)
    gb = jnp.pad(gb, ((0, 0), (0, GATE_LANES - n_gate)))
    return w_main, gw, gb


def _layer(x2d, pos, mod6, weights, batch, seq_len, per_batch_mod, state, emit_state):
    (w_main, w_gate, b_gate, w_hnorm, w_br_m, w_br_f, w_out, ln1_g, ln1_b,
     w_up, w_conv, b_conv, w_down, ln2_g, ln2_b) = weights
    p_main, gates = _inproj(x2d, pos, mod6, w_main, w_gate, b_gate, seq_len, per_batch_mod)
    ml = _mlstm(p_main, gates, w_hnorm, batch, seq_len, state=state, emit_state=emit_state)
    fr = _fourier(p_main, batch, seq_len)
    x1 = _mix(ml[0], fr, p_main, x2d, pos, mod6, w_br_m, w_br_f, w_out, ln1_g, ln1_b,
              seq_len, per_batch_mod)
    x2 = _ffn(x1, mod6, w_up, w_conv, b_conv, w_down, ln2_g, ln2_b, seq_len, per_batch_mod)
    return x2, ml[1:]


def kernel(x_prompt, x_sample, c, state_C, state_n, state_m, c_ctx, w_ada, b_ada, w_in, b_gate,
           w_hnorm, w_br_m, w_br_f, w_out, ln1_g, ln1_b, w_up, w_conv, b_conv, w_down, ln2_g, ln2_b):
    assert w_ada.shape[0] == DEPTH
    B, S, _ = x_prompt.shape
    DB, DS, _ = x_sample.shape
    l = 0
    n_cond = 16
    cond = jnp.zeros((n_cond, D_MODEL), F32).at[0].set(c_ctx).at[1:1 + DB].set(c)
    mod6 = _modulation(cond, w_ada[l], b_ada[l]).reshape(n_cond, 6, D_MODEL)
    w_main, w_gate, b_gate_l = _split_w_in(w_in[l], b_gate[l])
    weights = (w_main, w_gate, b_gate_l, w_hnorm[l], w_br_m[l].astype(BF16), w_br_f[l].astype(BF16),
               w_out[l].astype(BF16), ln1_g[l], ln1_b[l], w_up[l].astype(BF16), w_conv[l], b_conv[l],
               w_down[l].astype(BF16), ln2_g[l], ln2_b[l])
    pos = _grid_pos_tables(DS)

    yp, states = _layer(x_prompt.reshape(B * S, D_MODEL), None, mod6[0:1], weights, B, S,
                        per_batch_mod=False, state=None, emit_state=True)
    ys, _ = _layer(x_sample.reshape(DB * DS, D_MODEL), pos, mod6[1:1 + DB], weights, DB, DS,
                   per_batch_mod=True, state=(state_C[:, l], state_n[:, l], state_m[:, l]),
                   emit_state=False)
    new_C, new_n, new_m = states
    new_n = new_n.reshape(B, DEPTH, N_DIR, N_HEADS, HEAD_DIM)
    new_m = new_m[:, :, :, 0, 0].reshape(B, DEPTH, N_DIR, N_HEADS)
    return (yp.reshape(B, S, D_MODEL), ys.reshape(DB, DS, D_MODEL), new_C, new_n, new_m)
```

```python
import functools

import jax
import jax.numpy as jnp
from jax import lax
from jax.experimental import pallas as pl
from jax.experimental.pallas import tpu as pltpu

D_MODEL = 2048
N_HEADS = 4
HEAD_DIM = 256
D_MLSTM = N_HEADS * HEAD_DIM
N_FGROUPS = 4
FGROUP_DIM = 256
D_FOURIER = N_FGROUPS * FGROUP_DIM
D_FF = 5632
GRID_W = 64
N_DIR = 2
DEPTH = 1
ALPHA = (2.0 * DEPTH) ** 0.25
LN_EPS = 1e-5

F32 = jnp.float32
BF16 = jnp.bfloat16

CHUNK = 256
COL_Q = 0
COL_K = COL_Q + D_MLSTM
COL_V = COL_K + D_MLSTM
COL_O = COL_V + D_MLSTM
COL_FR = COL_O + D_MLSTM
COL_GA = COL_FR + D_FOURIER
COL_GB = COL_GA + D_MODEL
D_MAIN = COL_GB + D_MODEL
GATE_BLOCK = 1024
GATE_LANES = 128
HALO = 8

VMEM_LIMIT = 56 * 1024 * 1024

INPROJ_TM = 1024
INPROJ_TN = 2304
MIX_TM = 512
MIX_SUB = 256
FFN_TM_LONG = 1024
FFN_TM_SHORT = 512
FFN_TF = 512


def _params(sem, flags=None):
    return pltpu.CompilerParams(dimension_semantics=sem, vmem_limit_bytes=VMEM_LIMIT, flags=flags)


def _resident(shape, index_map):
    return pl.BlockSpec(shape, index_map, pipeline_mode=pl.Buffered(1))


def _norm(x):
    mu = jnp.mean(x, axis=-1, keepdims=True)
    xc = x - mu
    var = jnp.mean(xc * xc, axis=-1, keepdims=True)
    return xc * lax.rsqrt(var + LN_EPS)


def _dot(a, b):
    return jnp.dot(a, b, preferred_element_type=F32)


def _dot_nt(a, b):
    return lax.dot_general(a, b, (((1,), (1,)), ((), ())), preferred_element_type=F32)


def _mod_kernel(c_ref, w_ref, b_ref, o_ref):
    c = c_ref[...]
    s = c * jax.nn.sigmoid(c)
    o_ref[...] = _dot(s.astype(BF16), w_ref[...].astype(BF16)) + b_ref[...]


def _modulation(cond, w_ada, b_ada):
    rows, tn = cond.shape[0], 1024
    n_out = w_ada.shape[1]
    return pl.pallas_call(
        _mod_kernel,
        grid=(n_out // tn,),
        in_specs=[
            _resident((rows, D_MODEL), lambda j: (0, 0)),
            pl.BlockSpec((D_MODEL, tn), lambda j: (0, j)),
            pl.BlockSpec((1, tn), lambda j: (0, j)),
        ],
        out_specs=pl.BlockSpec((rows, tn), lambda j: (0, j)),
        out_shape=jax.ShapeDtypeStruct((rows, n_out), F32),
        compiler_params=_params(("arbitrary",)),
        name="modulation",
    )(cond, w_ada, b_ada.reshape(1, n_out))


LN_ROWS = 256


def _add_grid_pos(x, pr_ref, pc_ref, tok0):
    half = D_MODEL // 2
    out = []
    for k in range(x.shape[0] // GRID_W):
        blk = x[k * GRID_W:(k + 1) * GRID_W, :]
        pr = pr_ref[pl.ds(tok0 // GRID_W + k, 1), :]
        out.append(jnp.concatenate([blk[:, :half] + pr, blk[:, half:] + pc_ref[...]], axis=1))
    return jnp.concatenate(out, axis=0)


def _inproj_kernel(*refs, tm, tiles_per_seq, has_pos):
    if has_pos:
        x_ref, pr_ref, pc_ref, mod_ref, w_ref, wg_ref, bg_ref, p_ref, g_ref, h_scr = refs
    else:
        x_ref, mod_ref, w_ref, wg_ref, bg_ref, p_ref, g_ref, h_scr = refs

    @pl.when(pl.program_id(1) == 0)
    def _():
        sh = mod_ref[0, 0:1, :]
        sc = mod_ref[0, 1:2, :]

        def ln_rows(r, carry):
            rows = pl.ds(pl.multiple_of(r * LN_ROWS, LN_ROWS), LN_ROWS)
            x = x_ref[rows, :]
            if has_pos:
                x = _add_grid_pos(x, pr_ref, pc_ref,
                                  (pl.program_id(0) % tiles_per_seq) * tm + r * LN_ROWS)
            h = (_norm(x) * (1.0 + sc) + sh).astype(BF16)
            h_scr[rows, :] = h
            g_ref[rows, :] = _dot(h, wg_ref[...]) + bg_ref[...]
            return carry

        lax.fori_loop(0, tm // LN_ROWS, ln_rows, 0)

    p_ref[...] = _dot(h_scr[...], w_ref[...]).astype(BF16)


def _inproj(x2d, pos, mod6, w_main, w_gate, b_gate, seq_len, per_batch_mod):
    n_tok = x2d.shape[0]
    tm, tn = INPROJ_TM, INPROJ_TN
    tiles_per_seq = max(seq_len // tm, 1)
    has_pos = pos is not None
    mod_idx = (lambda i, j: (i // tiles_per_seq, 0, 0)) if per_batch_mod else (lambda i, j: (0, 0, 0))
    in_specs = [pl.BlockSpec((tm, D_MODEL), lambda i, j: (i, 0))]
    args = [x2d]
    if has_pos:
        in_specs += [_resident(p.shape, lambda i, j: (0, 0)) for p in pos]
        args += list(pos)
    in_specs += [
        pl.BlockSpec((1, 6, D_MODEL), mod_idx),
        pl.BlockSpec((D_MODEL, tn), lambda i, j: (0, j)),
        _resident((D_MODEL, GATE_LANES), lambda i, j: (0, 0)),
        _resident((1, GATE_LANES), lambda i, j: (0, 0)),
    ]
    args += [mod6, w_main, w_gate, b_gate]
    return pl.pallas_call(
        functools.partial(_inproj_kernel, tm=tm, tiles_per_seq=tiles_per_seq, has_pos=has_pos),
        grid=(n_tok // tm, D_MAIN // tn),
        in_specs=in_specs,
        out_specs=[
            pl.BlockSpec((tm, tn), lambda i, j: (i, j)),
            pl.BlockSpec((tm, GATE_LANES), lambda i, j: (i, 0)),
        ],
        out_shape=[
            jax.ShapeDtypeStruct((n_tok, D_MAIN), BF16),
            jax.ShapeDtypeStruct((n_tok, GATE_LANES), F32),
        ],
        scratch_shapes=[pltpu.VMEM((tm, D_MODEL), BF16)],
        compiler_params=_params(("parallel", "arbitrary")),
        name="inproj",
    )(*args)


def _log_sigmoid(x):
    return jnp.minimum(x, 0.0) - jnp.log1p(jnp.exp(-jnp.abs(x)))


def _head_out(h, o, wn):
    return (jax.nn.sigmoid(o.astype(F32)) * (_norm(h) * wn)).astype(BF16)


EXT = HEAD_DIM + GATE_LANES
PASS1_GROUP = 8


def _lane_tile(x, n):
    return jnp.concatenate([x] * n, axis=1)


def _mlstm_kernel(*refs, seq_len, has_state, emit_state):
    refs = list(refs)
    q_ref, k_ref, v_ref, o_ref, g_ref, wn_ref = refs[:6]
    refs = refs[6:]
    if has_state:
        m0_ref, c0_ref, n0_ref = refs[:3]
        refs = refs[3:]
    hm_ref = refs[0]
    refs = refs[1:]
    if emit_state:
        cout_ref, nout_ref, mout_ref = refs[:3]
        refs = refs[3:]
    num_scr, row_scr, kv_scr, sc_scr, c_scr, h_scr = refs
    L = CHUNK
    nc = seq_len // L
    scale = HEAD_DIM ** -0.5
    head = pl.program_id(1)
    gate_shift = jnp.where(head == 0, 0, GATE_LANES - 2 * N_DIR * head)

    def chunk_rows(c):
        return pl.ds(pl.multiple_of(c * L, L), L)

    def load_q(rows):
        return (q_ref[rows, :].astype(F32) * scale).astype(BF16)

    t_idx = lax.broadcasted_iota(jnp.int32, (L, L), 0)
    s_idx = lax.broadcasted_iota(jnp.int32, (L, L), 1)
    hi_rows = lax.broadcasted_iota(jnp.int32, (16, L), 0) < 8

    masks = [s_idx <= t_idx, s_idx >= t_idx]
    masks_b = [jnp.where(mk, 1.0, 0.0).astype(BF16) for mk in masks]
    group = min(nc, PASS1_GROUP)

    def pass1(grp, carry):
        chunks = [grp * group + i for i in range(group)]
        base, items = [], []
        for c in chunks:
            rows = chunk_rows(c)
            qb, kb, vb = load_q(rows), k_ref[rows, :], v_ref[rows, :]
            v_ext = jnp.concatenate([vb, jnp.ones((L, GATE_LANES), BF16)], axis=1)
            s0 = _dot_nt(qb, kb)
            g_row = jnp.transpose(pltpu.roll(g_ref[rows, :], gate_shift, 1))
            h_scr[rows, :] = jnp.zeros((L, HEAD_DIM), F32)
            base.append((c, rows, kb, v_ext, s0))
            for d in range(N_DIR):
                ig_row = g_row[2 * d:2 * d + 1, :]
                lf = _log_sigmoid(g_row[2 * d + 1:2 * d + 2, :])
                lf_hi = lf.astype(BF16)
                lf_lo = (lf - lf_hi.astype(F32)).astype(BF16)
                lhs = jnp.where(hi_rows, lf_hi.astype(F32), lf_lo.astype(F32)).astype(BF16)
                r16 = _dot_nt(lhs, masks_b[d])
                lf_rep = jnp.concatenate([jnp.broadcast_to(lf_hi, (GATE_LANES, L)),
                                          jnp.broadcast_to(lf_lo, (GATE_LANES, L))], axis=0)
                bb = _dot_nt(masks_b[d], lf_rep)
                items.append((len(base) - 1, d, ig_row, lf, r16, bb))
        states = []
        for bi, d, ig_row, lf, r16, bb in items:
            c, rows, kb, v_ext, s0 = base[bi]
            b_row = r16[0:1, :] + r16[8:9, :]
            b_rep = bb[:, :GATE_LANES] + bb[:, GATE_LANES:]
            c_row = ig_row - b_row
            c_max = jnp.broadcast_to(
                jnp.max(jnp.where(masks[d], c_row, -jnp.inf), axis=1, keepdims=True), (L, L))
            w = jnp.exp(jnp.where(masks[d], c_row - c_max, -jnp.inf))
            pv = _dot((s0 * w).astype(BF16), v_ext)
            num_scr[d, rows, :] = pv[:, :HEAD_DIM]
            row_scr[d, 0, rows, :] = pv[:, HEAD_DIM:]
            row_scr[d, 1, rows, :] = b_rep + c_max[:, :GATE_LANES]
            row_scr[d, 2, rows, :] = b_rep
            states.append((bi, d, ig_row, lf, b_row))
        k_ts = [jnp.transpose(kb.astype(F32)) for (_, _, kb, _, _) in base]
        for bi, d, ig_row, lf, b_row in states:
            c, rows, kb, v_ext, s0 = base[bi]
            b_last = jnp.sum(lf, axis=1, keepdims=True)
            g = b_last - b_row + ig_row
            g_max = jnp.max(g, axis=1, keepdims=True)
            wk = jnp.exp(g - g_max)
            kv_scr[d, c] = _dot((k_ts[bi] * wk).astype(BF16), v_ext)
            sc_scr[d, c, 0] = jnp.broadcast_to(b_last, (8, GATE_LANES))
            sc_scr[d, c, 1] = jnp.broadcast_to(g_max, (8, GATE_LANES))
        return carry

    lax.fori_loop(0, nc // group, pass1, 0)

    ms = []
    for d in range(N_DIR):
        if has_state:
            bh = pl.program_id(0) * (N_DIR * N_HEADS) + d * N_HEADS + head
            n_rep = jnp.transpose(jnp.broadcast_to(n0_ref[0, d, 0], (GATE_LANES, HEAD_DIM)))
            c_scr[d] = jnp.concatenate([c0_ref[0, d, 0], n_rep], axis=1)
            ms.append(jnp.full((1, GATE_LANES), m0_ref[bh], F32))
        else:
            c_scr[d] = jnp.zeros((HEAD_DIM, EXT), F32)
            ms.append(jnp.zeros((1, GATE_LANES), F32))

    def pass2(j, carry):
        new = []
        for d, m in enumerate(carry):
            c = j if d == 0 else nc - 1 - j
            rows = chunk_rows(c)
            c_ext = c_scr[d]
            qcn = _dot(load_q(rows), c_ext.astype(BF16))
            den_i, a_rep, b_rep = row_scr[d, 0, rows, :], row_scr[d, 1, rows, :], row_scr[d, 2, rows, :]
            m_rep = jnp.maximum(b_rep + m, a_rep)
            r_intra = jnp.exp(a_rep - m_rep)
            r_state = jnp.exp(b_rep + m - m_rep)
            den = r_intra * den_i + r_state * qcn[:, HEAD_DIM:]
            inv = 1.0 / jnp.maximum(jnp.abs(den), jnp.exp(-m_rep))
            h = (_lane_tile(r_intra * inv, 2) * num_scr[d, rows, :]
                 + _lane_tile(r_state * inv, 2) * qcn[:, :HEAD_DIM])
            h_scr[rows, :] += h
            b_last, g_max = sc_scr[d, c, 0][0:1, :], sc_scr[d, c, 1][0:1, :]
            m_new = jnp.maximum(b_last + m, g_max)
            decay = jnp.exp(b_last + m - m_new)
            gain = jnp.exp(g_max - m_new)
            c_scr[d] = _lane_tile(decay, 3) * c_ext + _lane_tile(gain, 3) * kv_scr[d, c]
            new.append(m_new)
        return tuple(new)

    ms = lax.fori_loop(0, nc, pass2, tuple(ms))

    wn = wn_ref[0]

    def finish(c, carry):
        rows = chunk_rows(c)
        hm_ref[rows, :] = _head_out(h_scr[rows, :], o_ref[rows, :], wn)
        return carry

    lax.fori_loop(0, nc, finish, 0)
    if emit_state:
        for d in range(N_DIR):
            c_ext = c_scr[d]
            cout_ref[0, 0, d, 0] = c_ext[:, :HEAD_DIM]
            nout_ref[0, 0, d, 0] = jnp.transpose(c_ext[:, HEAD_DIM:])[0:1, :]
            mout_ref[0, d, 0] = ms[d]


def _mlstm(p_main, gates, w_hnorm, batch, seq_len, state=None, emit_state=False):
    n_tok = p_main.shape[0]
    T = seq_len
    blk = lambda col: pl.BlockSpec((T, HEAD_DIM), lambda b, h: (b, col // HEAD_DIM + h))
    in_specs = [blk(COL_Q), blk(COL_K), blk(COL_V), blk(COL_O),
                pl.BlockSpec((T, GATE_LANES), lambda b, h: (b, 0)),
                pl.BlockSpec((1, 1, HEAD_DIM), lambda b, h: (h, 0, 0))]
    args = [p_main, p_main, p_main, p_main, gates, w_hnorm.reshape(N_HEADS, 1, HEAD_DIM)]
    has_state = state is not None
    if has_state:
        C0, n0, m0 = state
        in_specs += [
            pl.BlockSpec(memory_space=pltpu.SMEM),
            pl.BlockSpec((1, N_DIR, 1, HEAD_DIM, HEAD_DIM), lambda b, h: (b, 0, h, 0, 0)),
            pl.BlockSpec((1, N_DIR, 1, 1, HEAD_DIM), lambda b, h: (b, 0, h, 0, 0)),
        ]
        args += [m0.reshape(-1), C0, n0.reshape(batch, N_DIR, N_HEADS, 1, HEAD_DIM)]
    out_specs = [pl.BlockSpec((T, HEAD_DIM), lambda b, h: (b, h))]
    out_shape = [jax.ShapeDtypeStruct((n_tok, D_MLSTM), BF16)]
    if emit_state:
        out_specs += [
            pl.BlockSpec((1, 1, N_DIR, 1, HEAD_DIM, HEAD_DIM), lambda b, h: (b, 0, 0, h, 0, 0)),
            pl.BlockSpec((1, 1, N_DIR, 1, 1, HEAD_DIM), lambda b, h: (b, 0, 0, h, 0, 0)),
            pl.BlockSpec((1, N_DIR, 1, 1, GATE_LANES), lambda b, h: (b, 0, h, 0, 0)),
        ]
        out_shape += [
            jax.ShapeDtypeStruct((batch, DEPTH, N_DIR, N_HEADS, HEAD_DIM, HEAD_DIM), F32),
            jax.ShapeDtypeStruct((batch, DEPTH, N_DIR, N_HEADS, 1, HEAD_DIM), F32),
            jax.ShapeDtypeStruct((batch, N_DIR, N_HEADS, 1, GATE_LANES), F32),
        ]
    nc = T // CHUNK
    scratch = [pltpu.VMEM((N_DIR, T, HEAD_DIM), F32),
               pltpu.VMEM((N_DIR, 3, T, GATE_LANES), F32),
               pltpu.VMEM((N_DIR, nc, HEAD_DIM, EXT), F32),
               pltpu.VMEM((N_DIR, nc, 2, 8, GATE_LANES), F32),
               pltpu.VMEM((N_DIR, HEAD_DIM, EXT), F32),
               pltpu.VMEM((T, HEAD_DIM), F32)]
    return pl.pallas_call(
        functools.partial(_mlstm_kernel, seq_len=T, has_state=has_state, emit_state=emit_state),
        grid=(batch, N_HEADS),
        in_specs=in_specs,
        out_specs=out_specs,
        out_shape=out_shape,
        scratch_shapes=scratch,
        compiler_params=_params(("parallel", "parallel")),
        name="mlstm",
    )(*args)


def _fourier_kernel(x_ref, cs_ref, ct_ref, st_ref, o_ref, *, scale):
    for g in range(N_FGROUPS):
        cols = slice(g * FGROUP_DIM, (g + 1) * FGROUP_DIM)
        z = _dot(x_ref[:, cols], cs_ref[...])
        zc = z[:, :FGROUP_DIM].astype(BF16)
        zs = z[:, FGROUP_DIM:].astype(BF16)
        y = _dot(ct_ref[...], zc) - _dot(st_ref[...], zs)
        o_ref[:, cols] = (y * scale).astype(BF16)


def _dft_tables(n):
    k = jnp.arange(n, dtype=jnp.int32)
    ang = ((k[:, None] * k[None, :]) % n).astype(F32) * (2.0 * jnp.pi / n)
    return jnp.cos(ang), jnp.sin(ang)


DFT_BLOCK = 256
DFT_RADIX = 8
DFT_GROUPS = 2
HALF_LANES = 128


def _cadd(a, b):
    return a[0] + b[0], a[1] + b[1]


def _csub(a, b):
    return a[0] - b[0], a[1] - b[1]


def _mul_neg_i(a):
    return a[1], -a[0]


def _dft4(y):
    c0, c1 = _cadd(y[0], y[2]), _cadd(y[1], y[3])
    d0, d1 = _csub(y[0], y[2]), _mul_neg_i(_csub(y[1], y[3]))
    return [_cadd(c0, c1), _cadd(d0, d1), _csub(c0, c1), _csub(d0, d1)]


def _dft8(x):
    r = 0.5 ** 0.5
    a = [_cadd(x[n], x[n + 4]) for n in range(4)]
    b = [_csub(x[n], x[n + 4]) for n in range(4)]
    b[1] = ((b[1][0] + b[1][1]) * r, (b[1][1] - b[1][0]) * r)
    b[2] = _mul_neg_i(b[2])
    b[3] = ((b[3][1] - b[3][0]) * r, -(b[3][0] + b[3][1]) * r)
    even, odd = _dft4(a), _dft4(b)
    return [even[k // 2] if k % 2 == 0 else odd[k // 2] for k in range(8)]


def _fourier_long_kernel(x_ref, cs_ref, twc_ref, tws_ref, o_ref, w_scr, *, scale):
    nb = DFT_BLOCK
    cs = cs_ref[...]
    c_tab, s_tab = cs[:, :nb], cs[:, nb:]
    n_groups = x_ref.shape[1] // FGROUP_DIM
    zs = []
    for g in range(n_groups):
        cols = slice(g * FGROUP_DIM, (g + 1) * FGROUP_DIM)
        z = []
        for t1 in range(DFT_RADIX):
            zz = _dot(x_ref[t1 * nb:(t1 + 1) * nb, cols], cs)
            z.append((zz[:, :nb], -zz[:, nb:]))
        zs.append(z)
    for g, z in enumerate(zs):
        a = _dft8(z)
        for u1 in range(DFT_RADIX):
            ar, ai = a[u1]
            if u1 > 0:
                twc = twc_ref[u1 * nb:(u1 + 1) * nb, :]
                tws = tws_ref[u1 * nb:(u1 + 1) * nb, :]
                ar, ai = ar * twc + ai * tws, ai * twc - ar * tws
            y = (_dot(c_tab, ar.astype(BF16)) + _dot(s_tab, ai.astype(BF16))) * scale
            for half in range(FGROUP_DIM // HALF_LANES):
                w_scr[g, half, pl.ds(u1, nb, stride=DFT_RADIX), :] = (
                    y[:, half * HALF_LANES:(half + 1) * HALF_LANES])
    for g in range(n_groups):
        for half in range(FGROUP_DIM // HALF_LANES):
            lanes = slice(g * FGROUP_DIM + half * HALF_LANES, g * FGROUP_DIM + (half + 1) * HALF_LANES)
            o_ref[:, lanes] = w_scr[g, half].astype(BF16)


def _fourier_long(p_main, batch, seq_len):
    n_tok = p_main.shape[0]
    T = seq_len
    assert T == DFT_RADIX * DFT_BLOCK and FGROUP_DIM == DFT_BLOCK
    cc, sc = _dft_tables(FGROUP_DIM)
    cs = jnp.concatenate([cc, sc], axis=1).astype(BF16)
    u1 = jnp.repeat(jnp.arange(DFT_RADIX, dtype=jnp.int32), DFT_BLOCK)
    t2 = jnp.tile(jnp.arange(DFT_BLOCK, dtype=jnp.int32), DFT_RADIX)
    ang = ((u1 * t2) % T).astype(F32) * (2.0 * jnp.pi / T)
    twc = jnp.broadcast_to(jnp.cos(ang)[:, None], (T, FGROUP_DIM))
    tws = jnp.broadcast_to(jnp.sin(ang)[:, None], (T, FGROUP_DIM))
    gw = DFT_GROUPS * FGROUP_DIM
    return pl.pallas_call(
        functools.partial(_fourier_long_kernel, scale=float((T * FGROUP_DIM) ** -0.5)),
        grid=(batch, N_FGROUPS // DFT_GROUPS),
        in_specs=[
            pl.BlockSpec((T, gw), lambda b, g: (b, COL_FR // gw + g)),
            _resident((FGROUP_DIM, 2 * FGROUP_DIM), lambda b, g: (0, 0)),
            _resident((T, FGROUP_DIM), lambda b, g: (0, 0)),
            _resident((T, FGROUP_DIM), lambda b, g: (0, 0)),
        ],
        out_specs=pl.BlockSpec((T, gw), lambda b, g: (b, g)),
        out_shape=jax.ShapeDtypeStruct((n_tok, D_FOURIER), BF16),
        scratch_shapes=[pltpu.VMEM((DFT_GROUPS, FGROUP_DIM // HALF_LANES, T, HALF_LANES), F32)],
        compiler_params=_params(("parallel", "parallel")),
        name="fourier_long",
    )(p_main, cs, twc, tws)


def _fourier(p_main, batch, seq_len):
    if seq_len == DFT_RADIX * DFT_BLOCK:
        return _fourier_long(p_main, batch, seq_len)
    n_tok = p_main.shape[0]
    T = seq_len
    cc, sc = _dft_tables(FGROUP_DIM)
    cs = jnp.concatenate([cc, sc], axis=1).astype(BF16)
    ct, st = _dft_tables(T)
    return pl.pallas_call(
        functools.partial(_fourier_kernel, scale=float((T * FGROUP_DIM) ** -0.5)),
        grid=(batch,),
        in_specs=[
            pl.BlockSpec((T, D_FOURIER), lambda b: (b, COL_FR // D_FOURIER)),
            _resident((FGROUP_DIM, 2 * FGROUP_DIM), lambda b: (0, 0)),
            _resident((T, T), lambda b: (0, 0)),
            _resident((T, T), lambda b: (0, 0)),
        ],
        out_specs=pl.BlockSpec((T, D_FOURIER), lambda b: (b, 0)),
        out_shape=jax.ShapeDtypeStruct((n_tok, D_FOURIER), BF16),
        compiler_params=_params(("parallel",)),
        name="fourier",
    )(p_main, cs, ct.astype(BF16), st.astype(BF16))


def _mix_kernel(*refs, tm, tiles_per_seq, has_pos):
    if has_pos:
        (hm_ref, fr_ref, ga0_ref, ga1_ref, gb0_ref, gb1_ref, x_ref, pr_ref, pc_ref, mod_ref,
         wm_ref, wf_ref, wo_ref, lg_ref, lb_ref, o_ref) = refs
    else:
        (hm_ref, fr_ref, ga0_ref, ga1_ref, gb0_ref, gb1_ref, x_ref, mod_ref,
         wm_ref, wf_ref, wo_ref, lg_ref, lb_ref, o_ref) = refs
    g1 = mod_ref[0, 2:3, :]
    for r in range(tm // MIX_SUB):
        rows = slice(r * MIX_SUB, (r + 1) * MIX_SUB)
        a = _dot(hm_ref[rows, :], wm_ref[...])
        b = _dot(fr_ref[rows, :], wf_ref[...])
        gate = lambda g0, g1_: jax.nn.sigmoid(jnp.concatenate([g0[rows, :], g1_[rows, :]], axis=1).astype(F32))
        mixed = gate(ga0_ref, ga1_ref) * a + gate(gb0_ref, gb1_ref) * b
        z = _dot(mixed.astype(BF16), wo_ref[...])
        x = x_ref[rows, :]
        if has_pos:
            x = _add_grid_pos(x, pr_ref, pc_ref, (pl.program_id(0) % tiles_per_seq) * tm + r * MIX_SUB)
        o_ref[rows, :] = _norm(ALPHA * x + g1 * z) * lg_ref[...] + lb_ref[...]


def _mix(hm, fr, p_main, x2d, pos, mod6, w_br_m, w_br_f, w_out, ln_g, ln_b, seq_len, per_batch_mod):
    n_tok = x2d.shape[0]
    tm = MIX_TM
    tiles_per_seq = max(seq_len // tm, 1)
    has_pos = pos is not None
    mod_idx = (lambda i: (i // tiles_per_seq, 0, 0)) if per_batch_mod else (lambda i: (0, 0, 0))
    in_specs = [
        pl.BlockSpec((tm, D_MLSTM), lambda i: (i, 0)),
        pl.BlockSpec((tm, D_FOURIER), lambda i: (i, 0)),
        pl.BlockSpec((tm, GATE_BLOCK), lambda i: (i, COL_GA // GATE_BLOCK)),
        pl.BlockSpec((tm, GATE_BLOCK), lambda i: (i, COL_GA // GATE_BLOCK + 1)),
        pl.BlockSpec((tm, GATE_BLOCK), lambda i: (i, COL_GB // GATE_BLOCK)),
        pl.BlockSpec((tm, GATE_BLOCK), lambda i: (i, COL_GB // GATE_BLOCK + 1)),
        pl.BlockSpec((tm, D_MODEL), lambda i: (i, 0)),
    ]
    args = [hm, fr, p_main, p_main, p_main, p_main, x2d]
    if has_pos:
        in_specs += [_resident(p.shape, lambda i: (0, 0)) for p in pos]
        args += list(pos)
    in_specs += [
        pl.BlockSpec((1, 6, D_MODEL), mod_idx),
        _resident((D_MLSTM, D_MODEL), lambda i: (0, 0)),
        _resident((D_FOURIER, D_MODEL), lambda i: (0, 0)),
        _resident((D_MODEL, D_MODEL), lambda i: (0, 0)),
        _resident((1, D_MODEL), lambda i: (0, 0)),
        _resident((1, D_MODEL), lambda i: (0, 0)),
    ]
    args += [mod6, w_br_m, w_br_f, w_out, ln_g.reshape(1, D_MODEL), ln_b.reshape(1, D_MODEL)]
    return pl.pallas_call(
        functools.partial(_mix_kernel, tm=tm, tiles_per_seq=tiles_per_seq, has_pos=has_pos),
        grid=(n_tok // tm,),
        in_specs=in_specs,
        out_specs=pl.BlockSpec((tm, D_MODEL), lambda i: (i, 0)),
        out_shape=jax.ShapeDtypeStruct((n_tok, D_MODEL), F32),
        compiler_params=_params(("parallel",)),
        name="mix",
    )(*args)


def _ffn_kernel(*refs, tm, seg_len, has_halo, tiles_per_seq):
    if has_halo:
        (x_ref, xp_ref, xn_ref, mod_ref, wv_ref, wg_ref, cwv_ref, cwg_ref, cbv_ref, cbg_ref,
         wd_ref, lg_ref, lb_ref, o_ref, h_scr) = refs
        acc_scr = o_ref
    else:
        (x_ref, mod_ref, wv_ref, wg_ref, cwv_ref, cwg_ref, cbv_ref, cbg_ref,
         wd_ref, lg_ref, lb_ref, o_ref, h_scr, acc_scr) = refs
    f = pl.program_id(1)
    n_seg = tm // seg_len
    stride = seg_len + 2 * HALO
    rows = n_seg * stride
    zeros = jnp.zeros((HALO, D_MODEL), BF16)

    @pl.when(f == 0)
    def _():
        sh = mod_ref[0, 3:4, :]
        sc = mod_ref[0, 4:5, :]
        modulate = lambda x: _norm(x) * (1.0 + sc) + sh
        for s in range(n_seg):
            h_scr[s * stride + HALO:(s + 1) * stride - HALO, :] = (
                modulate(x_ref[s * seg_len:(s + 1) * seg_len, :]).astype(BF16))
            h_scr[s * stride:s * stride + HALO, :] = zeros
            h_scr[(s + 1) * stride - HALO:(s + 1) * stride, :] = zeros
        if has_halo:
            t = pl.program_id(0) % tiles_per_seq
            hp = jnp.where(t == 0, 0.0, modulate(xp_ref[...]))
            hn = jnp.where(t == tiles_per_seq - 1, 0.0, modulate(xn_ref[...]))
            h_scr[0:HALO, :] = hp.astype(BF16)
            h_scr[rows - HALO:rows, :] = hn.astype(BF16)
        acc_scr[...] = jnp.zeros_like(acc_scr)

    def conv(u, cw_ref, cb_ref):
        prev = pltpu.roll(u, 1, 0)
        nxt = pltpu.roll(u, rows - 1, 0)
        y = prev * cw_ref[0:1, :] + u * cw_ref[1:2, :] + nxt * cw_ref[2:3, :] + cb_ref[...]
        return y[HALO:rows - HALO, :]

    h = h_scr[...]
    gate = conv(_dot(h, wg_ref[...]), cwg_ref, cbg_ref)
    gate = gate * jax.nn.sigmoid(gate)
    val = conv(_dot(h, wv_ref[...]), cwv_ref, cbv_ref)
    act = (gate * val).astype(BF16)
    acc_scr[...] += _dot(act, wd_ref[...])

    @pl.when(f == pl.num_programs(1) - 1)
    def _():
        g2 = mod_ref[0, 5:6, :]
        for s in range(n_seg):
            x = x_ref[s * seg_len:(s + 1) * seg_len, :]
            y = acc_scr[s * stride:s * stride + seg_len, :]
            o_ref[s * seg_len:(s + 1) * seg_len, :] = _norm(ALPHA * x + g2 * y) * lg_ref[...] + lb_ref[...]


def _ffn(x1, mod6, w_up, w_conv, b_conv, w_down, ln_g, ln_b, seq_len, per_batch_mod):
    n_tok = x1.shape[0]
    tf = FFN_TF
    nf = D_FF // tf
    tm = FFN_TM_LONG if seq_len >= FFN_TM_LONG else FFN_TM_SHORT
    seg_len = min(seq_len, tm)
    tiles_per_seq = max(seq_len // tm, 1)
    has_halo = seq_len > tm
    h_rows = (tm // seg_len) * (seg_len + 2 * HALO)
    mod_idx = (lambda i, f: (i // tiles_per_seq, 0, 0)) if per_batch_mod else (lambda i, f: (0, 0, 0))
    hb = tm // HALO
    n_hblk = n_tok // HALO
    in_specs = [pl.BlockSpec((tm, D_MODEL), lambda i, f: (i, 0))]
    args = [x1]
    scratch = [pltpu.VMEM((h_rows, D_MODEL), BF16)]
    if has_halo:
        in_specs = [_resident((tm, D_MODEL), lambda i, f: (i, 0))]
        in_specs += [
            pl.BlockSpec((HALO, D_MODEL), lambda i, f: (jnp.maximum(i * hb - 1, 0), 0)),
            pl.BlockSpec((HALO, D_MODEL), lambda i, f: (jnp.minimum((i + 1) * hb, n_hblk - 1), 0)),
        ]
        args += [x1, x1]
    else:
        scratch.append(pltpu.VMEM((h_rows - 2 * HALO, D_MODEL), F32))
    in_specs += [
        pl.BlockSpec((1, 6, D_MODEL), mod_idx),
        pl.BlockSpec((D_MODEL, tf), lambda i, f: (0, f)),
        pl.BlockSpec((D_MODEL, tf), lambda i, f: (0, nf + f)),
        pl.BlockSpec((3, tf), lambda i, f: (0, f)),
        pl.BlockSpec((3, tf), lambda i, f: (0, nf + f)),
        pl.BlockSpec((1, tf), lambda i, f: (0, f)),
        pl.BlockSpec((1, tf), lambda i, f: (0, nf + f)),
        pl.BlockSpec((tf, D_MODEL), lambda i, f: (f, 0)),
        _resident((1, D_MODEL), lambda i, f: (0, 0)),
        _resident((1, D_MODEL), lambda i, f: (0, 0)),
    ]
    b_conv2 = b_conv.reshape(1, 2 * D_FF)
    args += [mod6, w_up, w_up, w_conv, w_conv, b_conv2, b_conv2, w_down,
             ln_g.reshape(1, D_MODEL), ln_b.reshape(1, D_MODEL)]
    return pl.pallas_call(
        functools.partial(_ffn_kernel, tm=tm, seg_len=seg_len, has_halo=has_halo,
                          tiles_per_seq=tiles_per_seq),
        grid=(n_tok // tm, nf),
        in_specs=in_specs,
        out_specs=pl.BlockSpec((tm, D_MODEL), lambda i, f: (i, 0)),
        out_shape=jax.ShapeDtypeStruct((n_tok, D_MODEL), F32),
        scratch_shapes=scratch,
        compiler_params=_params(("parallel", "arbitrary")),
        name="ffn",
    )(*args)


def _grid_pos_tables(n_tokens):
    quarter = D_MODEL // 4
    freq = 1.0 / (10000.0 ** (jnp.arange(quarter, dtype=F32) / quarter))
    er = jnp.arange(n_tokens // GRID_W, dtype=F32)[:, None] * freq
    ec = jnp.arange(GRID_W, dtype=F32)[:, None] * freq
    return (jnp.concatenate([jnp.sin(er), jnp.cos(er)], -1),
            jnp.concatenate([jnp.sin(ec), jnp.cos(ec)], -1))


def _split_w_in(w_in, b_gate):
    n_gate = 2 * N_DIR * N_HEADS
    w_main = jnp.concatenate([w_in[:, :4 * D_MLSTM], w_in[:, 4 * D_MLSTM + n_gate:]], axis=1).astype(BF16)
    gw = w_in[:, 4 * D_MLSTM:4 * D_MLSTM + n_gate].reshape(D_MODEL, N_DIR, 2, N_HEADS)
    gw = gw.transpose(0, 3, 1, 2).reshape(D_MODEL, n_gate)
    gw = jnp.pad(gw, ((0, 0), (0, GATE_LANES - n_gate))).astype(BF16)
    gb = b_gate.astype(F32).reshape(N_DIR, 2, N_HEADS).transpose(2, 0, 1).reshape(1, n_gate)
    gb = jnp.pad(gb, ((0, 0), (0, GATE_LANES - n_gate)))
    return w_main, gw, gb


def _layer(x2d, pos, mod6, weights, batch, seq_len, per_batch_mod, state, emit_state):
    (w_main, w_gate, b_gate, w_hnorm, w_br_m, w_br_f, w_out, ln1_g, ln1_b,
     w_up, w_conv, b_conv, w_down, ln2_g, ln2_b) = weights
    p_main, gates = _inproj(x2d, pos, mod6, w_main, w_gate, b_gate, seq_len, per_batch_mod)
    ml = _mlstm(p_main, gates, w_hnorm, batch, seq_len, state=state, emit_state=emit_state)
    fr = _fourier(p_main, batch, seq_len)
    x1 = _mix(ml[0], fr, p_main, x2d, pos, mod6, w_br_m, w_br_f, w_out, ln1_g, ln1_b,
              seq_len, per_batch_mod)
    x2 = _ffn(x1, mod6, w_up, w_conv, b_conv, w_down, ln2_g, ln2_b, seq_len, per_batch_mod)
    return x2, ml[1:]


def kernel(x_prompt, x_sample, c, state_C, state_n, state_m, c_ctx, w_ada, b_ada, w_in, b_gate,
           w_hnorm, w_br_m, w_br_f, w_out, ln1_g, ln1_b, w_up, w_conv, b_conv, w_down, ln2_g, ln2_b):
    assert w_ada.shape[0] == DEPTH
    B, S, _ = x_prompt.shape
    DB, DS, _ = x_sample.shape
    l = 0
    n_cond = 16
    cond = jnp.zeros((n_cond, D_MODEL), F32).at[0].set(c_ctx).at[1:1 + DB].set(c)
    mod6 = _modulation(cond, w_ada[l], b_ada[l]).reshape(n_cond, 6, D_MODEL)
    w_main, w_gate, b_gate_l = _split_w_in(w_in[l], b_gate[l])
    weights = (w_main, w_gate, b_gate_l, w_hnorm[l], w_br_m[l].astype(BF16), w_br_f[l].astype(BF16),
               w_out[l].astype(BF16), ln1_g[l], ln1_b[l], w_up[l].astype(BF16), w_conv[l], b_conv[l],
               w_down[l].astype(BF16), ln2_g[l], ln2_b[l])
    pos = _grid_pos_tables(DS)

    yp, states = _layer(x_prompt.reshape(B * S, D_MODEL), None, mod6[0:1], weights, B, S,
                        per_batch_mod=False, state=None, emit_state=True)
    ys, _ = _layer(x_sample.reshape(DB * DS, D_MODEL), pos, mod6[1:1 + DB], weights, DB, DS,
                   per_batch_mod=True, state=(state_C[:, l], state_n[:, l], state_m[:, l]),
                   emit_state=False)
    new_C, new_n, new_m = states
    new_n = new_n.reshape(B, DEPTH, N_DIR, N_HEADS, HEAD_DIM)
    new_m = new_m[:, :, :, 0, 0].reshape(B, DEPTH, N_DIR, N_HEADS)
    return (yp.reshape(B, S, D_MODEL), ys.reshape(DB, DS, D_MODEL), new_C, new_n, new_m)
```

```python
import functools

import jax
import jax.numpy as jnp
from jax import lax
from jax.experimental import pallas as pl
from jax.experimental.pallas import tpu as pltpu

D_MODEL = 2048
N_HEADS = 4
HEAD_DIM = 256
D_MLSTM = N_HEADS * HEAD_DIM
N_FGROUPS = 4
FGROUP_DIM = 256
D_FOURIER = N_FGROUPS * FGROUP_DIM
D_FF = 5632
GRID_W = 64
N_DIR = 2
DEPTH = 1
ALPHA = (2.0 * DEPTH) ** 0.25
LN_EPS = 1e-5

F32 = jnp.float32
BF16 = jnp.bfloat16

CHUNK = 256
COL_Q = 0
COL_K = COL_Q + D_MLSTM
COL_V = COL_K + D_MLSTM
COL_O = COL_V + D_MLSTM
COL_FR = COL_O + D_MLSTM
COL_GA = COL_FR + D_FOURIER
COL_GB = COL_GA + D_MODEL
D_MAIN = COL_GB + D_MODEL
GATE_BLOCK = 1024
GATE_LANES = 128
HALO = 8

VMEM_LIMIT = 56 * 1024 * 1024

INPROJ_TM = 1024
INPROJ_TN = 2304
MIX_TM = 512
MIX_SUB = 256
FFN_TM_LONG = 1024
FFN_TM_SHORT = 512
FFN_TF = 512
FFN_VALUE_WINDOWS = 2


def _params(sem, flags=None):
    return pltpu.CompilerParams(dimension_semantics=sem, vmem_limit_bytes=VMEM_LIMIT, flags=flags)


def _resident(shape, index_map):
    return pl.BlockSpec(shape, index_map, pipeline_mode=pl.Buffered(1))


def _norm(x):
    mu = jnp.mean(x, axis=-1, keepdims=True)
    xc = x - mu
    var = jnp.mean(xc * xc, axis=-1, keepdims=True)
    return xc * lax.rsqrt(var + LN_EPS)


def _dot(a, b):
    return jnp.dot(a, b, preferred_element_type=F32)


def _dot_nt(a, b):
    return lax.dot_general(a, b, (((1,), (1,)), ((), ())), preferred_element_type=F32)


def _mod_kernel(c_ref, w_ref, b_ref, o_ref):
    c = c_ref[...]
    s = c * jax.nn.sigmoid(c)
    o_ref[...] = _dot(s.astype(BF16), w_ref[...].astype(BF16)) + b_ref[...]


def _modulation(cond, w_ada, b_ada):
    rows, tn = cond.shape[0], 1024
    n_out = w_ada.shape[1]
    return pl.pallas_call(
        _mod_kernel,
        grid=(n_out // tn,),
        in_specs=[
            _resident((rows, D_MODEL), lambda j: (0, 0)),
            pl.BlockSpec((D_MODEL, tn), lambda j: (0, j)),
            pl.BlockSpec((1, tn), lambda j: (0, j)),
        ],
        out_specs=pl.BlockSpec((rows, tn), lambda j: (0, j)),
        out_shape=jax.ShapeDtypeStruct((rows, n_out), F32),
        compiler_params=_params(("arbitrary",)),
        name="modulation",
    )(cond, w_ada, b_ada.reshape(1, n_out))


LN_ROWS = 256


def _add_grid_pos(x, pr_ref, pc_ref, tok0):
    half = D_MODEL // 2
    out = []
    for k in range(x.shape[0] // GRID_W):
        blk = x[k * GRID_W:(k + 1) * GRID_W, :]
        pr = pr_ref[pl.ds(tok0 // GRID_W + k, 1), :]
        out.append(jnp.concatenate([blk[:, :half] + pr, blk[:, half:] + pc_ref[...]], axis=1))
    return jnp.concatenate(out, axis=0)


def _inproj_kernel(*refs, tm, tiles_per_seq, has_pos):
    if has_pos:
        x_ref, pr_ref, pc_ref, mod_ref, w_ref, wg_ref, bg_ref, p_ref, g_ref, h_scr = refs
    else:
        x_ref, mod_ref, w_ref, wg_ref, bg_ref, p_ref, g_ref, h_scr = refs

    @pl.when(pl.program_id(1) == 0)
    def _():
        sh = mod_ref[0, 0:1, :]
        sc = mod_ref[0, 1:2, :]

        def ln_rows(r, carry):
            rows = pl.ds(pl.multiple_of(r * LN_ROWS, LN_ROWS), LN_ROWS)
            x = x_ref[rows, :]
            if has_pos:
                x = _add_grid_pos(x, pr_ref, pc_ref,
                                  (pl.program_id(0) % tiles_per_seq) * tm + r * LN_ROWS)
            h = (_norm(x) * (1.0 + sc) + sh).astype(BF16)
            h_scr[rows, :] = h
            g_ref[rows, :] = _dot(h, wg_ref[...]) + bg_ref[...]
            return carry

        lax.fori_loop(0, tm // LN_ROWS, ln_rows, 0)

    p_ref[...] = _dot(h_scr[...], w_ref[...]).astype(BF16)


def _inproj(x2d, pos, mod6, w_main, w_gate, b_gate, seq_len, per_batch_mod):
    n_tok = x2d.shape[0]
    tm, tn = INPROJ_TM, INPROJ_TN
    tiles_per_seq = max(seq_len // tm, 1)
    has_pos = pos is not None
    mod_idx = (lambda i, j: (i // tiles_per_seq, 0, 0)) if per_batch_mod else (lambda i, j: (0, 0, 0))
    in_specs = [pl.BlockSpec((tm, D_MODEL), lambda i, j: (i, 0))]
    args = [x2d]
    if has_pos:
        in_specs += [_resident(p.shape, lambda i, j: (0, 0)) for p in pos]
        args += list(pos)
    in_specs += [
        pl.BlockSpec((1, 6, D_MODEL), mod_idx),
        pl.BlockSpec((D_MODEL, tn), lambda i, j: (0, j)),
        _resident((D_MODEL, GATE_LANES), lambda i, j: (0, 0)),
        _resident((1, GATE_LANES), lambda i, j: (0, 0)),
    ]
    args += [mod6, w_main, w_gate, b_gate]
    return pl.pallas_call(
        functools.partial(_inproj_kernel, tm=tm, tiles_per_seq=tiles_per_seq, has_pos=has_pos),
        grid=(n_tok // tm, D_MAIN // tn),
        in_specs=in_specs,
        out_specs=[
            pl.BlockSpec((tm, tn), lambda i, j: (i, j)),
            pl.BlockSpec((tm, GATE_LANES), lambda i, j: (i, 0)),
        ],
        out_shape=[
            jax.ShapeDtypeStruct((n_tok, D_MAIN), BF16),
            jax.ShapeDtypeStruct((n_tok, GATE_LANES), F32),
        ],
        scratch_shapes=[pltpu.VMEM((tm, D_MODEL), BF16)],
        compiler_params=_params(("parallel", "arbitrary")),
        name="inproj",
    )(*args)


def _log_sigmoid(x):
    return jnp.minimum(x, 0.0) - jnp.log1p(jnp.exp(-jnp.abs(x)))


def _head_out(h, o, wn):
    return (jax.nn.sigmoid(o.astype(F32)) * (_norm(h) * wn)).astype(BF16)


EXT = HEAD_DIM + GATE_LANES
PASS1_GROUP = 8


def _lane_tile(x, n):
    return jnp.concatenate([x] * n, axis=1)


def _mlstm_kernel(*refs, seq_len, has_state, emit_state):
    refs = list(refs)
    q_ref, k_ref, v_ref, o_ref, g_ref, wn_ref = refs[:6]
    refs = refs[6:]
    if has_state:
        m0_ref, c0_ref, n0_ref = refs[:3]
        refs = refs[3:]
    hm_ref = refs[0]
    refs = refs[1:]
    if emit_state:
        cout_ref, nout_ref, mout_ref = refs[:3]
        refs = refs[3:]
    num_scr, row_scr, kv_scr, sc_scr, c_scr, h_scr = refs
    L = CHUNK
    nc = seq_len // L
    scale = HEAD_DIM ** -0.5
    head = pl.program_id(1)
    gate_shift = jnp.where(head == 0, 0, GATE_LANES - 2 * N_DIR * head)

    def chunk_rows(c):
        return pl.ds(pl.multiple_of(c * L, L), L)

    def load_q(rows):
        return (q_ref[rows, :].astype(F32) * scale).astype(BF16)

    t_idx = lax.broadcasted_iota(jnp.int32, (L, L), 0)
    s_idx = lax.broadcasted_iota(jnp.int32, (L, L), 1)
    hi_rows = lax.broadcasted_iota(jnp.int32, (16, L), 0) < 8

    masks = [s_idx <= t_idx, s_idx >= t_idx]
    masks_b = [jnp.where(mk, 1.0, 0.0).astype(BF16) for mk in masks]
    group = min(nc, PASS1_GROUP)

    def pass1(grp, carry):
        chunks = [grp * group + i for i in range(group)]
        base, items = [], []
        for c in chunks:
            rows = chunk_rows(c)
            qb, kb, vb = load_q(rows), k_ref[rows, :], v_ref[rows, :]
            v_ext = jnp.concatenate([vb, jnp.ones((L, GATE_LANES), BF16)], axis=1)
            s0 = _dot_nt(qb, kb)
            g_row = jnp.transpose(pltpu.roll(g_ref[rows, :], gate_shift, 1))
            h_scr[rows, :] = jnp.zeros((L, HEAD_DIM), F32)
            base.append((c, rows, kb, v_ext, s0))
            for d in range(N_DIR):
                ig_row = g_row[2 * d:2 * d + 1, :]
                lf = _log_sigmoid(g_row[2 * d + 1:2 * d + 2, :])
                lf_hi = lf.astype(BF16)
                lf_lo = (lf - lf_hi.astype(F32)).astype(BF16)
                lhs = jnp.where(hi_rows, lf_hi.astype(F32), lf_lo.astype(F32)).astype(BF16)
                r16 = _dot_nt(lhs, masks_b[d])
                lf_rep = jnp.concatenate([jnp.broadcast_to(lf_hi, (GATE_LANES, L)),
                                          jnp.broadcast_to(lf_lo, (GATE_LANES, L))], axis=0)
                bb = _dot_nt(masks_b[d], lf_rep)
                items.append((len(base) - 1, d, ig_row, lf, r16, bb))
        states = []
        for bi, d, ig_row, lf, r16, bb in items:
            c, rows, kb, v_ext, s0 = base[bi]
            b_row = r16[0:1, :] + r16[8:9, :]
            b_rep = bb[:, :GATE_LANES] + bb[:, GATE_LANES:]
            c_row = ig_row - b_row
            c_max = jnp.broadcast_to(
                jnp.max(jnp.where(masks[d], c_row, -jnp.inf), axis=1, keepdims=True), (L, L))
            w = jnp.exp(jnp.where(masks[d], c_row - c_max, -jnp.inf))
            pv = _dot((s0 * w).astype(BF16), v_ext)
            num_scr[d, rows, :] = pv[:, :HEAD_DIM]
            row_scr[d, 0, rows, :] = pv[:, HEAD_DIM:]
            row_scr[d, 1, rows, :] = b_rep + c_max[:, :GATE_LANES]
            row_scr[d, 2, rows, :] = b_rep
            states.append((bi, d, ig_row, lf, b_row))
        k_ts = [jnp.transpose(kb.astype(F32)) for (_, _, kb, _, _) in base]
        for bi, d, ig_row, lf, b_row in states:
            c, rows, kb, v_ext, s0 = base[bi]
            b_last = jnp.sum(lf, axis=1, keepdims=True)
            g = b_last - b_row + ig_row
            g_max = jnp.max(g, axis=1, keepdims=True)
            wk = jnp.exp(g - g_max)
            kv_scr[d, c] = _dot((k_ts[bi] * wk).astype(BF16), v_ext)
            sc_scr[d, c, 0] = jnp.broadcast_to(b_last, (8, GATE_LANES))
            sc_scr[d, c, 1] = jnp.broadcast_to(g_max, (8, GATE_LANES))
        return carry

    lax.fori_loop(0, nc // group, pass1, 0)

    ms = []
    for d in range(N_DIR):
        if has_state:
            bh = pl.program_id(0) * (N_DIR * N_HEADS) + d * N_HEADS + head
            n_rep = jnp.transpose(jnp.broadcast_to(n0_ref[0, d, 0], (GATE_LANES, HEAD_DIM)))
            c_scr[d] = jnp.concatenate([c0_ref[0, d, 0], n_rep], axis=1)
            ms.append(jnp.full((1, GATE_LANES), m0_ref[bh], F32))
        else:
            c_scr[d] = jnp.zeros((HEAD_DIM, EXT), F32)
            ms.append(jnp.zeros((1, GATE_LANES), F32))

    def pass2(j, carry):
        new = []
        for d, m in enumerate(carry):
            c = j if d == 0 else nc - 1 - j
            rows = chunk_rows(c)
            c_ext = c_scr[d]
            qcn = _dot(load_q(rows), c_ext.astype(BF16))
            den_i, a_rep, b_rep = row_scr[d, 0, rows, :], row_scr[d, 1, rows, :], row_scr[d, 2, rows, :]
            m_rep = jnp.maximum(b_rep + m, a_rep)
            r_intra = jnp.exp(a_rep - m_rep)
            r_state = jnp.exp(b_rep + m - m_rep)
            den = r_intra * den_i + r_state * qcn[:, HEAD_DIM:]
            inv = 1.0 / jnp.maximum(jnp.abs(den), jnp.exp(-m_rep))
            h = (_lane_tile(r_intra * inv, 2) * num_scr[d, rows, :]
                 + _lane_tile(r_state * inv, 2) * qcn[:, :HEAD_DIM])
            h_scr[rows, :] += h
            b_last, g_max = sc_scr[d, c, 0][0:1, :], sc_scr[d, c, 1][0:1, :]
            m_new = jnp.maximum(b_last + m, g_max)
            decay = jnp.exp(b_last + m - m_new)
            gain = jnp.exp(g_max - m_new)
            c_scr[d] = _lane_tile(decay, 3) * c_ext + _lane_tile(gain, 3) * kv_scr[d, c]
            new.append(m_new)
        return tuple(new)

    ms = lax.fori_loop(0, nc, pass2, tuple(ms))

    wn = wn_ref[0]

    def finish(c, carry):
        rows = chunk_rows(c)
        hm_ref[rows, :] = _head_out(h_scr[rows, :], o_ref[rows, :], wn)
        return carry

    lax.fori_loop(0, nc, finish, 0)
    if emit_state:
        for d in range(N_DIR):
            c_ext = c_scr[d]
            cout_ref[0, 0, d, 0] = c_ext[:, :HEAD_DIM]
            nout_ref[0, 0, d, 0] = jnp.transpose(c_ext[:, HEAD_DIM:])[0:1, :]
            mout_ref[0, d, 0] = ms[d]


def _mlstm(p_main, gates, w_hnorm, batch, seq_len, state=None, emit_state=False):
    n_tok = p_main.shape[0]
    T = seq_len
    blk = lambda col: pl.BlockSpec((T, HEAD_DIM), lambda b, h: (b, col // HEAD_DIM + h))
    in_specs = [blk(COL_Q), blk(COL_K), blk(COL_V), blk(COL_O),
                pl.BlockSpec((T, GATE_LANES), lambda b, h: (b, 0)),
                pl.BlockSpec((1, 1, HEAD_DIM), lambda b, h: (h, 0, 0))]
    args = [p_main, p_main, p_main, p_main, gates, w_hnorm.reshape(N_HEADS, 1, HEAD_DIM)]
    has_state = state is not None
    if has_state:
        C0, n0, m0 = state
        in_specs += [
            pl.BlockSpec(memory_space=pltpu.SMEM),
            pl.BlockSpec((1, N_DIR, 1, HEAD_DIM, HEAD_DIM), lambda b, h: (b, 0, h, 0, 0)),
            pl.BlockSpec((1, N_DIR, 1, 1, HEAD_DIM), lambda b, h: (b, 0, h, 0, 0)),
        ]
        args += [m0.reshape(-1), C0, n0.reshape(batch, N_DIR, N_HEADS, 1, HEAD_DIM)]
    out_specs = [pl.BlockSpec((T, HEAD_DIM), lambda b, h: (b, h))]
    out_shape = [jax.ShapeDtypeStruct((n_tok, D_MLSTM), BF16)]
    if emit_state:
        out_specs += [
            pl.BlockSpec((1, 1, N_DIR, 1, HEAD_DIM, HEAD_DIM), lambda b, h: (b, 0, 0, h, 0, 0)),
            pl.BlockSpec((1, 1, N_DIR, 1, 1, HEAD_DIM), lambda b, h: (b, 0, 0, h, 0, 0)),
            pl.BlockSpec((1, N_DIR, 1, 1, GATE_LANES), lambda b, h: (b, 0, h, 0, 0)),
        ]
        out_shape += [
            jax.ShapeDtypeStruct((batch, DEPTH, N_DIR, N_HEADS, HEAD_DIM, HEAD_DIM), F32),
            jax.ShapeDtypeStruct((batch, DEPTH, N_DIR, N_HEADS, 1, HEAD_DIM), F32),
            jax.ShapeDtypeStruct((batch, N_DIR, N_HEADS, 1, GATE_LANES), F32),
        ]
    nc = T // CHUNK
    scratch = [pltpu.VMEM((N_DIR, T, HEAD_DIM), F32),
               pltpu.VMEM((N_DIR, 3, T, GATE_LANES), F32),
               pltpu.VMEM((N_DIR, nc, HEAD_DIM, EXT), F32),
               pltpu.VMEM((N_DIR, nc, 2, 8, GATE_LANES), F32),
               pltpu.VMEM((N_DIR, HEAD_DIM, EXT), F32),
               pltpu.VMEM((T, HEAD_DIM), F32)]
    return pl.pallas_call(
        functools.partial(_mlstm_kernel, seq_len=T, has_state=has_state, emit_state=emit_state),
        grid=(batch, N_HEADS),
        in_specs=in_specs,
        out_specs=out_specs,
        out_shape=out_shape,
        scratch_shapes=scratch,
        compiler_params=_params(("parallel", "parallel")),
        name="mlstm",
    )(*args)


def _fourier_kernel(x_ref, cs_ref, ct_ref, st_ref, o_ref, *, scale):
    for g in range(N_FGROUPS):
        cols = slice(g * FGROUP_DIM, (g + 1) * FGROUP_DIM)
        z = _dot(x_ref[:, cols], cs_ref[...])
        zc = z[:, :FGROUP_DIM].astype(BF16)
        zs = z[:, FGROUP_DIM:].astype(BF16)
        y = _dot(ct_ref[...], zc) - _dot(st_ref[...], zs)
        o_ref[:, cols] = (y * scale).astype(BF16)


def _dft_tables(n):
    k = jnp.arange(n, dtype=jnp.int32)
    ang = ((k[:, None] * k[None, :]) % n).astype(F32) * (2.0 * jnp.pi / n)
    return jnp.cos(ang), jnp.sin(ang)


DFT_BLOCK = 256
DFT_RADIX = 8
DFT_GROUPS = 2
HALF_LANES = 128


def _cadd(a, b):
    return a[0] + b[0], a[1] + b[1]


def _csub(a, b):
    return a[0] - b[0], a[1] - b[1]


def _mul_neg_i(a):
    return a[1], -a[0]


def _dft4(y):
    c0, c1 = _cadd(y[0], y[2]), _cadd(y[1], y[3])
    d0, d1 = _csub(y[0], y[2]), _mul_neg_i(_csub(y[1], y[3]))
    return [_cadd(c0, c1), _cadd(d0, d1), _csub(c0, c1), _csub(d0, d1)]


def _dft8(x):
    r = 0.5 ** 0.5
    a = [_cadd(x[n], x[n + 4]) for n in range(4)]
    b = [_csub(x[n], x[n + 4]) for n in range(4)]
    b[1] = ((b[1][0] + b[1][1]) * r, (b[1][1] - b[1][0]) * r)
    b[2] = _mul_neg_i(b[2])
    b[3] = ((b[3][1] - b[3][0]) * r, -(b[3][0] + b[3][1]) * r)
    even, odd = _dft4(a), _dft4(b)
    return [even[k // 2] if k % 2 == 0 else odd[k // 2] for k in range(8)]


def _fourier_long_kernel(x_ref, cs_ref, twc_ref, tws_ref, o_ref, w_scr, *, scale):
    nb = DFT_BLOCK
    cs = cs_ref[...]
    c_tab, s_tab = cs[:, :nb], cs[:, nb:]
    n_groups = x_ref.shape[1] // FGROUP_DIM
    zs = []
    for g in range(n_groups):
        cols = slice(g * FGROUP_DIM, (g + 1) * FGROUP_DIM)
        z = []
        for t1 in range(DFT_RADIX):
            zz = _dot(x_ref[t1 * nb:(t1 + 1) * nb, cols], cs)
            z.append((zz[:, :nb], -zz[:, nb:]))
        zs.append(z)
    for g, z in enumerate(zs):
        a = _dft8(z)
        for u1 in range(DFT_RADIX):
            ar, ai = a[u1]
            if u1 > 0:
                twc = twc_ref[u1 * nb:(u1 + 1) * nb, :]
                tws = tws_ref[u1 * nb:(u1 + 1) * nb, :]
                ar, ai = ar * twc + ai * tws, ai * twc - ar * tws
            y = (_dot(c_tab, ar.astype(BF16)) + _dot(s_tab, ai.astype(BF16))) * scale
            for half in range(FGROUP_DIM // HALF_LANES):
                w_scr[g, half, pl.ds(u1, nb, stride=DFT_RADIX), :] = (
                    y[:, half * HALF_LANES:(half + 1) * HALF_LANES])
    for g in range(n_groups):
        for half in range(FGROUP_DIM // HALF_LANES):
            lanes = slice(g * FGROUP_DIM + half * HALF_LANES, g * FGROUP_DIM + (half + 1) * HALF_LANES)
            o_ref[:, lanes] = w_scr[g, half].astype(BF16)


def _fourier_long(p_main, batch, seq_len):
    n_tok = p_main.shape[0]
    T = seq_len
    assert T == DFT_RADIX * DFT_BLOCK and FGROUP_DIM == DFT_BLOCK
    cc, sc = _dft_tables(FGROUP_DIM)
    cs = jnp.concatenate([cc, sc], axis=1).astype(BF16)
    u1 = jnp.repeat(jnp.arange(DFT_RADIX, dtype=jnp.int32), DFT_BLOCK)
    t2 = jnp.tile(jnp.arange(DFT_BLOCK, dtype=jnp.int32), DFT_RADIX)
    ang = ((u1 * t2) % T).astype(F32) * (2.0 * jnp.pi / T)
    twc = jnp.broadcast_to(jnp.cos(ang)[:, None], (T, FGROUP_DIM))
    tws = jnp.broadcast_to(jnp.sin(ang)[:, None], (T, FGROUP_DIM))
    gw = DFT_GROUPS * FGROUP_DIM
    return pl.pallas_call(
        functools.partial(_fourier_long_kernel, scale=float((T * FGROUP_DIM) ** -0.5)),
        grid=(batch, N_FGROUPS // DFT_GROUPS),
        in_specs=[
            pl.BlockSpec((T, gw), lambda b, g: (b, COL_FR // gw + g)),
            _resident((FGROUP_DIM, 2 * FGROUP_DIM), lambda b, g: (0, 0)),
            _resident((T, FGROUP_DIM), lambda b, g: (0, 0)),
            _resident((T, FGROUP_DIM), lambda b, g: (0, 0)),
        ],
        out_specs=pl.BlockSpec((T, gw), lambda b, g: (b, g)),
        out_shape=jax.ShapeDtypeStruct((n_tok, D_FOURIER), BF16),
        scratch_shapes=[pltpu.VMEM((DFT_GROUPS, FGROUP_DIM // HALF_LANES, T, HALF_LANES), F32)],
        compiler_params=_params(("parallel", "parallel")),
        name="fourier_long",
    )(p_main, cs, twc, tws)


def _fourier(p_main, batch, seq_len):
    if seq_len == DFT_RADIX * DFT_BLOCK:
        return _fourier_long(p_main, batch, seq_len)
    n_tok = p_main.shape[0]
    T = seq_len
    cc, sc = _dft_tables(FGROUP_DIM)
    cs = jnp.concatenate([cc, sc], axis=1).astype(BF16)
    ct, st = _dft_tables(T)
    return pl.pallas_call(
        functools.partial(_fourier_kernel, scale=float((T * FGROUP_DIM) ** -0.5)),
        grid=(batch,),
        in_specs=[
            pl.BlockSpec((T, D_FOURIER), lambda b: (b, COL_FR // D_FOURIER)),
            _resident((FGROUP_DIM, 2 * FGROUP_DIM), lambda b: (0, 0)),
            _resident((T, T), lambda b: (0, 0)),
            _resident((T, T), lambda b: (0, 0)),
        ],
        out_specs=pl.BlockSpec((T, D_FOURIER), lambda b: (b, 0)),
        out_shape=jax.ShapeDtypeStruct((n_tok, D_FOURIER), BF16),
        compiler_params=_params(("parallel",)),
        name="fourier",
    )(p_main, cs, ct.astype(BF16), st.astype(BF16))


def _mix_kernel(*refs, tm, tiles_per_seq, has_pos):
    if has_pos:
        (hm_ref, fr_ref, ga0_ref, ga1_ref, gb0_ref, gb1_ref, x_ref, pr_ref, pc_ref, mod_ref,
         wm_ref, wf_ref, wo_ref, lg_ref, lb_ref, o_ref) = refs
    else:
        (hm_ref, fr_ref, ga0_ref, ga1_ref, gb0_ref, gb1_ref, x_ref, mod_ref,
         wm_ref, wf_ref, wo_ref, lg_ref, lb_ref, o_ref) = refs
    g1 = mod_ref[0, 2:3, :]
    for r in range(tm // MIX_SUB):
        rows = slice(r * MIX_SUB, (r + 1) * MIX_SUB)
        a = _dot(hm_ref[rows, :], wm_ref[...])
        b = _dot(fr_ref[rows, :], wf_ref[...])
        gate = lambda g0, g1_: jax.nn.sigmoid(jnp.concatenate([g0[rows, :], g1_[rows, :]], axis=1).astype(F32))
        mixed = gate(ga0_ref, ga1_ref) * a + gate(gb0_ref, gb1_ref) * b
        z = _dot(mixed.astype(BF16), wo_ref[...])
        x = x_ref[rows, :]
        if has_pos:
            x = _add_grid_pos(x, pr_ref, pc_ref, (pl.program_id(0) % tiles_per_seq) * tm + r * MIX_SUB)
        o_ref[rows, :] = _norm(ALPHA * x + g1 * z) * lg_ref[...] + lb_ref[...]


def _mix(hm, fr, p_main, x2d, pos, mod6, w_br_m, w_br_f, w_out, ln_g, ln_b, seq_len, per_batch_mod):
    n_tok = x2d.shape[0]
    tm = MIX_TM
    tiles_per_seq = max(seq_len // tm, 1)
    has_pos = pos is not None
    mod_idx = (lambda i: (i // tiles_per_seq, 0, 0)) if per_batch_mod else (lambda i: (0, 0, 0))
    in_specs = [
        pl.BlockSpec((tm, D_MLSTM), lambda i: (i, 0)),
        pl.BlockSpec((tm, D_FOURIER), lambda i: (i, 0)),
        pl.BlockSpec((tm, GATE_BLOCK), lambda i: (i, COL_GA // GATE_BLOCK)),
        pl.BlockSpec((tm, GATE_BLOCK), lambda i: (i, COL_GA // GATE_BLOCK + 1)),
        pl.BlockSpec((tm, GATE_BLOCK), lambda i: (i, COL_GB // GATE_BLOCK)),
        pl.BlockSpec((tm, GATE_BLOCK), lambda i: (i, COL_GB // GATE_BLOCK + 1)),
        pl.BlockSpec((tm, D_MODEL), lambda i: (i, 0)),
    ]
    args = [hm, fr, p_main, p_main, p_main, p_main, x2d]
    if has_pos:
        in_specs += [_resident(p.shape, lambda i: (0, 0)) for p in pos]
        args += list(pos)
    in_specs += [
        pl.BlockSpec((1, 6, D_MODEL), mod_idx),
        _resident((D_MLSTM, D_MODEL), lambda i: (0, 0)),
        _resident((D_FOURIER, D_MODEL), lambda i: (0, 0)),
        _resident((D_MODEL, D_MODEL), lambda i: (0, 0)),
        _resident((1, D_MODEL), lambda i: (0, 0)),
        _resident((1, D_MODEL), lambda i: (0, 0)),
    ]
    args += [mod6, w_br_m, w_br_f, w_out, ln_g.reshape(1, D_MODEL), ln_b.reshape(1, D_MODEL)]
    return pl.pallas_call(
        functools.partial(_mix_kernel, tm=tm, tiles_per_seq=tiles_per_seq, has_pos=has_pos),
        grid=(n_tok // tm,),
        in_specs=in_specs,
        out_specs=pl.BlockSpec((tm, D_MODEL), lambda i: (i, 0)),
        out_shape=jax.ShapeDtypeStruct((n_tok, D_MODEL), F32),
        compiler_params=_params(("parallel",)),
        name="mix",
    )(*args)


def _ffn_kernel(*refs, tm, seg_len, has_halo, tiles_per_seq):
    if has_halo:
        (x_ref, xp_ref, xn_ref, mod_ref, wv_ref, wg_ref, cwv_ref, cwg_ref, cbv_ref, cbg_ref,
         wd_ref, lg_ref, lb_ref, o_ref, h_scr) = refs
        acc_scr = o_ref
    else:
        (x_ref, mod_ref, wv_ref, wg_ref, cwv_ref, cwg_ref, cbv_ref, cbg_ref,
         wd_ref, lg_ref, lb_ref, o_ref, h_scr, acc_scr) = refs
    f = pl.program_id(1)
    n_seg = tm // seg_len
    stride = seg_len + 2 * HALO
    rows = n_seg * stride
    zeros = jnp.zeros((HALO, D_MODEL), BF16)

    @pl.when(f == 0)
    def _():
        sh = mod_ref[0, 3:4, :]
        sc = mod_ref[0, 4:5, :]
        modulate = lambda x: _norm(x) * (1.0 + sc) + sh
        for s in range(n_seg):
            h_scr[s * stride + HALO:(s + 1) * stride - HALO, :] = (
                modulate(x_ref[s * seg_len:(s + 1) * seg_len, :]).astype(BF16))
            h_scr[s * stride:s * stride + HALO, :] = zeros
            h_scr[(s + 1) * stride - HALO:(s + 1) * stride, :] = zeros
        if has_halo:
            t = pl.program_id(0) % tiles_per_seq
            hp = jnp.where(t == 0, 0.0, modulate(xp_ref[...]))
            hn = jnp.where(t == tiles_per_seq - 1, 0.0, modulate(xn_ref[...]))
            h_scr[0:HALO, :] = hp.astype(BF16)
            h_scr[rows - HALO:rows, :] = hn.astype(BF16)
        acc_scr[...] = jnp.zeros_like(acc_scr)

    def conv(u, cw_ref, cb_ref):
        n = u.shape[0]
        prev = pltpu.roll(u, 1, 0)
        nxt = pltpu.roll(u, n - 1, 0)
        y = prev * cw_ref[0:1, :] + u * cw_ref[1:2, :] + nxt * cw_ref[2:3, :] + cb_ref[...]
        return y[HALO:n - HALO, :]

    gate = conv(_dot(h_scr[...], wg_ref[...]), cwg_ref, cbg_ref)
    gate = gate * jax.nn.sigmoid(gate)
    n_win = FFN_VALUE_WINDOWS if has_halo else 1
    win = (rows - 2 * HALO) // n_win
    uvs = [_dot(h_scr[p * win:(p + 1) * win + 2 * HALO, :], wv_ref[...]) for p in range(n_win)]
    val = jnp.concatenate([conv(u, cwv_ref, cbv_ref) for u in uvs], axis=0)
    act = (gate * val).astype(BF16)
    acc_scr[...] += _dot(act, wd_ref[...])

    @pl.when(f == pl.num_programs(1) - 1)
    def _():
        g2 = mod_ref[0, 5:6, :]
        for s in range(n_seg):
            x = x_ref[s * seg_len:(s + 1) * seg_len, :]
            y = acc_scr[s * stride:s * stride + seg_len, :]
            o_ref[s * seg_len:(s + 1) * seg_len, :] = _norm(ALPHA * x + g2 * y) * lg_ref[...] + lb_ref[...]


def _ffn(x1, mod6, w_up, w_conv, b_conv, w_down, ln_g, ln_b, seq_len, per_batch_mod):
    n_tok = x1.shape[0]
    tf = FFN_TF
    nf = D_FF // tf
    tm = FFN_TM_LONG if seq_len >= FFN_TM_LONG else FFN_TM_SHORT
    seg_len = min(seq_len, tm)
    tiles_per_seq = max(seq_len // tm, 1)
    has_halo = seq_len > tm
    h_rows = (tm // seg_len) * (seg_len + 2 * HALO)
    mod_idx = (lambda i, f: (i // tiles_per_seq, 0, 0)) if per_batch_mod else (lambda i, f: (0, 0, 0))
    hb = tm // HALO
    n_hblk = n_tok // HALO
    in_specs = [pl.BlockSpec((tm, D_MODEL), lambda i, f: (i, 0))]
    args = [x1]
    scratch = [pltpu.VMEM((h_rows, D_MODEL), BF16)]
    if has_halo:
        in_specs = [_resident((tm, D_MODEL), lambda i, f: (i, 0))]
        in_specs += [
            pl.BlockSpec((HALO, D_MODEL), lambda i, f: (jnp.maximum(i * hb - 1, 0), 0)),
            pl.BlockSpec((HALO, D_MODEL), lambda i, f: (jnp.minimum((i + 1) * hb, n_hblk - 1), 0)),
        ]
        args += [x1, x1]
    else:
        scratch.append(pltpu.VMEM((h_rows - 2 * HALO, D_MODEL), F32))
    in_specs += [
        pl.BlockSpec((1, 6, D_MODEL), mod_idx),
        pl.BlockSpec((D_MODEL, tf), lambda i, f: (0, f)),
        pl.BlockSpec((D_MODEL, tf), lambda i, f: (0, nf + f)),
        pl.BlockSpec((3, tf), lambda i, f: (0, f)),
        pl.BlockSpec((3, tf), lambda i, f: (0, nf + f)),
        pl.BlockSpec((1, tf), lambda i, f: (0, f)),
        pl.BlockSpec((1, tf), lambda i, f: (0, nf + f)),
        pl.BlockSpec((tf, D_MODEL), lambda i, f: (f, 0)),
        _resident((1, D_MODEL), lambda i, f: (0, 0)),
        _resident((1, D_MODEL), lambda i, f: (0, 0)),
    ]
    b_conv2 = b_conv.reshape(1, 2 * D_FF)
    args += [mod6, w_up, w_up, w_conv, w_conv, b_conv2, b_conv2, w_down,
             ln_g.reshape(1, D_MODEL), ln_b.reshape(1, D_MODEL)]
    return pl.pallas_call(
        functools.partial(_ffn_kernel, tm=tm, seg_len=seg_len, has_halo=has_halo,
                          tiles_per_seq=tiles_per_seq),
        grid=(n_tok // tm, nf),
        in_specs=in_specs,
        out_specs=pl.BlockSpec((tm, D_MODEL), lambda i, f: (i, 0)),
        out_shape=jax.ShapeDtypeStruct((n_tok, D_MODEL), F32),
        scratch_shapes=scratch,
        compiler_params=_params(("parallel", "arbitrary")),
        name="ffn",
    )(*args)


def _grid_pos_tables(n_tokens):
    quarter = D_MODEL // 4
    freq = 1.0 / (10000.0 ** (jnp.arange(quarter, dtype=F32) / quarter))
    er = jnp.arange(n_tokens // GRID_W, dtype=F32)[:, None] * freq
    ec = jnp.arange(GRID_W, dtype=F32)[:, None] * freq
    return (jnp.concatenate([jnp.sin(er), jnp.cos(er)], -1),
            jnp.concatenate([jnp.sin(ec), jnp.cos(ec)], -1))


def _split_w_in(w_in, b_gate):
    n_gate = 2 * N_DIR * N_HEADS
    w_main = jnp.concatenate([w_in[:, :4 * D_MLSTM], w_in[:, 4 * D_MLSTM + n_gate:]], axis=1).astype(BF16)
    gw = w_in[:, 4 * D_MLSTM:4 * D_MLSTM + n_gate].reshape(D_MODEL, N_DIR, 2, N_HEADS)
    gw = gw.transpose(0, 3, 1, 2).reshape(D_MODEL, n_gate)
    gw = jnp.pad(gw, ((0, 0), (0, GATE_LANES - n_gate))).astype(BF16)
    gb = b_gate.astype(F32).reshape(N_DIR, 2, N_HEADS).transpose(2, 0, 1).reshape(1, n_gate)
    gb = jnp.pad(gb, ((0, 0), (0, GATE_LANES - n_gate)))
    return w_main, gw, gb


def _layer(x2d, pos, mod6, weights, batch, seq_len, per_batch_mod, state, emit_state):
    (w_main, w_gate, b_gate, w_hnorm, w_br_m, w_br_f, w_out, ln1_g, ln1_b,
     w_up, w_conv, b_conv, w_down, ln2_g, ln2_b) = weights
    p_main, gates = _inproj(x2d, pos, mod6, w_main, w_gate, b_gate, seq_len, per_batch_mod)
    ml = _mlstm(p_main, gates, w_hnorm, batch, seq_len, state=state, emit_state=emit_state)
    fr = _fourier(p_main, batch, seq_len)
    x1 = _mix(ml[0], fr, p_main, x2d, pos, mod6, w_br_m, w_br_f, w_out, ln1_g, ln1_b,
              seq_len, per_batch_mod)
    x2 = _ffn(x1, mod6, w_up, w_conv, b_conv, w_down, ln2_g, ln2_b, seq_len, per_batch_mod)
    return x2, ml[1:]


def kernel(x_prompt, x_sample, c, state_C, state_n, state_m, c_ctx, w_ada, b_ada, w_in, b_gate,
           w_hnorm, w_br_m, w_br_f, w_out, ln1_g, ln1_b, w_up, w_conv, b_conv, w_down, ln2_g, ln2_b):
    assert w_ada.shape[0] == DEPTH
    B, S, _ = x_prompt.shape
    DB, DS, _ = x_sample.shape
    l = 0
    n_cond = 16
    cond = jnp.zeros((n_cond, D_MODEL), F32).at[0].set(c_ctx).at[1:1 + DB].set(c)
    mod6 = _modulation(cond, w_ada[l], b_ada[l]).reshape(n_cond, 6, D_MODEL)
    w_main, w_gate, b_gate_l = _split_w_in(w_in[l], b_gate[l])
    weights = (w_main, w_gate, b_gate_l, w_hnorm[l], w_br_m[l].astype(BF16), w_br_f[l].astype(BF16),
               w_out[l].astype(BF16), ln1_g[l], ln1_b[l], w_up[l].astype(BF16), w_conv[l], b_conv[l],
               w_down[l].astype(BF16), ln2_g[l], ln2_b[l])
    pos = _grid_pos_tables(DS)

    yp, states = _layer(x_prompt.reshape(B * S, D_MODEL), None, mod6[0:1], weights, B, S,
                        per_batch_mod=False, state=None, emit_state=True)
    ys, _ = _layer(x_sample.reshape(DB * DS, D_MODEL), pos, mod6[1:1 + DB], weights, DB, DS,
                   per_batch_mod=True, state=(state_C[:, l], state_n[:, l], state_m[:, l]),
                   emit_state=False)
    new_C, new_n, new_m = states
    new_n = new_n.reshape(B, DEPTH, N_DIR, N_HEADS, HEAD_DIM)
    new_m = new_m[:, :, :, 0, 0].reshape(B, DEPTH, N_DIR, N_HEADS)
    return (yp.reshape(B, S, D_MODEL), ys.reshape(DB, DS, D_MODEL), new_C, new_n, new_m)
```

```python
import functools

import jax
import jax.numpy as jnp
from jax import lax
from jax.experimental import pallas as pl
from jax.experimental.pallas import tpu as pltpu

D_MODEL = 2048
N_HEADS = 4
HEAD_DIM = 256
D_MLSTM = N_HEADS * HEAD_DIM
N_FGROUPS = 4
FGROUP_DIM = 256
D_FOURIER = N_FGROUPS * FGROUP_DIM
D_FF = 5632
GRID_W = 64
N_DIR = 2
DEPTH = 1
ALPHA = (2.0 * DEPTH) ** 0.25
LN_EPS = 1e-5

F32 = jnp.float32
BF16 = jnp.bfloat16

CHUNK = 256
COL_Q = 0
COL_K = COL_Q + D_MLSTM
COL_V = COL_K + D_MLSTM
COL_O = COL_V + D_MLSTM
COL_FR = COL_O + D_MLSTM
COL_GA = COL_FR + D_FOURIER
COL_GB = COL_GA + D_MODEL
D_MAIN = COL_GB + D_MODEL
GATE_BLOCK = 1024
GATE_LANES = 128
HALO = 8

VMEM_LIMIT = 56 * 1024 * 1024

INPROJ_TM = 1024
INPROJ_TN = 2304
MIX_TM = 512
MIX_SUB = 256
FFN_TM_LONG = 1024
FFN_TM_SHORT = 512
FFN_TF = 512
FFN_VALUE_WINDOWS = 2


def _params(sem, flags=None):
    return pltpu.CompilerParams(dimension_semantics=sem, vmem_limit_bytes=VMEM_LIMIT, flags=flags)


def _resident(shape, index_map):
    return pl.BlockSpec(shape, index_map, pipeline_mode=pl.Buffered(1))


def _norm(x):
    mu = jnp.mean(x, axis=-1, keepdims=True)
    xc = x - mu
    var = jnp.mean(xc * xc, axis=-1, keepdims=True)
    return xc * lax.rsqrt(var + LN_EPS)


def _dot(a, b):
    return jnp.dot(a, b, preferred_element_type=F32)


def _dot_nt(a, b):
    return lax.dot_general(a, b, (((1,), (1,)), ((), ())), preferred_element_type=F32)


def _mod_kernel(c_ref, w_ref, b_ref, o_ref):
    c = c_ref[...]
    s = c * jax.nn.sigmoid(c)
    o_ref[...] = _dot(s.astype(BF16), w_ref[...].astype(BF16)) + b_ref[...]


def _modulation(cond, w_ada, b_ada):
    rows, tn = cond.shape[0], 1024
    n_out = w_ada.shape[1]
    return pl.pallas_call(
        _mod_kernel,
        grid=(n_out // tn,),
        in_specs=[
            _resident((rows, D_MODEL), lambda j: (0, 0)),
            pl.BlockSpec((D_MODEL, tn), lambda j: (0, j)),
            pl.BlockSpec((1, tn), lambda j: (0, j)),
        ],
        out_specs=pl.BlockSpec((rows, tn), lambda j: (0, j)),
        out_shape=jax.ShapeDtypeStruct((rows, n_out), F32),
        compiler_params=_params(("arbitrary",)),
        name="modulation",
    )(cond, w_ada, b_ada.reshape(1, n_out))


LN_ROWS = 256


def _add_grid_pos(x, pr_ref, pc_ref, tok0):
    half = D_MODEL // 2
    out = []
    for k in range(x.shape[0] // GRID_W):
        blk = x[k * GRID_W:(k + 1) * GRID_W, :]
        pr = pr_ref[pl.ds(tok0 // GRID_W + k, 1), :]
        out.append(jnp.concatenate([blk[:, :half] + pr, blk[:, half:] + pc_ref[...]], axis=1))
    return jnp.concatenate(out, axis=0)


def _inproj_kernel(*refs, tm, tiles_per_seq, has_pos):
    if has_pos:
        x_ref, pr_ref, pc_ref, mod_ref, w_ref, wg_ref, bg_ref, p_ref, g_ref, h_scr = refs
    else:
        x_ref, mod_ref, w_ref, wg_ref, bg_ref, p_ref, g_ref, h_scr = refs

    @pl.when(pl.program_id(1) == 0)
    def _():
        sh = mod_ref[0, 0:1, :]
        sc = mod_ref[0, 1:2, :]

        def ln_rows(r, carry):
            rows = pl.ds(pl.multiple_of(r * LN_ROWS, LN_ROWS), LN_ROWS)
            x = x_ref[rows, :]
            if has_pos:
                x = _add_grid_pos(x, pr_ref, pc_ref,
                                  (pl.program_id(0) % tiles_per_seq) * tm + r * LN_ROWS)
            h = (_norm(x) * (1.0 + sc) + sh).astype(BF16)
            h_scr[rows, :] = h
            g_ref[rows, :] = _dot(h, wg_ref[...]) + bg_ref[...]
            return carry

        lax.fori_loop(0, tm // LN_ROWS, ln_rows, 0)

    p_ref[...] = _dot(h_scr[...], w_ref[...]).astype(BF16)


def _inproj(x2d, pos, mod6, w_main, w_gate, b_gate, seq_len, per_batch_mod):
    n_tok = x2d.shape[0]
    tm, tn = INPROJ_TM, INPROJ_TN
    tiles_per_seq = max(seq_len // tm, 1)
    has_pos = pos is not None
    mod_idx = (lambda i, j: (i // tiles_per_seq, 0, 0)) if per_batch_mod else (lambda i, j: (0, 0, 0))
    in_specs = [pl.BlockSpec((tm, D_MODEL), lambda i, j: (i, 0))]
    args = [x2d]
    if has_pos:
        in_specs += [_resident(p.shape, lambda i, j: (0, 0)) for p in pos]
        args += list(pos)
    in_specs += [
        pl.BlockSpec((1, 6, D_MODEL), mod_idx),
        pl.BlockSpec((D_MODEL, tn), lambda i, j: (0, j)),
        _resident((D_MODEL, GATE_LANES), lambda i, j: (0, 0)),
        _resident((1, GATE_LANES), lambda i, j: (0, 0)),
    ]
    args += [mod6, w_main, w_gate, b_gate]
    return pl.pallas_call(
        functools.partial(_inproj_kernel, tm=tm, tiles_per_seq=tiles_per_seq, has_pos=has_pos),
        grid=(n_tok // tm, D_MAIN // tn),
        in_specs=in_specs,
        out_specs=[
            pl.BlockSpec((tm, tn), lambda i, j: (i, j)),
            pl.BlockSpec((tm, GATE_LANES), lambda i, j: (i, 0)),
        ],
        out_shape=[
            jax.ShapeDtypeStruct((n_tok, D_MAIN), BF16),
            jax.ShapeDtypeStruct((n_tok, GATE_LANES), F32),
        ],
        scratch_shapes=[pltpu.VMEM((tm, D_MODEL), BF16)],
        compiler_params=_params(("parallel", "arbitrary")),
        name="inproj",
    )(*args)


def _log_sigmoid(x):
    return jnp.minimum(x, 0.0) - jnp.log1p(jnp.exp(-jnp.abs(x)))


def _head_out(h, o, wn):
    return (jax.nn.sigmoid(o.astype(F32)) * (_norm(h) * wn)).astype(BF16)


EXT = HEAD_DIM + GATE_LANES
PASS1_GROUP = 8


def _lane_tile(x, n):
    return jnp.concatenate([x] * n, axis=1)


def _mlstm_kernel(*refs, seq_len, has_state, emit_state):
    refs = list(refs)
    q_ref, k_ref, v_ref, o_ref, g_ref, wn_ref = refs[:6]
    refs = refs[6:]
    if has_state:
        m0_ref, c0_ref, n0_ref = refs[:3]
        refs = refs[3:]
    hm_ref = refs[0]
    refs = refs[1:]
    if emit_state:
        cout_ref, nout_ref, mout_ref = refs[:3]
        refs = refs[3:]
    num_scr, row_scr, kv_scr, sc_scr, c_scr, h_scr = refs
    L = CHUNK
    nc = seq_len // L
    scale = HEAD_DIM ** -0.5
    head = pl.program_id(1)
    gate_shift = jnp.where(head == 0, 0, GATE_LANES - 2 * N_DIR * head)

    def chunk_rows(c):
        return pl.ds(pl.multiple_of(c * L, L), L)

    def load_q(rows):
        return (q_ref[rows, :].astype(F32) * scale).astype(BF16)

    t_idx = lax.broadcasted_iota(jnp.int32, (L, L), 0)
    s_idx = lax.broadcasted_iota(jnp.int32, (L, L), 1)
    hi_rows = lax.broadcasted_iota(jnp.int32, (16, L), 0) < 8

    masks = [s_idx <= t_idx, s_idx >= t_idx]
    masks_b = [jnp.where(mk, 1.0, 0.0).astype(BF16) for mk in masks]
    group = min(nc, PASS1_GROUP)

    def pass1(grp, carry):
        chunks = [grp * group + i for i in range(group)]
        base, items = [], []
        for c in chunks:
            rows = chunk_rows(c)
            qb, kb, vb = load_q(rows), k_ref[rows, :], v_ref[rows, :]
            v_ext = jnp.concatenate([vb, jnp.ones((L, GATE_LANES), BF16)], axis=1)
            s0 = _dot_nt(qb, kb)
            g_row = jnp.transpose(pltpu.roll(g_ref[rows, :], gate_shift, 1))
            h_scr[rows, :] = jnp.zeros((L, HEAD_DIM), F32)
            base.append((c, rows, kb, v_ext, s0))
            for d in range(N_DIR):
                ig_row = g_row[2 * d:2 * d + 1, :]
                lf = _log_sigmoid(g_row[2 * d + 1:2 * d + 2, :])
                lf_hi = lf.astype(BF16)
                lf_lo = (lf - lf_hi.astype(F32)).astype(BF16)
                lhs = jnp.where(hi_rows, lf_hi.astype(F32), lf_lo.astype(F32)).astype(BF16)
                r16 = _dot_nt(lhs, masks_b[d])
                lf_rep = jnp.concatenate([jnp.broadcast_to(lf_hi, (GATE_LANES, L)),
                                          jnp.broadcast_to(lf_lo, (GATE_LANES, L))], axis=0)
                bb = _dot_nt(masks_b[d], lf_rep)
                items.append((len(base) - 1, d, ig_row, lf, r16, bb))
        states = []
        for bi, d, ig_row, lf, r16, bb in items:
            c, rows, kb, v_ext, s0 = base[bi]
            b_row = r16[0:1, :] + r16[8:9, :]
            b_rep = bb[:, :GATE_LANES] + bb[:, GATE_LANES:]
            c_row = ig_row - b_row
            c_max = jnp.broadcast_to(
                jnp.max(jnp.where(masks[d], c_row, -jnp.inf), axis=1, keepdims=True), (L, L))
            w = jnp.exp(jnp.where(masks[d], c_row - c_max, -jnp.inf))
            pv = _dot((s0 * w).astype(BF16), v_ext)
            num_scr[d, rows, :] = pv[:, :HEAD_DIM]
            row_scr[d, 0, rows, :] = pv[:, HEAD_DIM:]
            row_scr[d, 1, rows, :] = b_rep + c_max[:, :GATE_LANES]
            row_scr[d, 2, rows, :] = b_rep
            states.append((bi, d, ig_row, lf, b_row))
        k_ts = [jnp.transpose(kb.astype(F32)) for (_, _, kb, _, _) in base]
        for bi, d, ig_row, lf, b_row in states:
            c, rows, kb, v_ext, s0 = base[bi]
            b_last = jnp.sum(lf, axis=1, keepdims=True)
            g = b_last - b_row + ig_row
            g_max = jnp.max(g, axis=1, keepdims=True)
            wk = jnp.exp(g - g_max)
            kv_scr[d, c] = _dot((k_ts[bi] * wk).astype(BF16), v_ext)
            sc_scr[d, c, 0] = jnp.broadcast_to(b_last, (8, GATE_LANES))
            sc_scr[d, c, 1] = jnp.broadcast_to(g_max, (8, GATE_LANES))
        return carry

    lax.fori_loop(0, nc // group, pass1, 0)

    ms = []
    for d in range(N_DIR):
        if has_state:
            bh = pl.program_id(0) * (N_DIR * N_HEADS) + d * N_HEADS + head
            n_rep = jnp.transpose(jnp.broadcast_to(n0_ref[0, d, 0], (GATE_LANES, HEAD_DIM)))
            c_scr[d] = jnp.concatenate([c0_ref[0, d, 0], n_rep], axis=1)
            ms.append(jnp.full((1, GATE_LANES), m0_ref[bh], F32))
        else:
            c_scr[d] = jnp.zeros((HEAD_DIM, EXT), F32)
            ms.append(jnp.zeros((1, GATE_LANES), F32))

    def pass2(j, carry):
        new = []
        for d, m in enumerate(carry):
            c = j if d == 0 else nc - 1 - j
            rows = chunk_rows(c)
            c_ext = c_scr[d]
            qcn = _dot(load_q(rows), c_ext.astype(BF16))
            den_i, a_rep, b_rep = row_scr[d, 0, rows, :], row_scr[d, 1, rows, :], row_scr[d, 2, rows, :]
            m_rep = jnp.maximum(b_rep + m, a_rep)
            r_intra = jnp.exp(a_rep - m_rep)
            r_state = jnp.exp(b_rep + m - m_rep)
            den = r_intra * den_i + r_state * qcn[:, HEAD_DIM:]
            inv = 1.0 / jnp.maximum(jnp.abs(den), jnp.exp(-m_rep))
            h = (_lane_tile(r_intra * inv, 2) * num_scr[d, rows, :]
                 + _lane_tile(r_state * inv, 2) * qcn[:, :HEAD_DIM])
            h_scr[rows, :] += h
            b_last, g_max = sc_scr[d, c, 0][0:1, :], sc_scr[d, c, 1][0:1, :]
            m_new = jnp.maximum(b_last + m, g_max)
            decay = jnp.exp(b_last + m - m_new)
            gain = jnp.exp(g_max - m_new)
            c_scr[d] = _lane_tile(decay, 3) * c_ext + _lane_tile(gain, 3) * kv_scr[d, c]
            new.append(m_new)
        return tuple(new)

    ms = lax.fori_loop(0, nc, pass2, tuple(ms))

    wn = wn_ref[0]

    def finish(c, carry):
        rows = chunk_rows(c)
        hm_ref[rows, :] = _head_out(h_scr[rows, :], o_ref[rows, :], wn)
        return carry

    lax.fori_loop(0, nc, finish, 0)
    if emit_state:
        for d in range(N_DIR):
            c_ext = c_scr[d]
            cout_ref[0, 0, d, 0] = c_ext[:, :HEAD_DIM]
            nout_ref[0, 0, d, 0] = jnp.transpose(c_ext[:, HEAD_DIM:])[0:1, :]
            mout_ref[0, d, 0] = ms[d]


def _mlstm(p_main, gates, w_hnorm, batch, seq_len, state=None, emit_state=False):
    n_tok = p_main.shape[0]
    T = seq_len
    blk = lambda col: pl.BlockSpec((T, HEAD_DIM), lambda b, h: (b, col // HEAD_DIM + h))
    in_specs = [blk(COL_Q), blk(COL_K), blk(COL_V), blk(COL_O),
                pl.BlockSpec((T, GATE_LANES), lambda b, h: (b, 0)),
                pl.BlockSpec((1, 1, HEAD_DIM), lambda b, h: (h, 0, 0))]
    args = [p_main, p_main, p_main, p_main, gates, w_hnorm.reshape(N_HEADS, 1, HEAD_DIM)]
    has_state = state is not None
    if has_state:
        C0, n0, m0 = state
        in_specs += [
            pl.BlockSpec(memory_space=pltpu.SMEM),
            pl.BlockSpec((1, N_DIR, 1, HEAD_DIM, HEAD_DIM), lambda b, h: (b, 0, h, 0, 0)),
            pl.BlockSpec((1, N_DIR, 1, 1, HEAD_DIM), lambda b, h: (b, 0, h, 0, 0)),
        ]
        args += [m0.reshape(-1), C0, n0.reshape(batch, N_DIR, N_HEADS, 1, HEAD_DIM)]
    out_specs = [pl.BlockSpec((T, HEAD_DIM), lambda b, h: (b, h))]
    out_shape = [jax.ShapeDtypeStruct((n_tok, D_MLSTM), BF16)]
    if emit_state:
        out_specs += [
            pl.BlockSpec((1, 1, N_DIR, 1, HEAD_DIM, HEAD_DIM), lambda b, h: (b, 0, 0, h, 0, 0)),
            pl.BlockSpec((1, 1, N_DIR, 1, 1, HEAD_DIM), lambda b, h: (b, 0, 0, h, 0, 0)),
            pl.BlockSpec((1, N_DIR, 1, 1, GATE_LANES), lambda b, h: (b, 0, h, 0, 0)),
        ]
        out_shape += [
            jax.ShapeDtypeStruct((batch, DEPTH, N_DIR, N_HEADS, HEAD_DIM, HEAD_DIM), F32),
            jax.ShapeDtypeStruct((batch, DEPTH, N_DIR, N_HEADS, 1, HEAD_DIM), F32),
            jax.ShapeDtypeStruct((batch, N_DIR, N_HEADS, 1, GATE_LANES), F32),
        ]
    nc = T // CHUNK
    scratch = [pltpu.VMEM((N_DIR, T, HEAD_DIM), F32),
               pltpu.VMEM((N_DIR, 3, T, GATE_LANES), F32),
               pltpu.VMEM((N_DIR, nc, HEAD_DIM, EXT), F32),
               pltpu.VMEM((N_DIR, nc, 2, 8, GATE_LANES), F32),
               pltpu.VMEM((N_DIR, HEAD_DIM, EXT), F32),
               pltpu.VMEM((T, HEAD_DIM), F32)]
    return pl.pallas_call(
        functools.partial(_mlstm_kernel, seq_len=T, has_state=has_state, emit_state=emit_state),
        grid=(batch, N_HEADS),
        in_specs=in_specs,
        out_specs=out_specs,
        out_shape=out_shape,
        scratch_shapes=scratch,
        compiler_params=_params(("parallel", "parallel")),
        name="mlstm",
    )(*args)


def _fourier_kernel(x_ref, cs_ref, ct_ref, st_ref, o_ref, *, scale):
    for g in range(N_FGROUPS):
        cols = slice(g * FGROUP_DIM, (g + 1) * FGROUP_DIM)
        z = _dot(x_ref[:, cols], cs_ref[...])
        zc = z[:, :FGROUP_DIM].astype(BF16)
        zs = z[:, FGROUP_DIM:].astype(BF16)
        y = _dot(ct_ref[...], zc) - _dot(st_ref[...], zs)
        o_ref[:, cols] = (y * scale).astype(BF16)


def _dft_tables(n):
    k = jnp.arange(n, dtype=jnp.int32)
    ang = ((k[:, None] * k[None, :]) % n).astype(F32) * (2.0 * jnp.pi / n)
    return jnp.cos(ang), jnp.sin(ang)


DFT_BLOCK = 256
DFT_RADIX = 8
DFT_GROUPS = 2
HALF_LANES = 128


def _cadd(a, b):
    return a[0] + b[0], a[1] + b[1]


def _csub(a, b):
    return a[0] - b[0], a[1] - b[1]


def _mul_neg_i(a):
    return a[1], -a[0]


def _dft4(y):
    c0, c1 = _cadd(y[0], y[2]), _cadd(y[1], y[3])
    d0, d1 = _csub(y[0], y[2]), _mul_neg_i(_csub(y[1], y[3]))
    return [_cadd(c0, c1), _cadd(d0, d1), _csub(c0, c1), _csub(d0, d1)]


def _dft8(x):
    r = 0.5 ** 0.5
    a = [_cadd(x[n], x[n + 4]) for n in range(4)]
    b = [_csub(x[n], x[n + 4]) for n in range(4)]
    b[1] = ((b[1][0] + b[1][1]) * r, (b[1][1] - b[1][0]) * r)
    b[2] = _mul_neg_i(b[2])
    b[3] = ((b[3][1] - b[3][0]) * r, -(b[3][0] + b[3][1]) * r)
    even, odd = _dft4(a), _dft4(b)
    return [even[k // 2] if k % 2 == 0 else odd[k // 2] for k in range(8)]


def _fourier_long_kernel(x_ref, cs_ref, twc_ref, tws_ref, o_ref, w_scr, *, scale):
    nb = DFT_BLOCK
    cs = cs_ref[...]
    c_tab, s_tab = cs[:, :nb], cs[:, nb:]
    n_groups = x_ref.shape[1] // FGROUP_DIM
    zs = []
    for g in range(n_groups):
        cols = slice(g * FGROUP_DIM, (g + 1) * FGROUP_DIM)
        z = []
        for t1 in range(DFT_RADIX):
            zz = _dot(x_ref[t1 * nb:(t1 + 1) * nb, cols], cs)
            z.append((zz[:, :nb], -zz[:, nb:]))
        zs.append(z)
    for g, z in enumerate(zs):
        a = _dft8(z)
        for u1 in range(DFT_RADIX):
            ar, ai = a[u1]
            if u1 > 0:
                twc = twc_ref[u1 * nb:(u1 + 1) * nb, :]
                tws = tws_ref[u1 * nb:(u1 + 1) * nb, :]
                ar, ai = ar * twc + ai * tws, ai * twc - ar * tws
            y = (_dot(c_tab, ar.astype(BF16)) + _dot(s_tab, ai.astype(BF16))) * scale
            for half in range(FGROUP_DIM // HALF_LANES):
                w_scr[g, half, pl.ds(u1, nb, stride=DFT_RADIX), :] = (
                    y[:, half * HALF_LANES:(half + 1) * HALF_LANES])
    for g in range(n_groups):
        for half in range(FGROUP_DIM // HALF_LANES):
            lanes = slice(g * FGROUP_DIM + half * HALF_LANES, g * FGROUP_DIM + (half + 1) * HALF_LANES)
            o_ref[:, lanes] = w_scr[g, half].astype(BF16)


def _fourier_long(p_main, batch, seq_len):
    n_tok = p_main.shape[0]
    T = seq_len
    assert T == DFT_RADIX * DFT_BLOCK and FGROUP_DIM == DFT_BLOCK
    cc, sc = _dft_tables(FGROUP_DIM)
    cs = jnp.concatenate([cc, sc], axis=1).astype(BF16)
    u1 = jnp.repeat(jnp.arange(DFT_RADIX, dtype=jnp.int32), DFT_BLOCK)
    t2 = jnp.tile(jnp.arange(DFT_BLOCK, dtype=jnp.int32), DFT_RADIX)
    ang = ((u1 * t2) % T).astype(F32) * (2.0 * jnp.pi / T)
    twc = jnp.broadcast_to(jnp.cos(ang)[:, None], (T, FGROUP_DIM))
    tws = jnp.broadcast_to(jnp.sin(ang)[:, None], (T, FGROUP_DIM))
    gw = DFT_GROUPS * FGROUP_DIM
    return pl.pallas_call(
        functools.partial(_fourier_long_kernel, scale=float((T * FGROUP_DIM) ** -0.5)),
        grid=(batch, N_FGROUPS // DFT_GROUPS),
        in_specs=[
            pl.BlockSpec((T, gw), lambda b, g: (b, COL_FR // gw + g)),
            _resident((FGROUP_DIM, 2 * FGROUP_DIM), lambda b, g: (0, 0)),
            _resident((T, FGROUP_DIM), lambda b, g: (0, 0)),
            _resident((T, FGROUP_DIM), lambda b, g: (0, 0)),
        ],
        out_specs=pl.BlockSpec((T, gw), lambda b, g: (b, g)),
        out_shape=jax.ShapeDtypeStruct((n_tok, D_FOURIER), BF16),
        scratch_shapes=[pltpu.VMEM((DFT_GROUPS, FGROUP_DIM // HALF_LANES, T, HALF_LANES), F32)],
        compiler_params=_params(("parallel", "parallel")),
        name="fourier_long",
    )(p_main, cs, twc, tws)


def _fourier(p_main, batch, seq_len):
    if seq_len == DFT_RADIX * DFT_BLOCK:
        return _fourier_long(p_main, batch, seq_len)
    n_tok = p_main.shape[0]
    T = seq_len
    cc, sc = _dft_tables(FGROUP_DIM)
    cs = jnp.concatenate([cc, sc], axis=1).astype(BF16)
    ct, st = _dft_tables(T)
    return pl.pallas_call(
        functools.partial(_fourier_kernel, scale=float((T * FGROUP_DIM) ** -0.5)),
        grid=(batch,),
        in_specs=[
            pl.BlockSpec((T, D_FOURIER), lambda b: (b, COL_FR // D_FOURIER)),
            _resident((FGROUP_DIM, 2 * FGROUP_DIM), lambda b: (0, 0)),
            _resident((T, T), lambda b: (0, 0)),
            _resident((T, T), lambda b: (0, 0)),
        ],
        out_specs=pl.BlockSpec((T, D_FOURIER), lambda b: (b, 0)),
        out_shape=jax.ShapeDtypeStruct((n_tok, D_FOURIER), BF16),
        compiler_params=_params(("parallel",)),
        name="fourier",
    )(p_main, cs, ct.astype(BF16), st.astype(BF16))


def _mix_kernel(*refs, tm, tiles_per_seq, has_pos):
    if has_pos:
        (hm_ref, fr_ref, ga0_ref, ga1_ref, gb0_ref, gb1_ref, x_ref, pr_ref, pc_ref, mod_ref,
         wm_ref, wf_ref, wo_ref, lg_ref, lb_ref, o_ref) = refs
    else:
        (hm_ref, fr_ref, ga0_ref, ga1_ref, gb0_ref, gb1_ref, x_ref, mod_ref,
         wm_ref, wf_ref, wo_ref, lg_ref, lb_ref, o_ref) = refs
    g1 = mod_ref[0, 2:3, :]
    for r in range(tm // MIX_SUB):
        rows = slice(r * MIX_SUB, (r + 1) * MIX_SUB)
        a = _dot(hm_ref[rows, :], wm_ref[...])
        b = _dot(fr_ref[rows, :], wf_ref[...])
        gate = lambda g0, g1_: jax.nn.sigmoid(jnp.concatenate([g0[rows, :], g1_[rows, :]], axis=1).astype(F32))
        mixed = gate(ga0_ref, ga1_ref) * a + gate(gb0_ref, gb1_ref) * b
        z = _dot(mixed.astype(BF16), wo_ref[...])
        x = x_ref[rows, :]
        if has_pos:
            x = _add_grid_pos(x, pr_ref, pc_ref, (pl.program_id(0) % tiles_per_seq) * tm + r * MIX_SUB)
        o_ref[rows, :] = _norm(ALPHA * x + g1 * z) * lg_ref[...] + lb_ref[...]


def _mix(hm, fr, p_main, x2d, pos, mod6, w_br_m, w_br_f, w_out, ln_g, ln_b, seq_len, per_batch_mod):
    n_tok = x2d.shape[0]
    tm = MIX_TM
    tiles_per_seq = max(seq_len // tm, 1)
    has_pos = pos is not None
    mod_idx = (lambda i: (i // tiles_per_seq, 0, 0)) if per_batch_mod else (lambda i: (0, 0, 0))
    in_specs = [
        pl.BlockSpec((tm, D_MLSTM), lambda i: (i, 0)),
        pl.BlockSpec((tm, D_FOURIER), lambda i: (i, 0)),
        pl.BlockSpec((tm, GATE_BLOCK), lambda i: (i, COL_GA // GATE_BLOCK)),
        pl.BlockSpec((tm, GATE_BLOCK), lambda i: (i, COL_GA // GATE_BLOCK + 1)),
        pl.BlockSpec((tm, GATE_BLOCK), lambda i: (i, COL_GB // GATE_BLOCK)),
        pl.BlockSpec((tm, GATE_BLOCK), lambda i: (i, COL_GB // GATE_BLOCK + 1)),
        pl.BlockSpec((tm, D_MODEL), lambda i: (i, 0)),
    ]
    args = [hm, fr, p_main, p_main, p_main, p_main, x2d]
    if has_pos:
        in_specs += [_resident(p.shape, lambda i: (0, 0)) for p in pos]
        args += list(pos)
    in_specs += [
        pl.BlockSpec((1, 6, D_MODEL), mod_idx),
        _resident((D_MLSTM, D_MODEL), lambda i: (0, 0)),
        _resident((D_FOURIER, D_MODEL), lambda i: (0, 0)),
        _resident((D_MODEL, D_MODEL), lambda i: (0, 0)),
        _resident((1, D_MODEL), lambda i: (0, 0)),
        _resident((1, D_MODEL), lambda i: (0, 0)),
    ]
    args += [mod6, w_br_m, w_br_f, w_out, ln_g.reshape(1, D_MODEL), ln_b.reshape(1, D_MODEL)]
    return pl.pallas_call(
        functools.partial(_mix_kernel, tm=tm, tiles_per_seq=tiles_per_seq, has_pos=has_pos),
        grid=(n_tok // tm,),
        in_specs=in_specs,
        out_specs=pl.BlockSpec((tm, D_MODEL), lambda i: (i, 0)),
        out_shape=jax.ShapeDtypeStruct((n_tok, D_MODEL), F32),
        compiler_params=_params(("parallel",)),
        name="mix",
    )(*args)


def _ffn_kernel(*refs, tm, seg_len, has_halo, tiles_per_seq):
    if has_halo:
        (x_ref, xp_ref, xn_ref, mod_ref, wv_ref, wg_ref, cwv_ref, cwg_ref, cbv_ref, cbg_ref,
         wd_ref, lg_ref, lb_ref, o_ref, h_scr) = refs
        acc_scr = o_ref
    else:
        (x_ref, mod_ref, wv_ref, wg_ref, cwv_ref, cwg_ref, cbv_ref, cbg_ref,
         wd_ref, lg_ref, lb_ref, o_ref, h_scr, acc_scr) = refs
    f = pl.program_id(1)
    n_seg = tm // seg_len
    stride = seg_len + 2 * HALO
    rows = n_seg * stride
    zeros = jnp.zeros((HALO, D_MODEL), BF16)

    @pl.when(f == 0)
    def _():
        sh = mod_ref[0, 3:4, :]
        sc = mod_ref[0, 4:5, :]
        modulate = lambda x: _norm(x) * (1.0 + sc) + sh
        for s in range(n_seg):
            for r in range(0, seg_len, LN_ROWS):
                h_scr[s * stride + HALO + r:s * stride + HALO + r + LN_ROWS, :] = (
                    modulate(x_ref[s * seg_len + r:s * seg_len + r + LN_ROWS, :]).astype(BF16))
            h_scr[s * stride:s * stride + HALO, :] = zeros
            h_scr[(s + 1) * stride - HALO:(s + 1) * stride, :] = zeros
        if has_halo:
            t = pl.program_id(0) % tiles_per_seq
            hp = jnp.where(t == 0, 0.0, modulate(xp_ref[...]))
            hn = jnp.where(t == tiles_per_seq - 1, 0.0, modulate(xn_ref[...]))
            h_scr[0:HALO, :] = hp.astype(BF16)
            h_scr[rows - HALO:rows, :] = hn.astype(BF16)
        acc_scr[...] = jnp.zeros_like(acc_scr)

    def conv(u, cw_ref, cb_ref):
        n = u.shape[0]
        prev = pltpu.roll(u, 1, 0)
        nxt = pltpu.roll(u, n - 1, 0)
        y = prev * cw_ref[0:1, :] + u * cw_ref[1:2, :] + nxt * cw_ref[2:3, :] + cb_ref[...]
        return y[HALO:n - HALO, :]

    gate = conv(_dot(h_scr[...], wg_ref[...]), cwg_ref, cbg_ref)
    gate = gate * jax.nn.sigmoid(gate)
    n_win = FFN_VALUE_WINDOWS if has_halo else 1
    win = (rows - 2 * HALO) // n_win
    uvs = [_dot(h_scr[p * win:(p + 1) * win + 2 * HALO, :], wv_ref[...]) for p in range(n_win)]
    val = jnp.concatenate([conv(u, cwv_ref, cbv_ref) for u in uvs], axis=0)
    act = (gate * val).astype(BF16)
    acc_scr[...] += _dot(act, wd_ref[...])

    @pl.when(f == pl.num_programs(1) - 1)
    def _():
        g2 = mod_ref[0, 5:6, :]
        for s in range(n_seg):
            for r in range(0, seg_len, LN_ROWS):
                x = x_ref[s * seg_len + r:s * seg_len + r + LN_ROWS, :]
                y = acc_scr[s * stride + r:s * stride + r + LN_ROWS, :]
                o_ref[s * seg_len + r:s * seg_len + r + LN_ROWS, :] = (
                    _norm(ALPHA * x + g2 * y) * lg_ref[...] + lb_ref[...])


def _ffn(x1, mod6, w_up, w_conv, b_conv, w_down, ln_g, ln_b, seq_len, per_batch_mod):
    n_tok = x1.shape[0]
    tf = FFN_TF
    nf = D_FF // tf
    tm = FFN_TM_LONG if seq_len >= FFN_TM_LONG else FFN_TM_SHORT
    seg_len = min(seq_len, tm)
    tiles_per_seq = max(seq_len // tm, 1)
    has_halo = seq_len > tm
    h_rows = (tm // seg_len) * (seg_len + 2 * HALO)
    mod_idx = (lambda i, f: (i // tiles_per_seq, 0, 0)) if per_batch_mod else (lambda i, f: (0, 0, 0))
    hb = tm // HALO
    n_hblk = n_tok // HALO
    in_specs = [pl.BlockSpec((tm, D_MODEL), lambda i, f: (i, 0))]
    args = [x1]
    scratch = [pltpu.VMEM((h_rows, D_MODEL), BF16)]
    if has_halo:
        in_specs = [_resident((tm, D_MODEL), lambda i, f: (i, 0))]
        in_specs += [
            pl.BlockSpec((HALO, D_MODEL), lambda i, f: (jnp.maximum(i * hb - 1, 0), 0)),
            pl.BlockSpec((HALO, D_MODEL), lambda i, f: (jnp.minimum((i + 1) * hb, n_hblk - 1), 0)),
        ]
        args += [x1, x1]
    else:
        scratch.append(pltpu.VMEM((h_rows - 2 * HALO, D_MODEL), F32))
    in_specs += [
        pl.BlockSpec((1, 6, D_MODEL), mod_idx),
        pl.BlockSpec((D_MODEL, tf), lambda i, f: (0, f)),
        pl.BlockSpec((D_MODEL, tf), lambda i, f: (0, nf + f)),
        pl.BlockSpec((3, tf), lambda i, f: (0, f)),
        pl.BlockSpec((3, tf), lambda i, f: (0, nf + f)),
        pl.BlockSpec((1, tf), lambda i, f: (0, f)),
        pl.BlockSpec((1, tf), lambda i, f: (0, nf + f)),
        pl.BlockSpec((tf, D_MODEL), lambda i, f: (f, 0)),
        _resident((1, D_MODEL), lambda i, f: (0, 0)),
        _resident((1, D_MODEL), lambda i, f: (0, 0)),
    ]
    b_conv2 = b_conv.reshape(1, 2 * D_FF)
    args += [mod6, w_up, w_up, w_conv, w_conv, b_conv2, b_conv2, w_down,
             ln_g.reshape(1, D_MODEL), ln_b.reshape(1, D_MODEL)]
    return pl.pallas_call(
        functools.partial(_ffn_kernel, tm=tm, seg_len=seg_len, has_halo=has_halo,
                          tiles_per_seq=tiles_per_seq),
        grid=(n_tok // tm, nf),
        in_specs=in_specs,
        out_specs=pl.BlockSpec((tm, D_MODEL), lambda i, f: (i, 0)),
        out_shape=jax.ShapeDtypeStruct((n_tok, D_MODEL), F32),
        scratch_shapes=scratch,
        compiler_params=_params(("parallel", "arbitrary")),
        name="ffn",
    )(*args)


def _grid_pos_tables(n_tokens):
    quarter = D_MODEL // 4
    freq = 1.0 / (10000.0 ** (jnp.arange(quarter, dtype=F32) / quarter))
    er = jnp.arange(n_tokens // GRID_W, dtype=F32)[:, None] * freq
    ec = jnp.arange(GRID_W, dtype=F32)[:, None] * freq
    return (jnp.concatenate([jnp.sin(er), jnp.cos(er)], -1),
            jnp.concatenate([jnp.sin(ec), jnp.cos(ec)], -1))


def _split_w_in(w_in, b_gate):
    n_gate = 2 * N_DIR * N_HEADS
    w_main = jnp.concatenate([w_in[:, :4 * D_MLSTM], w_in[:, 4 * D_MLSTM + n_gate:]], axis=1).astype(BF16)
    gw = w_in[:, 4 * D_MLSTM:4 * D_MLSTM + n_gate].reshape(D_MODEL, N_DIR, 2, N_HEADS)
    gw = gw.transpose(0, 3, 1, 2).reshape(D_MODEL, n_gate)
    gw = jnp.pad(gw, ((0, 0), (0, GATE_LANES - n_gate))).astype(BF16)
    gb = b_gate.astype(F32).reshape(N_DIR, 2, N_HEADS).transpose(2, 0, 1).reshape(1, n_gate)
    gb = jnp.pad(gb, ((0, 0), (0, GATE_LANES - n_gate)))
    return w_main, gw, gb


def _layer(x2d, pos, mod6, weights, batch, seq_len, per_batch_mod, state, emit_state):
    (w_main, w_gate, b_gate, w_hnorm, w_br_m, w_br_f, w_out, ln1_g, ln1_b,
     w_up, w_conv, b_conv, w_down, ln2_g, ln2_b) = weights
    p_main, gates = _inproj(x2d, pos, mod6, w_main, w_gate, b_gate, seq_len, per_batch_mod)
    ml = _mlstm(p_main, gates, w_hnorm, batch, seq_len, state=state, emit_state=emit_state)
    fr = _fourier(p_main, batch, seq_len)
    x1 = _mix(ml[0], fr, p_main, x2d, pos, mod6, w_br_m, w_br_f, w_out, ln1_g, ln1_b,
              seq_len, per_batch_mod)
    x2 = _ffn(x1, mod6, w_up, w_conv, b_conv, w_down, ln2_g, ln2_b, seq_len, per_batch_mod)
    return x2, ml[1:]


def kernel(x_prompt, x_sample, c, state_C, state_n, state_m, c_ctx, w_ada, b_ada, w_in, b_gate,
           w_hnorm, w_br_m, w_br_f, w_out, ln1_g, ln1_b, w_up, w_conv, b_conv, w_down, ln2_g, ln2_b):
    assert w_ada.shape[0] == DEPTH
    B, S, _ = x_prompt.shape
    DB, DS, _ = x_sample.shape
    l = 0
    n_cond = 16
    cond = jnp.zeros((n_cond, D_MODEL), F32).at[0].set(c_ctx).at[1:1 + DB].set(c)
    mod6 = _modulation(cond, w_ada[l], b_ada[l]).reshape(n_cond, 6, D_MODEL)
    w_main, w_gate, b_gate_l = _split_w_in(w_in[l], b_gate[l])
    weights = (w_main, w_gate, b_gate_l, w_hnorm[l], w_br_m[l].astype(BF16), w_br_f[l].astype(BF16),
               w_out[l].astype(BF16), ln1_g[l], ln1_b[l], w_up[l].astype(BF16), w_conv[l], b_conv[l],
               w_down[l].astype(BF16), ln2_g[l], ln2_b[l])
    pos = _grid_pos_tables(DS)

    yp, states = _layer(x_prompt.reshape(B * S, D_MODEL), None, mod6[0:1], weights, B, S,
                        per_batch_mod=False, state=None, emit_state=True)
    ys, _ = _layer(x_sample.reshape(DB * DS, D_MODEL), pos, mod6[1:1 + DB], weights, DB, DS,
                   per_batch_mod=True, state=(state_C[:, l], state_n[:, l], state_m[:, l]),
                   emit_state=False)
    new_C, new_n, new_m = states
    new_n = new_n.reshape(B, DEPTH, N_DIR, N_HEADS, HEAD_DIM)
    new_m = new_m[:, :, :, 0, 0].reshape(B, DEPTH, N_DIR, N_HEADS)
    return (yp.reshape(B, S, D_MODEL), ys.reshape(DB, DS, D_MODEL), new_C, new_n, new_m)
```
